```python
import jax, jax.numpy as jnp
from jax import lax
import numpy as np

D_MODEL = 2048
BATCH = 4
SEQ = 2048
DEPTH = 4
DEC_BATCH = 128
DEC_SEQ = 1
PAST_LEN = 16384
PAGE_SIZE = 128

CONV_W = 4
RG_W = D_MODEL // 2
RG_BLOCKS = 8
RG_BS = RG_W // RG_BLOCKS
RG_C = 8.0
HG_HEADS = 8
HG_DK = 128
HG_DV = 128
HG_W = HG_HEADS * HG_DK
HG_SCALE = HG_DK ** -0.5
GD_HEADS = 8
GD_DK = 128
GD_DV = 128
GD_W = GD_HEADS * GD_DK
GD_SCALE = GD_DK ** -0.5
N_BRANCH = 3
D_FF = 4 * D_MODEL
CHUNK = 64
EPS = 1e-6
IN_SIZES = (RG_W, RG_W, HG_W, HG_W, HG_HEADS * HG_DV, HG_HEADS * HG_DV,
            GD_W, GD_W, GD_HEADS * GD_DV, GD_HEADS * GD_DV, GD_HEADS, GD_HEADS,
            N_BRANCH * D_MODEL)
N_IN = sum(IN_SIZES)

kernel_name = "hybrid_rglru_hgrn2_gdn_decoder_step"


def rms_norm(x, w):
    xf = x.astype(jnp.float32)
    y = xf * lax.rsqrt(jnp.mean(xf * xf, axis=-1, keepdims=True) + EPS)
    return (y * w.astype(jnp.float32)).astype(x.dtype)


def gated_rms_norm(o, w, z):
    o = o * lax.rsqrt(jnp.mean(o * o, axis=-1, keepdims=True) + EPS) * w.astype(jnp.float32)
    return o * jax.nn.silu(z)


def l2_norm(x):
    return x * lax.rsqrt(jnp.sum(x * x, axis=-1, keepdims=True) + EPS)


def split_cols(p):
    out = []
    start = 0
    for n in IN_SIZES:
        out.append(p[..., start:start + n])
        start += n
    return out


def heads(x, h):
    return x.reshape(x.shape[0], x.shape[1], h, -1)


def causal_conv(x, buf, w):
    T = x.shape[1]
    xp = jnp.concatenate([buf, x], axis=1)
    y = sum(w[j].astype(jnp.float32) * xp[:, j:j + T] for j in range(CONV_W))
    return y, xp[:, -(CONV_W - 1):]


def _chunk_len(T):
    return CHUNK if T % CHUNK == 0 else T


def _to_chunks(x, C):
    B, T, H, d = x.shape
    return x.reshape(B, T // C, C, H, d).transpose(1, 0, 3, 2, 4)


def _to_chunks_h(x, C):
    B, T, H = x.shape
    return x.reshape(B, T // C, C, H).transpose(1, 0, 3, 2)


def _from_chunks(x):
    N, B, H, C, d = x.shape
    return x.transpose(1, 0, 3, 2, 4).reshape(B, N * C, H, d)


def chunk_gla(q, k, v, logf, S0):
    C = _chunk_len(q.shape[1])
    tril = jnp.tril(jnp.ones((C, C), dtype=bool))

    def step(S, xs):
        qc, kc, vc, gc = xs
        G = jnp.cumsum(gc, axis=2)
        G_last = G[:, :, -1]
        rel = G[:, :, :, None, :] - G[:, :, None, :, :]
        decay = jnp.exp(jnp.where(tril[:, :, None], rel, -jnp.inf))
        A = jnp.einsum('bhtd,bhsd,bhtsd->bhts', qc, kc, decay)
        o = (jnp.einsum('bhtd,bhdv->bhtv', qc * jnp.exp(G), S)
             + jnp.einsum('bhts,bhsv->bhtv', A, vc))
        k_dec = kc * jnp.exp(G_last[:, :, None, :] - G)
        S = jnp.exp(G_last)[..., None] * S + jnp.einsum('bhsd,bhsv->bhdv', k_dec, vc)
        return S, o

    S, o = lax.scan(step, S0, (_to_chunks(q, C), _to_chunks(k, C), _to_chunks(v, C), _to_chunks(logf, C)))
    return _from_chunks(o), S


def chunk_gated_delta(q, k, v, g, beta, S0):
    C = _chunk_len(q.shape[1])
    tril = jnp.tril(jnp.ones((C, C), dtype=bool))
    strict = jnp.tril(jnp.ones((C, C), dtype=bool), -1)
    eye = jnp.eye(C, dtype=jnp.float32)

    def step(S, xs):
        qc, kc, vc, gc, bc = xs
        G = jnp.cumsum(gc, axis=-1)
        G_last = G[..., -1]
        decay = jnp.exp(jnp.where(tril, G[..., :, None] - G[..., None, :], -jnp.inf))
        kb = kc * bc[..., None]
        L = jnp.where(strict, jnp.einsum('bhtd,bhsd->bhts', kb, kc) * decay, 0.0) + eye
        rhs = jnp.concatenate([vc * bc[..., None], kb * jnp.exp(G)[..., None]], axis=-1)
        sol = lax.linalg.triangular_solve(L, rhs, left_side=True, lower=True)
        u, w = sol[..., :GD_DV], sol[..., GD_DV:]
        v_new = u - jnp.einsum('bhtd,bhdv->bhtv', w, S)
        qk = jnp.where(tril, jnp.einsum('bhtd,bhsd->bhts', qc, kc) * decay, 0.0)
        o = (jnp.einsum('bhtd,bhdv->bhtv', qc * jnp.exp(G)[..., None], S)
             + jnp.einsum('bhts,bhsv->bhtv', qk, v_new))
        k_dec = kc * jnp.exp(G_last[..., None] - G)[..., None]
        S = jnp.exp(G_last)[..., None, None] * S + jnp.einsum('bhsd,bhsv->bhdv', k_dec, v_new)
        return S, o

    S, o = lax.scan(step, S0, (_to_chunks(q, C), _to_chunks(k, C), _to_chunks(v, C),
                               _to_chunks_h(g, C), _to_chunks_h(beta, C)))
    return _from_chunks(o), S


def _lru_combine(left, right):
    a1, b1 = left
    a2, b2 = right
    return a1 * a2, a2 * b1 + b2


def rg_lru_branch(xr, gate, pos, h0, conv_buf, lp):
    B, T, _ = xr.shape
    xc, conv_new = causal_conv(xr, conv_buf, lp['rg_conv_w'])
    xc = xc + lp['rg_conv_b'].astype(jnp.float32)
    xb = xc.reshape(B, T, RG_BLOCKS, RG_BS)
    r = jax.nn.sigmoid(jnp.einsum('btni,nij->btnj', xb, lp['rg_wa'].astype(jnp.float32)).reshape(B, T, RG_W)
                       + lp['rg_ba'].astype(jnp.float32))
    i = jax.nn.sigmoid(jnp.einsum('btni,nij->btnj', xb, lp['rg_wx'].astype(jnp.float32)).reshape(B, T, RG_W)
                       + lp['rg_bx'].astype(jnp.float32))
    log_a = -RG_C * r * jax.nn.softplus(-lp['rg_a_param'].astype(jnp.float32))
    a = jnp.exp(log_a)
    mult = jnp.where((pos == 0)[None, :, None], 1.0, jnp.sqrt(-jnp.expm1(2.0 * log_a)))
    b = mult * (i * xc)
    b = b.at[:, 0].add(a[:, 0] * h0)
    _, h = lax.associative_scan(_lru_combine, (a, b), axis=1)
    y = h * jax.nn.gelu(gate, approximate=True)
    return y, h[:, -1], conv_new


def hgrn2_branch(q, fx, i, gx, lb, S0, lp):
    B, T, _ = q.shape
    lb = lb.astype(jnp.float32)
    f = lb + (1.0 - lb) * jax.nn.sigmoid(fx)
    logf = jnp.log(f)
    k = (1.0 - lb) * jax.nn.sigmoid(-fx)
    o, S = chunk_gla(heads(q, HG_HEADS) * HG_SCALE, heads(k, HG_HEADS), heads(i, HG_HEADS),
                     heads(logf, HG_HEADS), S0)
    o = gated_rms_norm(o, lp['hg_norm_w'], heads(gx, HG_HEADS))
    return o.reshape(B, T, HG_HEADS * HG_DV), S


def gdn_branch(q, k, v, z, a, b, conv_buf, S0, lp):
    B, T, _ = q.shape
    qkv, conv_new = causal_conv(jnp.concatenate([q, k, v], axis=-1), conv_buf, lp['gd_conv_w'])
    qkv = jax.nn.silu(qkv)
    qc, kc, vc = qkv[..., :GD_W], qkv[..., GD_W:2 * GD_W], qkv[..., 2 * GD_W:]
    qh = l2_norm(heads(qc, GD_HEADS)) * GD_SCALE
    kh = l2_norm(heads(kc, GD_HEADS))
    vh = heads(vc, GD_HEADS)
    g = -jnp.exp(lp['gd_A_log'].astype(jnp.float32)) * jax.nn.softplus(a + lp['gd_dt_bias'].astype(jnp.float32))
    beta = jax.nn.sigmoid(b)
    o, S = chunk_gated_delta(qh, kh, vh, g, beta, S0)
    o = gated_rms_norm(o, lp['gd_norm_w'], heads(z, GD_HEADS))
    return o.reshape(B, T, GD_HEADS * GD_DV), S, conv_new


def trunk_layer(x, pos, state, lp, lb):
    dt = x.dtype
    B, T, _ = x.shape
    rg_h, rg_conv, hg_S, gd_S, gd_conv = state
    u = rms_norm(x, lp['norm_mix_w'])
    proj = (u @ lp['w_in']).astype(jnp.float32)
    (rg_x, rg_gate, hg_q, hg_f, hg_i, hg_g, gd_q, gd_k, gd_v, gd_z, gd_a, gd_b, merge) = split_cols(proj)
    y_rg, rg_h, rg_conv = rg_lru_branch(rg_x, rg_gate, pos, rg_h, rg_conv, lp)
    y_hg, hg_S = hgrn2_branch(hg_q, hg_f, hg_i, hg_g, lb, hg_S, lp)
    y_gd, gd_S, gd_conv = gdn_branch(gd_q, gd_k, gd_v, gd_z, gd_a, gd_b, gd_conv, gd_S, lp)
    gates = jax.nn.sigmoid(merge).reshape(B, T, N_BRANCH, D_MODEL)
    mixed = (gates[:, :, 0] * (y_rg.astype(dt) @ lp['w_br_rg']).astype(jnp.float32)
             + gates[:, :, 1] * (y_hg.astype(dt) @ lp['w_br_hg']).astype(jnp.float32)
             + gates[:, :, 2] * (y_gd.astype(dt) @ lp['w_br_gd']).astype(jnp.float32))
    x = x + mixed.astype(dt) @ lp['w_out']
    hmid = rms_norm(x, lp['norm_mlp_w'])
    x = x + jnp.square(jax.nn.relu(hmid @ lp['w_up'])) @ lp['w_down']
    return x, (rg_h, rg_conv, hg_S, gd_S, gd_conv)


def setup_inputs(seed: int = 0) -> dict:
    key = jax.random.key(seed)
    ks = jax.random.split(key, 32)
    f32 = jnp.float32

    def nrm(k, shape, scale):
        return jax.random.normal(k, shape, f32) * scale

    a0 = jax.random.uniform(ks[17], (DEPTH, RG_W), f32, minval=0.9, maxval=0.999)
    dt0 = jnp.exp(jax.random.uniform(ks[21], (DEPTH, GD_HEADS), f32, minval=np.log(1e-3), maxval=np.log(1e-1)))
    return {
        'x_prompt': nrm(ks[0], (BATCH, SEQ, D_MODEL), 1.0),
        'x_sample': nrm(ks[1], (DEC_BATCH, DEC_SEQ, D_MODEL), 1.0),
        'state_rg_h': nrm(ks[2], (DEPTH, DEC_BATCH, RG_W), 0.5),
        'state_rg_conv': nrm(ks[3], (DEPTH, DEC_BATCH, CONV_W - 1, RG_W), 1.0),
        'state_hg_S': nrm(ks[4], (DEPTH, DEC_BATCH, HG_HEADS, HG_DK, HG_DV), 0.5),
        'state_gd_S': nrm(ks[5], (DEPTH, DEC_BATCH, GD_HEADS, GD_DK, GD_DV), 0.1),
        'state_gd_conv': nrm(ks[6], (DEPTH, DEC_BATCH, CONV_W - 1, 3 * GD_W), 1.0),
        'norm_mix_w': 1.0 + nrm(ks[7], (DEPTH, D_MODEL), 0.01),
        'norm_mlp_w': 1.0 + nrm(ks[8], (DEPTH, D_MODEL), 0.01),
        'norm_final_w': 1.0 + nrm(ks[9], (D_MODEL,), 0.01),
        'w_in': nrm(ks[10], (DEPTH, D_MODEL, N_IN), D_MODEL ** -0.5),
        'rg_conv_w': nrm(ks[11], (DEPTH, CONV_W, RG_W), CONV_W ** -0.5),
        'rg_conv_b': nrm(ks[12], (DEPTH, RG_W), 0.01),
        'rg_wa': nrm(ks[13], (DEPTH, RG_BLOCKS, RG_BS, RG_BS), RG_BS ** -0.5),
        'rg_ba': nrm(ks[14], (DEPTH, RG_W), 0.01),
        'rg_wx': nrm(ks[15], (DEPTH, RG_BLOCKS, RG_BS, RG_BS), RG_BS ** -0.5),
        'rg_bx': nrm(ks[16], (DEPTH, RG_W), 0.01),
        'rg_a_param': jnp.log(a0) - jnp.log1p(-a0),
        'hg_lb_logits': nrm(ks[18], (DEPTH, HG_W), 0.5),
        'hg_norm_w': 1.0 + nrm(ks[19], (DEPTH, HG_DV), 0.01),
        'gd_conv_w': nrm(ks[20], (DEPTH, CONV_W, 3 * GD_W), CONV_W ** -0.5),
        'gd_A_log': jnp.log(jax.random.uniform(ks[22], (DEPTH, GD_HEADS), f32, minval=1.0, maxval=16.0)),
        'gd_dt_bias': dt0 + jnp.log(-jnp.expm1(-dt0)),
        'gd_norm_w': 1.0 + nrm(ks[23], (DEPTH, GD_DV), 0.01),
        'w_br_rg': nrm(ks[24], (DEPTH, RG_W, D_MODEL), RG_W ** -0.5),
        'w_br_hg': nrm(ks[25], (DEPTH, HG_HEADS * HG_DV, D_MODEL), (HG_HEADS * HG_DV) ** -0.5),
        'w_br_gd': nrm(ks[26], (DEPTH, GD_HEADS * GD_DV, D_MODEL), (GD_HEADS * GD_DV) ** -0.5),
        'w_out': nrm(ks[27], (DEPTH, D_MODEL, D_MODEL), D_MODEL ** -0.5),
        'w_up': nrm(ks[28], (DEPTH, D_MODEL, D_FF), D_MODEL ** -0.5),
        'w_down': nrm(ks[29], (DEPTH, D_FF, D_MODEL), D_FF ** -0.5),
    }


def reference(x_prompt, x_sample, state_rg_h, state_rg_conv, state_hg_S, state_gd_S, state_gd_conv,
              norm_mix_w, norm_mlp_w, norm_final_w, w_in, rg_conv_w, rg_conv_b, rg_wa, rg_ba, rg_wx, rg_bx,
              rg_a_param, hg_lb_logits, hg_norm_w, gd_conv_w, gd_A_log, gd_dt_bias, gd_norm_w,
              w_br_rg, w_br_hg, w_br_gd, w_out, w_up, w_down):
    f32 = jnp.float32
    lb_all = jnp.cumsum(jax.nn.softmax(hg_lb_logits.astype(f32), axis=0), axis=0)
    lb_all = lb_all - lb_all[0:1]

    pos_p = jnp.arange(SEQ)
    pos_s = PAST_LEN + jnp.arange(DEC_SEQ)
    xp, xs = x_prompt, x_sample
    p_states = []
    s_states = []
    for l in range(DEPTH):
        lp = {
            'norm_mix_w': norm_mix_w[l], 'norm_mlp_w': norm_mlp_w[l], 'w_in': w_in[l],
            'rg_conv_w': rg_conv_w[l], 'rg_conv_b': rg_conv_b[l], 'rg_wa': rg_wa[l], 'rg_ba': rg_ba[l],
            'rg_wx': rg_wx[l], 'rg_bx': rg_bx[l], 'rg_a_param': rg_a_param[l], 'hg_norm_w': hg_norm_w[l],
            'gd_conv_w': gd_conv_w[l], 'gd_A_log': gd_A_log[l], 'gd_dt_bias': gd_dt_bias[l],
            'gd_norm_w': gd_norm_w[l], 'w_br_rg': w_br_rg[l], 'w_br_hg': w_br_hg[l], 'w_br_gd': w_br_gd[l],
            'w_out': w_out[l], 'w_up': w_up[l], 'w_down': w_down[l],
        }
        st_p0 = (jnp.zeros((BATCH, RG_W), f32),
                 jnp.zeros((BATCH, CONV_W - 1, RG_W), f32),
                 jnp.zeros((BATCH, HG_HEADS, HG_DK, HG_DV), f32),
                 jnp.zeros((BATCH, GD_HEADS, GD_DK, GD_DV), f32),
                 jnp.zeros((BATCH, CONV_W - 1, 3 * GD_W), f32))
        st_s0 = (state_rg_h[l].astype(f32), state_rg_conv[l].astype(f32), state_hg_S[l].astype(f32),
                 state_gd_S[l].astype(f32), state_gd_conv[l].astype(f32))
        xp, st_p = trunk_layer(xp, pos_p, st_p0, lp, lb_all[l])
        xs, st_s = trunk_layer(xs, pos_s, st_s0, lp, lb_all[l])
        p_states.append(st_p)
        s_states.append(st_s)

    y_prompt = rms_norm(xp, norm_final_w)
    y_sample = rms_norm(xs, norm_final_w)

    def stack(sts, j, like):
        return jnp.stack([s[j] for s in sts], axis=0).astype(like.dtype)

    return (y_prompt, y_sample,
            stack(p_states, 0, state_rg_h), stack(p_states, 1, state_rg_conv), stack(p_states, 2, state_hg_S),
            stack(p_states, 3, state_gd_S), stack(p_states, 4, state_gd_conv),
            stack(s_states, 0, state_rg_h), stack(s_states, 1, state_rg_conv), stack(s_states, 2, state_hg_S),
            stack(s_states, 3, state_gd_S), stack(s_states, 4, state_gd_conv))
```

```python
import functools

import jax
import jax.numpy as jnp
from jax import lax
from jax.experimental import pallas as pl
from jax.experimental.pallas import tpu as pltpu

f32 = jnp.float32
bf16 = jnp.bfloat16

EPS = 1e-6
RG_C = 8.0
HEAD = 128
LANE = 128
CHUNK = 64
SUB = 16
CONV_W = 4
VMEM_LIMIT = 56 * 1024 * 1024

_NT = (((1,), (1,)), ((), ()))
_TN = (((0,), (0,)), ((), ()))


def _params(*sem):
    return pltpu.CompilerParams(dimension_semantics=sem, vmem_limit_bytes=VMEM_LIMIT)


def _pick(n, cands):
    for c in cands:
        if n % c == 0:
            return c
    raise ValueError(f"no tile for {n} among {cands}")


def _mm(a, b, dims=None):
    a = a.astype(bf16)
    b = b.astype(bf16)
    if dims is None:
        return jnp.dot(a, b, preferred_element_type=f32)
    return lax.dot_general(a, b, dims, preferred_element_type=f32)


def _split3(x):
    hi = x.astype(bf16)
    r = x - hi.astype(f32)
    mid = r.astype(bf16)
    lo = (r - mid.astype(f32)).astype(bf16)
    return hi, mid, lo


def _mm_exact_lhs(a_bf16, x):
    hi, mid, lo = _split3(x)
    return (jnp.dot(a_bf16, hi, preferred_element_type=f32) + jnp.dot(a_bf16, mid, preferred_element_type=f32)
            + jnp.dot(a_bf16, lo, preferred_element_type=f32))


def _mm_hi(a, b):
    ah, am, _ = _split3(a)
    bh, bm, _ = _split3(b)
    d = functools.partial(jnp.dot, preferred_element_type=f32)
    return d(ah, bh) + (d(ah, bm) + d(am, bh))


def _expm1_neg(x, ex):
    return -jnp.tanh(0.5 * x) * (ex + 1.0)


def _softplus(x):
    return jnp.maximum(x, 0.0) + jnp.log1p(jnp.exp(-jnp.abs(x)))


def _silu(x):
    return x * jax.nn.sigmoid(x)


def _gated_rms(o, w, z):
    o = o * lax.rsqrt(jnp.mean(o * o, axis=-1, keepdims=True) + EPS) * w
    return o * _silu(z)


def _l2norm(x):
    return x * lax.rsqrt(jnp.sum(x * x, axis=-1, keepdims=True) + EPS)


def _tril_mask(n, strict=False):
    r = lax.broadcasted_iota(jnp.int32, (n, n), 0)
    c = lax.broadcasted_iota(jnp.int32, (n, n), 1)
    return (r > c) if strict else (r >= c)


def _norm_body(has_delta, emit_x, *refs):
    refs = list(refs)
    x_ref = refs.pop(0)
    d_ref = refs.pop(0) if has_delta else None
    w_ref = refs.pop(0)
    xo_ref = refs.pop(0) if emit_x else None
    n_ref = refs.pop(0)
    x = x_ref[...]
    if has_delta:
        x = x + d_ref[...]
    if emit_x:
        xo_ref[...] = x
    y = x * lax.rsqrt(jnp.mean(x * x, axis=-1, keepdims=True) + EPS)
    n_ref[...] = (y * w_ref[...]).astype(n_ref.dtype)


def _add_norm(x, delta, w_row, out_dtype, emit_x):
    M, D = x.shape
    tm = _pick(M, (416, 320, 256, 128, 64, 16))
    row = pl.BlockSpec((tm, D), lambda m: (m, 0))
    in_specs = [row] + ([row] if delta is not None else []) + [pl.BlockSpec((1, D), lambda m: (0, 0))]
    out_shape = ([jax.ShapeDtypeStruct((M, D), f32)] if emit_x else []) + [jax.ShapeDtypeStruct((M, D), out_dtype)]
    out_specs = ([row] if emit_x else []) + [row]
    args = [x] + ([delta] if delta is not None else []) + [w_row]
    out = pl.pallas_call(
        functools.partial(_norm_body, delta is not None, emit_x),
        grid=(M // tm,), in_specs=in_specs, out_specs=out_specs, out_shape=out_shape,
        compiler_params=_params("arbitrary"), name="add_norm")(*args)
    return out if emit_x else (None, out[0])


def _gemm_wres_body(epi, a_ref, w_ref, o_ref, wb):
    @pl.when(pl.program_id(1) == 0)
    def _():
        wb[...] = w_ref[...].astype(bf16)
    acc = jnp.dot(a_ref[...], wb[...], preferred_element_type=f32)
    if epi is not None:
        acc = epi(acc)
    o_ref[...] = acc.astype(o_ref.dtype)


def _gemm_wres(a, w, layer, col_block0, n_out, tn, epi=None, out_dtype=f32, name="gemm"):
    M, K = a.shape
    tm = _pick(M, (640, 512, 320, 256, 128, 64, 16))
    return pl.pallas_call(
        functools.partial(_gemm_wres_body, epi),
        grid=(n_out // tn, M // tm),
        in_specs=[pl.BlockSpec((tm, K), lambda n, m: (m, 0)),
                  pl.BlockSpec((None, K, tn), lambda n, m: (layer, 0, n + col_block0))],
        out_specs=pl.BlockSpec((tm, tn), lambda n, m: (m, n)),
        out_shape=jax.ShapeDtypeStruct((M, n_out), out_dtype),
        scratch_shapes=[pltpu.VMEM((K, tn), bf16)],
        compiler_params=_params("arbitrary", "arbitrary"), name=name)(a, w)


def _gemm_ksplit_body(rt, a_ref, w_ref, o_ref, wb):
    k = pl.program_id(1)
    wb[...] = w_ref[...].astype(bf16)
    tm = a_ref.shape[0]

    def rows(r, c):
        r0 = pl.multiple_of(r * rt, rt)
        p = jnp.dot(a_ref[pl.ds(r0, rt), :], wb[...], preferred_element_type=f32)

        @pl.when(k == 0)
        def _():
            o_ref[pl.ds(r0, rt), :] = p

        @pl.when(k > 0)
        def _():
            o_ref[pl.ds(r0, rt), :] += p
        return c

    lax.fori_loop(0, tm // rt, rows, 0)


def _gemm_ksplit(a, w, layer, name="gemm_ksplit"):
    M, K = a.shape
    N = w.shape[-1]
    tm = _pick(M, (2080, 1280, 640, 320, 128, 64, 16))
    rt = _pick(tm, (416, 320, 256, 128, 64, 16))
    tk = _pick(K, (256, 128))
    return pl.pallas_call(
        functools.partial(_gemm_ksplit_body, rt),
        grid=(M // tm, K // tk),
        in_specs=[pl.BlockSpec((tm, tk), lambda m, k: (m, k)),
                  pl.BlockSpec((None, tk, N), lambda m, k: (layer, k, 0))],
        out_specs=pl.BlockSpec((tm, N), lambda m, k: (m, 0)),
        out_shape=jax.ShapeDtypeStruct((M, N), f32),
        scratch_shapes=[pltpu.VMEM((tk, N), bf16)],
        compiler_params=_params("arbitrary", "arbitrary"), name=name)(a, w)


MERGE_SHIFT = 16


def _mix_body(u_ref, yr_ref, yh_ref, yg_ref, wm0, wm1, wm2, wx0, wx1, wx2, wr_ref, wh_ref, wg_ref, o_ref, wmb, wbb):
    tn = o_ref.shape[1]

    @pl.when(pl.program_id(1) == 0)
    def _():
        for b, (wm, wx) in enumerate(((wm0, wx0), (wm1, wx1), (wm2, wx2))):
            wcat = jnp.concatenate([wm[...], wx[...]], axis=1)
            wmb[b] = wcat[:, MERGE_SHIFT:MERGE_SHIFT + tn].astype(bf16)
        for b, wr in enumerate((wr_ref, wh_ref, wg_ref)):
            wbb[b] = wr[...].astype(bf16)

    u = u_ref[...]
    acc = None
    for b, y_ref in enumerate((yr_ref, yh_ref, yg_ref)):
        gate = jax.nn.sigmoid(jnp.dot(u, wmb[b], preferred_element_type=f32))
        p = jnp.dot(y_ref[...], wbb[b], preferred_element_type=f32)
        acc = gate * p if acc is None else acc + gate * p
    o_ref[...] = acc.astype(o_ref.dtype)


def _mix(u, y_rg, y_hg, y_gd, w_in, w_br_rg, w_br_hg, w_br_gd, layer, merge_col0):
    M, D = u.shape
    W = y_rg.shape[1]
    tn = 256
    tm = _pick(M, (416, 320, 256, 128, 64, 16))
    nt = D // tn
    assert (merge_col0 - MERGE_SHIFT) % tn == 0
    base = (merge_col0 - MERGE_SHIFT) // tn
    r = tn // LANE

    def wm_spec(b):
        return pl.BlockSpec((None, D, tn), lambda n, m: (layer, 0, base + b * nt + n))

    def wx_spec(b):
        return pl.BlockSpec((None, D, LANE), lambda n, m: (layer, 0, (base + b * nt + n + 1) * r))

    row = lambda w: pl.BlockSpec((tm, w), lambda n, m: (m, 0))
    br = pl.BlockSpec((None, W, tn), lambda n, m: (layer, 0, n))
    return pl.pallas_call(
        _mix_body,
        grid=(nt, M // tm),
        in_specs=[row(D), row(W), row(W), row(W), wm_spec(0), wm_spec(1), wm_spec(2),
                  wx_spec(0), wx_spec(1), wx_spec(2), br, br, br],
        out_specs=pl.BlockSpec((tm, tn), lambda n, m: (m, n)),
        out_shape=jax.ShapeDtypeStruct((M, D), bf16),
        scratch_shapes=[pltpu.VMEM((3, D, tn), bf16), pltpu.VMEM((3, W, tn), bf16)],
        compiler_params=_params("arbitrary", "arbitrary"), name="mix")(
            u, y_rg, y_hg, y_gd, w_in, w_in, w_in, w_in, w_in, w_in, w_br_rg, w_br_hg, w_br_gd)


def _lb_body(x_ref, o_ref):
    x = x_ref[...]
    depth = x.shape[0]
    m = jnp.max(x, axis=0, keepdims=True)
    e = jnp.exp(x - m)
    p = e / jnp.sum(e, axis=0, keepdims=True)
    acc = jnp.zeros_like(p[0:1])
    o_ref[0:1, :] = acc
    for l in range(1, depth):
        acc = acc + p[l:l + 1]
        o_ref[l:l + 1, :] = acc


def _lower_bounds(logits):
    return pl.pallas_call(_lb_body, out_shape=jax.ShapeDtypeStruct(logits.shape, f32), name="hg_lower_bounds")(logits)


def _rg_gates(xc, wa, wx, ba, bx, sp):
    xb = xc.astype(bf16)
    r = jax.nn.sigmoid(jnp.dot(xb, wa.astype(bf16), preferred_element_type=f32) + ba)
    i = jax.nn.sigmoid(jnp.dot(xb, wx.astype(bf16), preferred_element_type=f32) + bx)
    log_a = (-RG_C) * r * sp
    a = jnp.exp(log_a)
    mult = jnp.sqrt(_expm1_neg(2.0 * log_a, a * a))
    return a, mult, i


def _rg_prompt_body(x_ref, gate_ref, cw_ref, cb_ref, wa_ref, wx_ref, ba_ref, bx_ref, ap_ref, y_ref, h_ref, xbuf, hprev):
    c = pl.program_id(1)
    tc = x_ref.shape[0]
    nblk = x_ref.shape[1] // HEAD

    @pl.when(c == 0)
    def _():
        xbuf[0:8, :] = jnp.zeros((8, xbuf.shape[1]), f32)
        hprev[...] = jnp.zeros_like(hprev)

    xbuf[8:8 + tc, :] = x_ref[...]
    row = lax.broadcasted_iota(jnp.int32, (tc, HEAD), 0)
    first = jnp.logical_and(row == 0, c == 0)
    for n in range(nblk):
        ls = slice(n * HEAD, (n + 1) * HEAD)
        xc = cb_ref[:, ls]
        for j in range(CONV_W):
            xc = xc + cw_ref[j:j + 1, ls] * xbuf[pl.ds(8 - (CONV_W - 1) + j, tc), ls]
        sp = _softplus(-ap_ref[:, ls])
        a, mult, i = _rg_gates(xc, wa_ref[n], wx_ref[n], ba_ref[:, ls], bx_ref[:, ls], sp)
        mult = jnp.where(first, 1.0, mult)
        b = mult * (i * xc)
        s = 1
        while s < tc:
            keep = row >= s
            a_sh = jnp.where(keep, pltpu.roll(a, s, axis=0), 1.0)
            b_sh = jnp.where(keep, pltpu.roll(b, s, axis=0), 0.0)
            b = a * b_sh + b
            a = a * a_sh
            s *= 2
        h = b + a * hprev[:, ls]
        hprev[:, ls] = h[tc - 1:tc, :]
        y_ref[:, ls] = (h * jax.nn.gelu(gate_ref[:, ls], approximate=True)).astype(y_ref.dtype)
    xbuf[0:8, :] = xbuf[tc:tc + 8, :]
    h_ref[...] = hprev[...]


def _rg_prompt(proj, B, T, M, lp, layer):
    W = lp["rg_ba"].shape[-1]
    tc = _pick(T, (256, 128, 64))
    nT = T // tc
    nblk = W // HEAD
    vec = pl.BlockSpec((None, 1, W), lambda b, c: (layer, 0, 0))
    blk = pl.BlockSpec((None, nblk, HEAD, HEAD), lambda b, c: (layer, 0, 0, 0))
    y, h = pl.pallas_call(
        _rg_prompt_body,
        grid=(B, nT),
        in_specs=[pl.BlockSpec((tc, W), lambda b, c: (b * nT + c, 0)),
                  pl.BlockSpec((tc, W), lambda b, c: (b * nT + c, 1)),
                  pl.BlockSpec((None, CONV_W, W), lambda b, c: (layer, 0, 0)),
                  vec, blk, blk, vec, vec, vec],
        out_specs=[pl.BlockSpec((tc, W), lambda b, c: (b * nT + c, 0)),
                   pl.BlockSpec((None, 1, W), lambda b, c: (b, 0, 0))],
        out_shape=[jax.ShapeDtypeStruct((M, W), bf16), jax.ShapeDtypeStruct((B, 1, W), f32)],
        scratch_shapes=[pltpu.VMEM((8 + tc, W), f32), pltpu.VMEM((1, W), f32)],
        compiler_params=_params("arbitrary", "arbitrary"), name="rg_prompt")(
            proj, proj, lp["rg_conv_w"], lp["rg_conv_b"], lp["rg_wa"], lp["rg_wx"], lp["rg_ba"], lp["rg_bx"],
            lp["rg_a_param"])
    return y, h[:, 0]


def _rg_sample_body(x_ref, gate_ref, cs_ref, h0_ref, cw_ref, cb_ref, wa_ref, wx_ref, ba_ref, bx_ref, ap_ref,
                    yin_ref, y_ref, h_ref):
    del yin_ref
    nblk = x_ref.shape[1] // HEAD
    for n in range(nblk):
        ls = slice(n * HEAD, (n + 1) * HEAD)
        xc = cb_ref[:, ls] + cw_ref[CONV_W - 1:CONV_W, ls] * x_ref[:, ls]
        for j in range(CONV_W - 1):
            xc = xc + cw_ref[j:j + 1, ls] * cs_ref[j, :, ls]
        sp = _softplus(-ap_ref[:, ls])
        a, mult, i = _rg_gates(xc, wa_ref[n], wx_ref[n], ba_ref[:, ls], bx_ref[:, ls], sp)
        h = a * h0_ref[:, ls] + mult * (i * xc)
        h_ref[:, ls] = h
        y_ref[:, ls] = (h * jax.nn.gelu(gate_ref[:, ls], approximate=True)).astype(y_ref.dtype)


def _rg_sample(proj, y_all, row0, DB, conv_state_t, h0, lp, layer):
    W = lp["rg_ba"].shape[-1]
    nblk = W // HEAD
    assert row0 % DB == 0
    rb = row0 // DB
    vec = pl.BlockSpec((None, 1, W), lambda i: (layer, 0, 0))
    blk = pl.BlockSpec((None, nblk, HEAD, HEAD), lambda i: (layer, 0, 0, 0))
    y, h = pl.pallas_call(
        _rg_sample_body,
        grid=(1,),
        in_specs=[pl.BlockSpec((DB, W), lambda i: (rb, 0)),
                  pl.BlockSpec((DB, W), lambda i: (rb, 1)),
                  pl.BlockSpec((None, CONV_W - 1, DB, W), lambda i: (layer, 0, 0, 0)),
                  pl.BlockSpec((None, DB, W), lambda i: (layer, 0, 0)),
                  pl.BlockSpec((None, CONV_W, W), lambda i: (layer, 0, 0)),
                  vec, blk, blk, vec, vec, vec,
                  pl.BlockSpec(memory_space=pl.ANY)],
        out_specs=[pl.BlockSpec((DB, W), lambda i: (rb, 0)),
                   pl.BlockSpec((DB, W), lambda i: (0, 0))],
        out_shape=[jax.ShapeDtypeStruct(y_all.shape, y_all.dtype), jax.ShapeDtypeStruct((DB, W), f32)],
        input_output_aliases={11: 0},
        compiler_params=_params("arbitrary"), name="rg_sample")(
            proj, proj, conv_state_t, h0, lp["rg_conv_w"], lp["rg_conv_b"], lp["rg_wa"], lp["rg_wx"],
            lp["rg_ba"], lp["rg_bx"], lp["rg_a_param"], y_all)
    return y, h


def _hg_gates(fx, lb):
    f = lb + (1.0 - lb) * jax.nn.sigmoid(fx)
    k = (1.0 - lb) * jax.nn.sigmoid(-fx)
    return f, k


def _hg_prompt_body(scale, q_ref, f_ref, i_ref, g_ref, lb_ref, nw_ref, y_ref, s_ref, st):
    c = pl.program_id(2)
    nchunk = q_ref.shape[0] // CHUNK
    nsub = CHUNK // SUB

    @pl.when(c == 0)
    def _():
        st[...] = jnp.zeros_like(st)

    lb = lb_ref[...]
    nw = nw_ref[...]
    tril = jnp.where(_tril_mask(CHUNK), 1.0, 0.0).astype(bf16)
    trow = lax.broadcasted_iota(jnp.int32, (SUB, HEAD), 0)
    srow = lax.broadcasted_iota(jnp.int32, (CHUNK, HEAD), 0)
    lane = lax.broadcasted_iota(jnp.int32, (SUB, CHUNK), 1)

    def chunk(ci, carry):
        r0 = pl.multiple_of(ci * CHUNK, CHUNK)
        rows = pl.ds(r0, CHUNK)
        f, k = _hg_gates(f_ref[rows, :], lb)
        q = q_ref[rows, :] * scale
        v = i_ref[rows, :]
        G = _mm_exact_lhs(tril, jnp.log(f))
        a_rows = []
        for i in range(nsub):
            sl = slice(i * SUB, (i + 1) * SUB)
            g_i, q_i, k_i = G[sl], q[sl], k[sl]
            a_d = jnp.zeros((SUB, CHUNK), f32)
            for s in range(SUB):
                e = jnp.where(trow >= s, jnp.exp(g_i - g_i[s:s + 1, :]), 0.0)
                col = jnp.sum(q_i * k_i[s:s + 1, :] * e, axis=-1, keepdims=True)
                a_d = jnp.where(lane == i * SUB + s, col, a_d)
            if i > 0:
                g_b = G[i * SUB - 1:i * SUB, :]
                q_t = q_i * jnp.exp(g_i - g_b)
                k_t = jnp.where(srow < i * SUB, k * jnp.exp(g_b - G), 0.0)
                a_d = a_d + _mm(q_t, k_t, _NT)
            a_rows.append(a_d)
        A = jnp.concatenate(a_rows, axis=0)
        st_old = st[...]
        o = _mm(A, v) + _mm(q * jnp.exp(G), st_old, _NT)
        g_last = G[CHUNK - 1:CHUNK, :]
        k_dec = k * jnp.exp(g_last - G)
        st[...] = st_old * jnp.exp(g_last) + _mm(v, k_dec, _TN)
        y_ref[rows, :] = _gated_rms(o, nw, g_ref[rows, :]).astype(y_ref.dtype)
        return carry

    lax.fori_loop(0, nchunk, chunk, 0)

    @pl.when(c == pl.num_programs(2) - 1)
    def _():
        s_ref[...] = st[...].T


def _hg_prompt(proj, B, T, M, H, col0, lb, norm_w, layer):
    tc = _pick(T, (256, 128, 64))
    nT = T // tc
    col = lambda j: pl.BlockSpec((tc, HEAD), lambda b, h, c: (b * nT + c, col0 + j * H + h))
    y, S = pl.pallas_call(
        functools.partial(_hg_prompt_body, HEAD ** -0.5),
        grid=(B, H, nT),
        in_specs=[col(0), col(1), col(2), col(3),
                  pl.BlockSpec((None, 1, HEAD), lambda b, h, c: (layer, 0, h)),
                  pl.BlockSpec((None, 1, HEAD), lambda b, h, c: (layer, 0, 0))],
        out_specs=[pl.BlockSpec((tc, HEAD), lambda b, h, c: (b * nT + c, h)),
                   pl.BlockSpec((None, None, HEAD, HEAD), lambda b, h, c: (b, h, 0, 0))],
        out_shape=[jax.ShapeDtypeStruct((M, H * HEAD), bf16), jax.ShapeDtypeStruct((B, H, HEAD, HEAD), f32)],
        scratch_shapes=[pltpu.VMEM((HEAD, HEAD), f32)],
        compiler_params=_params("arbitrary", "arbitrary", "arbitrary"), name="hg_prompt")(
            proj, proj, proj, proj, lb, norm_w)
    return y, S


SB = 16


def _state_step(s_ref, so_ref, o_scr, d_rows, kT, qT, vnew_fn):
    for j in range(SB):
        S = s_ref[j]
        kcol = kT[:, j:j + 1]
        d, vnew = vnew_fn(j, S, kcol)
        Sn = d * S + kcol * vnew
        so_ref[j] = Sn
        o_scr[j:j + 1, :] = jnp.sum(qT[:, j:j + 1] * Sn, axis=0, keepdims=True)


def _hg_sample_body(scale, q_ref, f_ref, i_ref, g_ref, lb_ref, nw_ref, s_ref, yin_ref, y_ref, so_ref, o_scr):
    del yin_ref
    f, k = _hg_gates(f_ref[...], lb_ref[...])
    q = q_ref[...] * scale
    v = i_ref[...]
    fT, kT, qT = f.T, k.T, q.T

    def vnew(j, S, kcol):
        return fT[:, j:j + 1], v[j:j + 1, :]

    _state_step(s_ref, so_ref, o_scr, None, kT, qT, vnew)
    y_ref[...] = _gated_rms(o_scr[...], nw_ref[...], g_ref[...]).astype(y_ref.dtype)


def _hg_sample(proj, y_all, row0, DB, H, col0, lb, norm_w, state, layer):
    assert row0 % SB == 0 and DB % SB == 0
    rb = row0 // SB
    col = lambda j: pl.BlockSpec((SB, HEAD), lambda h, b: (rb + b, col0 + j * H + h))
    y, S = pl.pallas_call(
        functools.partial(_hg_sample_body, HEAD ** -0.5),
        grid=(H, DB // SB),
        in_specs=[col(0), col(1), col(2), col(3),
                  pl.BlockSpec((None, 1, HEAD), lambda h, b: (layer, 0, h)),
                  pl.BlockSpec((None, 1, HEAD), lambda h, b: (layer, 0, 0)),
                  pl.BlockSpec((None, SB, None, HEAD, HEAD), lambda h, b: (layer, b, h, 0, 0)),
                  pl.BlockSpec(memory_space=pl.ANY)],
        out_specs=[pl.BlockSpec((SB, HEAD), lambda h, b: (rb + b, h)),
                   pl.BlockSpec((SB, None, HEAD, HEAD), lambda h, b: (b, h, 0, 0))],
        out_shape=[jax.ShapeDtypeStruct(y_all.shape, y_all.dtype), jax.ShapeDtypeStruct((DB, H, HEAD, HEAD), f32)],
        scratch_shapes=[pltpu.VMEM((SB, HEAD), f32)],
        input_output_aliases={7: 0},
        compiler_params=_params("arbitrary", "arbitrary"), name="hg_sample")(
            proj, proj, proj, proj, lb, norm_w, state, y_all)
    return y, S


def _pick_lane(x, idx):
    lane = lax.broadcasted_iota(jnp.int32, x.shape, 1)
    col = jnp.sum(jnp.where(lane == idx, x, 0.0), axis=1, keepdims=True)
    return jnp.broadcast_to(col, (x.shape[0], HEAD))


def _gd_gates(ab, alog_row, dtb_row, h, H):
    a = _pick_lane(ab, h)
    b = _pick_lane(ab, H + h)
    alog = _pick_lane(alog_row, h)
    dtb = _pick_lane(dtb_row, h)
    g = -jnp.exp(alog) * _softplus(a + dtb)
    return g, jax.nn.sigmoid(b)


def _gd_prompt_body(scale, H, q_ref, k_ref, v_ref, z_ref, ab_ref, cwq_ref, cwk_ref, cwv_ref, alog_ref, dtb_ref, nw_ref,
                    y_ref, s_ref, xq, xk, xv, qs, ks, vs, gs, bs, S_scr):
    h = pl.program_id(1)
    c = pl.program_id(2)
    tc = q_ref.shape[0]
    nchunk = tc // CHUNK

    @pl.when(c == 0)
    def _():
        for xb in (xq, xk, xv):
            xb[0:8, :] = jnp.zeros((8, HEAD), f32)
        S_scr[...] = jnp.zeros_like(S_scr)

    def conv_silu(x_ref, xb, cw_ref):
        xb[8:8 + tc, :] = x_ref[...]
        y = cw_ref[0:1, :] * xb[pl.ds(8 - (CONV_W - 1), tc), :]
        for j in range(1, CONV_W):
            y = y + cw_ref[j:j + 1, :] * xb[pl.ds(8 - (CONV_W - 1) + j, tc), :]
        xb[0:8, :] = xb[tc:tc + 8, :]
        return _silu(y)

    qs[...] = _l2norm(conv_silu(q_ref, xq, cwq_ref)) * scale
    ks[...] = _l2norm(conv_silu(k_ref, xk, cwk_ref))
    vs[...] = conv_silu(v_ref, xv, cwv_ref)
    g, beta = _gd_gates(ab_ref[...], alog_ref[...], dtb_ref[...], h, H)
    gs[...] = g
    bs[...] = beta

    nw = nw_ref[...]
    tril_b = _tril_mask(CHUNK)
    strict_b = _tril_mask(CHUNK, strict=True)
    tril = jnp.where(tril_b, 1.0, 0.0).astype(bf16)
    eye = (lax.broadcasted_iota(jnp.int32, (CHUNK, CHUNK), 0)
           == lax.broadcasted_iota(jnp.int32, (CHUNK, CHUNK), 1)).astype(f32)

    def chunk(ci, carry):
        r0 = pl.multiple_of(ci * CHUNK, CHUNK)
        rows = pl.ds(r0, CHUNK)
        q, k, v, beta = qs[rows, :], ks[rows, :], vs[rows, :], bs[rows, :]
        G = _mm_exact_lhs(tril, gs[rows, :])
        g_col = G[:, :CHUNK]
        g_row = G.T[:CHUNK, :]
        decay = jnp.where(tril_b, jnp.exp(g_col - g_row), 0.0)
        kb = k * beta
        N = jnp.where(strict_b, -(_mm(kb, k, _NT) * decay), 0.0)
        Tm = eye + N
        P = N
        p = 2
        while p < CHUNK:
            P = _mm_hi(P, P)
            Tm = Tm + _mm_hi(Tm, P)
            p *= 2
        eG = jnp.exp(G)
        u = _mm_hi(Tm, v * beta)
        w = _mm_hi(Tm, kb * eG)
        S = S_scr[...]
        v_new = u - _mm(w, S)
        qk = jnp.where(tril_b, _mm(q, k, _NT) * decay, 0.0)
        o = _mm(q * eG, S) + _mm(qk, v_new)
        g_last = G[CHUNK - 1:CHUNK, :]
        k_dec = k * jnp.exp(g_last - G)
        S_scr[...] = jnp.exp(g_last) * S + _mm(k_dec, v_new, _TN)
        y_ref[rows, :] = _gated_rms(o, nw, z_ref[rows, :]).astype(y_ref.dtype)
        return carry

    lax.fori_loop(0, nchunk, chunk, 0)

    @pl.when(c == pl.num_programs(2) - 1)
    def _():
        s_ref[...] = S_scr[...]


def _gd_prompt(proj, pab, B, T, M, H, col0, lp, layer):
    tc = _pick(T, (256, 128, 64))
    nT = T // tc
    col = lambda j: pl.BlockSpec((tc, HEAD), lambda b, h, c: (b * nT + c, col0 + j * H + h))
    cw = lambda j: pl.BlockSpec((None, CONV_W, HEAD), lambda b, h, c: (layer, 0, j * H + h))
    small = pl.BlockSpec((None, 1, H), lambda b, h, c: (layer, 0, 0))
    buf = pltpu.VMEM((8 + tc, HEAD), f32)
    tile = pltpu.VMEM((tc, HEAD), f32)
    y, S = pl.pallas_call(
        functools.partial(_gd_prompt_body, HEAD ** -0.5, H),
        grid=(B, H, nT),
        in_specs=[col(0), col(1), col(2), col(3),
                  pl.BlockSpec((tc, LANE), lambda b, h, c: (b * nT + c, 0)),
                  cw(0), cw(1), cw(2), small, small,
                  pl.BlockSpec((None, 1, HEAD), lambda b, h, c: (layer, 0, 0))],
        out_specs=[pl.BlockSpec((tc, HEAD), lambda b, h, c: (b * nT + c, h)),
                   pl.BlockSpec((None, None, HEAD, HEAD), lambda b, h, c: (b, h, 0, 0))],
        out_shape=[jax.ShapeDtypeStruct((M, H * HEAD), bf16), jax.ShapeDtypeStruct((B, H, HEAD, HEAD), f32)],
        scratch_shapes=[buf, buf, buf, tile, tile, tile, tile, tile, pltpu.VMEM((HEAD, HEAD), f32)],
        compiler_params=_params("arbitrary", "arbitrary", "arbitrary"), name="gd_prompt")(
            proj, proj, proj, proj, pab, lp["gd_conv_w"], lp["gd_conv_w"], lp["gd_conv_w"],
            lp["gd_A_log"], lp["gd_dt_bias"], lp["gd_norm_w"])
    return y, S


def _gd_sample_body(scale, H, q_ref, k_ref, v_ref, z_ref, ab_ref, csq_ref, csk_ref, csv_ref, cwq_ref, cwk_ref, cwv_ref,
                    alog_ref, dtb_ref, nw_ref, s_ref, yin_ref, y_ref, so_ref, o_scr):
    del yin_ref
    h = pl.program_id(0)

    def conv_silu(x_ref, cs_ref, cw_ref):
        y = cw_ref[CONV_W - 1:CONV_W, :] * x_ref[...]
        for j in range(CONV_W - 1):
            y = y + cw_ref[j:j + 1, :] * cs_ref[j]
        return _silu(y)

    q = _l2norm(conv_silu(q_ref, csq_ref, cwq_ref)) * scale
    k = _l2norm(conv_silu(k_ref, csk_ref, cwk_ref))
    v = conv_silu(v_ref, csv_ref, cwv_ref)
    g, beta = _gd_gates(ab_ref[...], alog_ref[...], dtb_ref[...], h, H)
    eg = jnp.exp(g)
    kT, qT = k.T, q.T

    def vnew(j, S, kcol):
        egj = eg[j:j + 1, :]
        kS = jnp.sum(kcol * S, axis=0, keepdims=True)
        return egj, beta[j:j + 1, :] * (v[j:j + 1, :] - egj * kS)

    _state_step(s_ref, so_ref, o_scr, None, kT, qT, vnew)
    y_ref[...] = _gated_rms(o_scr[...], nw_ref[...], z_ref[...]).astype(y_ref.dtype)


def _gd_sample(proj, pab, y_all, row0, DB, H, col0, conv_state_t, state, lp, layer):
    assert row0 % SB == 0 and DB % SB == 0
    rb = row0 // SB
    col = lambda j: pl.BlockSpec((SB, HEAD), lambda h, b: (rb + b, col0 + j * H + h))
    cs = lambda j: pl.BlockSpec((None, CONV_W - 1, SB, HEAD), lambda h, b: (layer, 0, b, j * H + h))
    cw = lambda j: pl.BlockSpec((None, CONV_W, HEAD), lambda h, b: (layer, 0, j * H + h))
    small = pl.BlockSpec((None, 1, H), lambda h, b: (layer, 0, 0))
    y, S = pl.pallas_call(
        functools.partial(_gd_sample_body, HEAD ** -0.5, H),
        grid=(H, DB // SB),
        in_specs=[col(0), col(1), col(2), col(3),
                  pl.BlockSpec((SB, LANE), lambda h, b: (rb + b, 0)),
                  cs(0), cs(1), cs(2), cw(0), cw(1), cw(2), small, small,
                  pl.BlockSpec((None, 1, HEAD), lambda h, b: (layer, 0, 0)),
                  pl.BlockSpec((None, SB, None, HEAD, HEAD), lambda h, b: (layer, b, h, 0, 0)),
                  pl.BlockSpec(memory_space=pl.ANY)],
        out_specs=[pl.BlockSpec((SB, HEAD), lambda h, b: (rb + b, h)),
                   pl.BlockSpec((SB, None, HEAD, HEAD), lambda h, b: (b, h, 0, 0))],
        out_shape=[jax.ShapeDtypeStruct(y_all.shape, y_all.dtype), jax.ShapeDtypeStruct((DB, H, HEAD, HEAD), f32)],
        scratch_shapes=[pltpu.VMEM((SB, HEAD), f32)],
        input_output_aliases={15: 0},
        compiler_params=_params("arbitrary", "arbitrary"), name="gd_sample")(
            proj, proj, proj, proj, pab, conv_state_t, conv_state_t, conv_state_t,
            lp["gd_conv_w"], lp["gd_conv_w"], lp["gd_conv_w"], lp["gd_A_log"], lp["gd_dt_bias"], lp["gd_norm_w"],
            state, y_all)
    return y, S


def kernel(x_prompt, x_sample, state_rg_h, state_rg_conv, state_hg_S, state_gd_S, state_gd_conv, norm_mix_w, norm_mlp_w, norm_final_w, w_in, rg_conv_w, rg_conv_b, rg_wa, rg_ba, rg_wx, rg_bx, rg_a_param, hg_lb_logits, hg_norm_w, gd_conv_w, gd_A_log, gd_dt_bias, gd_norm_w, w_br_rg, w_br_hg, w_br_gd, w_out, w_up, w_down):
    B, T, D = x_prompt.shape
    DB, DT, _ = x_sample.shape
    assert DT == 1
    depth = w_in.shape[0]
    RW = rg_ba.shape[-1]
    H = gd_A_log.shape[-1]
    PT = B * T
    M = PT + DB
    assert RW % HEAD == 0 and hg_norm_w.shape[-1] == HEAD and gd_norm_w.shape[-1] == HEAD
    HW = H * HEAD
    n_main = 2 * RW + 8 * HW
    merge_col0 = n_main + 2 * H
    assert w_in.shape[-1] == merge_col0 + 3 * D and n_main % 1024 == 0 and n_main % LANE == 0
    hg_col0 = 2 * RW // HEAD
    gd_col0 = hg_col0 + 4 * H

    row3 = lambda a: a.reshape(depth, 1, a.shape[-1])
    lp = dict(rg_conv_w=rg_conv_w, rg_conv_b=row3(rg_conv_b), rg_wa=rg_wa, rg_wx=rg_wx, rg_ba=row3(rg_ba),
              rg_bx=row3(rg_bx), rg_a_param=row3(rg_a_param), gd_conv_w=gd_conv_w, gd_A_log=row3(gd_A_log),
              gd_dt_bias=row3(gd_dt_bias), gd_norm_w=row3(gd_norm_w))
    hg_nw = row3(hg_norm_w)
    lb = row3(_lower_bounds(hg_lb_logits.astype(f32)))
    rg_conv_t = jnp.swapaxes(state_rg_conv, 1, 2)
    gd_conv_t = jnp.swapaxes(state_gd_conv, 1, 2)

    x = jnp.concatenate([x_prompt.reshape(PT, D), x_sample.reshape(DB, D)], axis=0)
    _, u = _add_norm(x, None, norm_mix_w[0:1], bf16, emit_x=False)

    p_states, s_states = [], []
    y_final = None
    for l in range(depth):
        proj = _gemm_wres(u, w_in, l, 0, n_main, 1024, name="in_proj")
        pab = _gemm_wres(u, w_in, l, n_main // LANE, LANE, LANE, name="in_proj_ab")

        y_rg, p_h = _rg_prompt(proj, B, T, M, lp, l)
        y_rg, s_h = _rg_sample(proj, y_rg, PT, DB, rg_conv_t, state_rg_h, lp, l)
        y_hg, p_hgS = _hg_prompt(proj, B, T, M, H, hg_col0, lb, hg_nw, l)
        y_hg, s_hgS = _hg_sample(proj, y_hg, PT, DB, H, hg_col0, lb, hg_nw, state_hg_S, l)
        y_gd, p_gdS = _gd_prompt(proj, pab, B, T, M, H, gd_col0, lp, l)
        y_gd, s_gdS = _gd_sample(proj, pab, y_gd, PT, DB, H, gd_col0, gd_conv_t, state_gd_S, lp, l)

        pp = proj[:PT].reshape(B, T, -1)[:, T - (CONV_W - 1):, :]
        ps = proj[PT:]
        gq = gd_col0 * HEAD
        p_states.append((p_h, pp[..., :RW], p_hgS, p_gdS, pp[..., gq:gq + 3 * HW]))
        s_states.append((s_h,
                         jnp.concatenate([state_rg_conv[l][:, 1:], ps[:, None, :RW]], axis=1),
                         s_hgS, s_gdS,
                         jnp.concatenate([state_gd_conv[l][:, 1:], ps[:, None, gq:gq + 3 * HW]], axis=1)))

        mixed = _mix(u, y_rg, y_hg, y_gd, w_in, w_br_rg, w_br_hg, w_br_gd, l, merge_col0)
        d1 = _gemm_wres(mixed, w_out, l, 0, D, 1024, name="out_proj")
        x, hmid = _add_norm(x, d1, norm_mlp_w[l:l + 1], bf16, emit_x=True)
        hh = _gemm_wres(hmid, w_up, l, 0, w_up.shape[-1], 1024, epi=lambda a: jnp.square(jnp.maximum(a, 0.0)),
                        out_dtype=bf16, name="mlp_up")
        d2 = _gemm_ksplit(hh, w_down, l, name="mlp_down")
        if l + 1 < depth:
            x, u = _add_norm(x, d2, norm_mix_w[l + 1:l + 2], bf16, emit_x=True)
        else:
            _, y_final = _add_norm(x, d2, norm_final_w.reshape(1, D), f32, emit_x=False)

    def stack(sts, j, like):
        return jnp.stack([s[j] for s in sts], axis=0).astype(like.dtype)

    return (y_final[:PT].reshape(B, T, D), y_final[PT:].reshape(DB, DT, D),
            stack(p_states, 0, state_rg_h), stack(p_states, 1, state_rg_conv), stack(p_states, 2, state_hg_S),
            stack(p_states, 3, state_gd_S), stack(p_states, 4, state_gd_conv),
            stack(s_states, 0, state_rg_h), stack(s_states, 1, state_rg_conv), stack(s_states, 2, state_hg_S),
            stack(s_states, 3, state_gd_S), stack(s_states, 4, state_gd_conv))
```

```python
import functools

import jax
import jax.numpy as jnp
from jax import lax
from jax.experimental import pallas as pl
from jax.experimental.pallas import tpu as pltpu

f32 = jnp.float32
bf16 = jnp.bfloat16

EPS = 1e-6
RG_C = 8.0
HEAD = 128
LANE = 128
CHUNK = 64
SUB = 16
HEADS_PER_STEP = 2
CONV_W = 4
VMEM_LIMIT = 56 * 1024 * 1024

_NT = (((1,), (1,)), ((), ()))
_TN = (((0,), (0,)), ((), ()))


def _params(*sem):
    return pltpu.CompilerParams(dimension_semantics=sem, vmem_limit_bytes=VMEM_LIMIT)


def _pick(n, cands):
    for c in cands:
        if n % c == 0:
            return c
    raise ValueError(f"no tile for {n} among {cands}")


def _mm(a, b, dims=None):
    a = a.astype(bf16)
    b = b.astype(bf16)
    if dims is None:
        return jnp.dot(a, b, preferred_element_type=f32)
    return lax.dot_general(a, b, dims, preferred_element_type=f32)


def _split3(x):
    hi = x.astype(bf16)
    r = x - hi.astype(f32)
    mid = r.astype(bf16)
    lo = (r - mid.astype(f32)).astype(bf16)
    return hi, mid, lo


def _mm_exact_lhs(a_bf16, x):
    hi, mid, lo = _split3(x)
    return (jnp.dot(a_bf16, hi, preferred_element_type=f32) + jnp.dot(a_bf16, mid, preferred_element_type=f32)
            + jnp.dot(a_bf16, lo, preferred_element_type=f32))


def _mm_hi(a, b):
    ah, am, _ = _split3(a)
    bh, bm, _ = _split3(b)
    d = functools.partial(jnp.dot, preferred_element_type=f32)
    return d(ah, bh) + (d(ah, bm) + d(am, bh))


def _expm1_neg(x, ex):
    return -jnp.tanh(0.5 * x) * (ex + 1.0)


def _softplus(x):
    return jnp.maximum(x, 0.0) + jnp.log1p(jnp.exp(-jnp.abs(x)))


def _silu(x):
    return x * jax.nn.sigmoid(x)


def _gated_rms(o, w, z):
    o = o * lax.rsqrt(jnp.mean(o * o, axis=-1, keepdims=True) + EPS) * w
    return o * _silu(z)


def _l2norm(x):
    return x * lax.rsqrt(jnp.sum(x * x, axis=-1, keepdims=True) + EPS)


def _tril_mask(n, strict=False):
    r = lax.broadcasted_iota(jnp.int32, (n, n), 0)
    c = lax.broadcasted_iota(jnp.int32, (n, n), 1)
    return (r > c) if strict else (r >= c)


def _chunk_tril(n):
    r = lax.broadcasted_iota(jnp.int32, (n, n), 0)
    c = lax.broadcasted_iota(jnp.int32, (n, n), 1)
    same = (r // CHUNK) == (c // CHUNK)
    return jnp.where(jnp.logical_and(r >= c, same), 1.0, 0.0).astype(bf16)


def _norm_body(has_delta, emit_x, *refs):
    refs = list(refs)
    x_ref = refs.pop(0)
    d_ref = refs.pop(0) if has_delta else None
    w_ref = refs.pop(0)
    xo_ref = refs.pop(0) if emit_x else None
    n_ref = refs.pop(0)
    x = x_ref[...]
    if has_delta:
        x = x + d_ref[...]
    if emit_x:
        xo_ref[...] = x
    y = x * lax.rsqrt(jnp.mean(x * x, axis=-1, keepdims=True) + EPS)
    n_ref[...] = (y * w_ref[...]).astype(n_ref.dtype)


def _add_norm(x, delta, w_row, out_dtype, emit_x):
    M, D = x.shape
    tm = _pick(M, (416, 320, 256, 128, 64, 16))
    row = pl.BlockSpec((tm, D), lambda m: (m, 0))
    in_specs = [row] + ([row] if delta is not None else []) + [pl.BlockSpec((1, D), lambda m: (0, 0))]
    out_shape = ([jax.ShapeDtypeStruct((M, D), f32)] if emit_x else []) + [jax.ShapeDtypeStruct((M, D), out_dtype)]
    out_specs = ([row] if emit_x else []) + [row]
    args = [x] + ([delta] if delta is not None else []) + [w_row]
    out = pl.pallas_call(
        functools.partial(_norm_body, delta is not None, emit_x),
        grid=(M // tm,), in_specs=in_specs, out_specs=out_specs, out_shape=out_shape,
        compiler_params=_params("arbitrary"), name="add_norm")(*args)
    return out if emit_x else (None, out[0])


def _gemm_wres_body(epi, a_ref, w_ref, o_ref, wb):
    @pl.when(pl.program_id(1) == 0)
    def _():
        wb[...] = w_ref[...].astype(bf16)
    acc = jnp.dot(a_ref[...], wb[...], preferred_element_type=f32)
    if epi is not None:
        acc = epi(acc)
    o_ref[...] = acc.astype(o_ref.dtype)


def _gemm_wres(a, w, layer, col_block0, n_out, tn, epi=None, out_dtype=f32, name="gemm"):
    M, K = a.shape
    tm = _pick(M, (640, 512, 320, 256, 128, 64, 16))
    return pl.pallas_call(
        functools.partial(_gemm_wres_body, epi),
        grid=(n_out // tn, M // tm),
        in_specs=[pl.BlockSpec((tm, K), lambda n, m: (m, 0)),
                  pl.BlockSpec((None, K, tn), lambda n, m: (layer, 0, n + col_block0))],
        out_specs=pl.BlockSpec((tm, tn), lambda n, m: (m, n)),
        out_shape=jax.ShapeDtypeStruct((M, n_out), out_dtype),
        scratch_shapes=[pltpu.VMEM((K, tn), bf16)],
        compiler_params=_params("arbitrary", "arbitrary"), name=name)(a, w)


def _gemm_ksplit_body(rt, a_ref, w_ref, o_ref, wb):
    k = pl.program_id(1)
    wb[...] = w_ref[...].astype(bf16)
    tm = a_ref.shape[0]

    def rows(r, c):
        r0 = pl.multiple_of(r * rt, rt)
        p = jnp.dot(a_ref[pl.ds(r0, rt), :], wb[...], preferred_element_type=f32)

        @pl.when(k == 0)
        def _():
            o_ref[pl.ds(r0, rt), :] = p

        @pl.when(k > 0)
        def _():
            o_ref[pl.ds(r0, rt), :] += p
        return c

    lax.fori_loop(0, tm // rt, rows, 0)


def _gemm_ksplit(a, w, layer, name="gemm_ksplit"):
    M, K = a.shape
    N = w.shape[-1]
    tm = _pick(M, (1040, 640, 320, 128, 64, 16))
    rt = _pick(tm, (208, 320, 128, 64, 16))
    tk = _pick(K, (1024, 512, 256, 128))
    return pl.pallas_call(
        functools.partial(_gemm_ksplit_body, rt),
        grid=(M // tm, K // tk),
        in_specs=[pl.BlockSpec((tm, tk), lambda m, k: (m, k)),
                  pl.BlockSpec((None, tk, N), lambda m, k: (layer, k, 0))],
        out_specs=pl.BlockSpec((tm, N), lambda m, k: (m, 0)),
        out_shape=jax.ShapeDtypeStruct((M, N), f32),
        scratch_shapes=[pltpu.VMEM((tk, N), bf16)],
        compiler_params=_params("arbitrary", "arbitrary"), name=name)(a, w)


MERGE_SHIFT = 16


def _mix_body(u_ref, yr_ref, yh_ref, yg_ref, wm0, wm1, wm2, wx0, wx1, wx2, wr_ref, wh_ref, wg_ref, o_ref, wmb, wbb):
    tn = o_ref.shape[1]

    @pl.when(pl.program_id(1) == 0)
    def _():
        for b, (wm, wx) in enumerate(((wm0, wx0), (wm1, wx1), (wm2, wx2))):
            wcat = jnp.concatenate([wm[...], wx[...]], axis=1)
            wmb[b] = wcat[:, MERGE_SHIFT:MERGE_SHIFT + tn].astype(bf16)
        for b, wr in enumerate((wr_ref, wh_ref, wg_ref)):
            wbb[b] = wr[...].astype(bf16)

    u = u_ref[...]
    acc = None
    for b, y_ref in enumerate((yr_ref, yh_ref, yg_ref)):
        gate = jax.nn.sigmoid(jnp.dot(u, wmb[b], preferred_element_type=f32))
        p = jnp.dot(y_ref[...], wbb[b], preferred_element_type=f32)
        acc = gate * p if acc is None else acc + gate * p
    o_ref[...] = acc.astype(o_ref.dtype)


def _mix(u, y_rg, y_hg, y_gd, w_in, w_br_rg, w_br_hg, w_br_gd, layer, merge_col0):
    M, D = u.shape
    W = y_rg.shape[1]
    tn = 256
    tm = _pick(M, (416, 320, 256, 128, 64, 16))
    nt = D // tn
    assert (merge_col0 - MERGE_SHIFT) % tn == 0
    base = (merge_col0 - MERGE_SHIFT) // tn
    r = tn // LANE

    def wm_spec(b):
        return pl.BlockSpec((None, D, tn), lambda n, m: (layer, 0, base + b * nt + n))

    def wx_spec(b):
        return pl.BlockSpec((None, D, LANE), lambda n, m: (layer, 0, (base + b * nt + n + 1) * r))

    row = lambda w: pl.BlockSpec((tm, w), lambda n, m: (m, 0))
    br = pl.BlockSpec((None, W, tn), lambda n, m: (layer, 0, n))
    return pl.pallas_call(
        _mix_body,
        grid=(nt, M // tm),
        in_specs=[row(D), row(W), row(W), row(W), wm_spec(0), wm_spec(1), wm_spec(2),
                  wx_spec(0), wx_spec(1), wx_spec(2), br, br, br],
        out_specs=pl.BlockSpec((tm, tn), lambda n, m: (m, n)),
        out_shape=jax.ShapeDtypeStruct((M, D), bf16),
        scratch_shapes=[pltpu.VMEM((3, D, tn), bf16), pltpu.VMEM((3, W, tn), bf16)],
        compiler_params=_params("arbitrary", "arbitrary"), name="mix")(
            u, y_rg, y_hg, y_gd, w_in, w_in, w_in, w_in, w_in, w_in, w_br_rg, w_br_hg, w_br_gd)


def _lb_body(x_ref, o_ref):
    x = x_ref[...]
    depth = x.shape[0]
    m = jnp.max(x, axis=0, keepdims=True)
    e = jnp.exp(x - m)
    p = e / jnp.sum(e, axis=0, keepdims=True)
    acc = jnp.zeros_like(p[0:1])
    o_ref[0:1, :] = acc
    for l in range(1, depth):
        acc = acc + p[l:l + 1]
        o_ref[l:l + 1, :] = acc


def _lower_bounds(logits):
    return pl.pallas_call(_lb_body, out_shape=jax.ShapeDtypeStruct(logits.shape, f32), name="hg_lower_bounds")(logits)


def _rg_gates(xc, wa, wx, ba, bx, sp):
    xb = xc.astype(bf16)
    r = jax.nn.sigmoid(jnp.dot(xb, wa.astype(bf16), preferred_element_type=f32) + ba)
    i = jax.nn.sigmoid(jnp.dot(xb, wx.astype(bf16), preferred_element_type=f32) + bx)
    log_a = (-RG_C) * r * sp
    a = jnp.exp(log_a)
    mult = jnp.sqrt(_expm1_neg(2.0 * log_a, a * a))
    return a, mult, i


def _rg_prompt_body(x_ref, gate_ref, cw_ref, cb_ref, wa_ref, wx_ref, ba_ref, bx_ref, ap_ref, y_ref, h_ref, xbuf, hprev):
    c = pl.program_id(1)
    tc = x_ref.shape[0]
    nblk = x_ref.shape[1] // HEAD

    @pl.when(c == 0)
    def _():
        xbuf[0:8, :] = jnp.zeros((8, xbuf.shape[1]), f32)
        hprev[...] = jnp.zeros_like(hprev)

    xbuf[8:8 + tc, :] = x_ref[...]
    row = lax.broadcasted_iota(jnp.int32, (tc, HEAD), 0)
    first = jnp.logical_and(row == 0, c == 0)
    for n in range(nblk):
        ls = slice(n * HEAD, (n + 1) * HEAD)
        xc = cb_ref[:, ls]
        for j in range(CONV_W):
            xc = xc + cw_ref[j:j + 1, ls] * xbuf[pl.ds(8 - (CONV_W - 1) + j, tc), ls]
        sp = _softplus(-ap_ref[:, ls])
        a, mult, i = _rg_gates(xc, wa_ref[n], wx_ref[n], ba_ref[:, ls], bx_ref[:, ls], sp)
        mult = jnp.where(first, 1.0, mult)
        b = mult * (i * xc)
        s = 1
        while s < tc:
            keep = row >= s
            a_sh = jnp.where(keep, pltpu.roll(a, s, axis=0), 1.0)
            b_sh = jnp.where(keep, pltpu.roll(b, s, axis=0), 0.0)
            b = a * b_sh + b
            a = a * a_sh
            s *= 2
        h = b + a * hprev[:, ls]
        hprev[:, ls] = h[tc - 1:tc, :]
        y_ref[:, ls] = (h * jax.nn.gelu(gate_ref[:, ls], approximate=True)).astype(y_ref.dtype)
    xbuf[0:8, :] = xbuf[tc:tc + 8, :]
    h_ref[...] = hprev[...]


def _rg_prompt(proj, B, T, M, lp, layer):
    W = lp["rg_ba"].shape[-1]
    tc = _pick(T, (256, 128, 64))
    nT = T // tc
    nblk = W // HEAD
    vec = pl.BlockSpec((None, 1, W), lambda b, c: (layer, 0, 0))
    blk = pl.BlockSpec((None, nblk, HEAD, HEAD), lambda b, c: (layer, 0, 0, 0))
    y, h = pl.pallas_call(
        _rg_prompt_body,
        grid=(B, nT),
        in_specs=[pl.BlockSpec((tc, W), lambda b, c: (b * nT + c, 0)),
                  pl.BlockSpec((tc, W), lambda b, c: (b * nT + c, 1)),
                  pl.BlockSpec((None, CONV_W, W), lambda b, c: (layer, 0, 0)),
                  vec, blk, blk, vec, vec, vec],
        out_specs=[pl.BlockSpec((tc, W), lambda b, c: (b * nT + c, 0)),
                   pl.BlockSpec((None, 1, W), lambda b, c: (b, 0, 0))],
        out_shape=[jax.ShapeDtypeStruct((M, W), bf16), jax.ShapeDtypeStruct((B, 1, W), f32)],
        scratch_shapes=[pltpu.VMEM((8 + tc, W), f32), pltpu.VMEM((1, W), f32)],
        compiler_params=_params("arbitrary", "arbitrary"), name="rg_prompt")(
            proj, proj, lp["rg_conv_w"], lp["rg_conv_b"], lp["rg_wa"], lp["rg_wx"], lp["rg_ba"], lp["rg_bx"],
            lp["rg_a_param"])
    return y, h[:, 0]


def _rg_sample_body(x_ref, gate_ref, cs_ref, h0_ref, cw_ref, cb_ref, wa_ref, wx_ref, ba_ref, bx_ref, ap_ref,
                    yin_ref, y_ref, h_ref):
    del yin_ref
    nblk = x_ref.shape[1] // HEAD
    for n in range(nblk):
        ls = slice(n * HEAD, (n + 1) * HEAD)
        xc = cb_ref[:, ls] + cw_ref[CONV_W - 1:CONV_W, ls] * x_ref[:, ls]
        for j in range(CONV_W - 1):
            xc = xc + cw_ref[j:j + 1, ls] * cs_ref[j, :, ls]
        sp = _softplus(-ap_ref[:, ls])
        a, mult, i = _rg_gates(xc, wa_ref[n], wx_ref[n], ba_ref[:, ls], bx_ref[:, ls], sp)
        h = a * h0_ref[:, ls] + mult * (i * xc)
        h_ref[:, ls] = h
        y_ref[:, ls] = (h * jax.nn.gelu(gate_ref[:, ls], approximate=True)).astype(y_ref.dtype)


def _rg_sample(proj, y_all, row0, DB, conv_state_t, h0, lp, layer):
    W = lp["rg_ba"].shape[-1]
    nblk = W // HEAD
    assert row0 % DB == 0
    rb = row0 // DB
    vec = pl.BlockSpec((None, 1, W), lambda i: (layer, 0, 0))
    blk = pl.BlockSpec((None, nblk, HEAD, HEAD), lambda i: (layer, 0, 0, 0))
    y, h = pl.pallas_call(
        _rg_sample_body,
        grid=(1,),
        in_specs=[pl.BlockSpec((DB, W), lambda i: (rb, 0)),
                  pl.BlockSpec((DB, W), lambda i: (rb, 1)),
                  pl.BlockSpec((None, CONV_W - 1, DB, W), lambda i: (layer, 0, 0, 0)),
                  pl.BlockSpec((None, DB, W), lambda i: (layer, 0, 0)),
                  pl.BlockSpec((None, CONV_W, W), lambda i: (layer, 0, 0)),
                  vec, blk, blk, vec, vec, vec,
                  pl.BlockSpec(memory_space=pl.ANY)],
        out_specs=[pl.BlockSpec((DB, W), lambda i: (rb, 0)),
                   pl.BlockSpec((DB, W), lambda i: (0, 0))],
        out_shape=[jax.ShapeDtypeStruct(y_all.shape, y_all.dtype), jax.ShapeDtypeStruct((DB, W), f32)],
        input_output_aliases={11: 0},
        compiler_params=_params("arbitrary"), name="rg_sample")(
            proj, proj, conv_state_t, h0, lp["rg_conv_w"], lp["rg_conv_b"], lp["rg_wa"], lp["rg_wx"],
            lp["rg_ba"], lp["rg_bx"], lp["rg_a_param"], y_all)
    return y, h


def _hg_gates(fx, lb):
    f = lb + (1.0 - lb) * jax.nn.sigmoid(fx)
    k = (1.0 - lb) * jax.nn.sigmoid(-fx)
    return f, k


def _hg_intra_diag(G, q, k):
    trow = lax.broadcasted_iota(jnp.int32, (SUB, HEAD), 0)
    lane = lax.broadcasted_iota(jnp.int32, (SUB, CHUNK), 1)
    blocks = []
    for i in range(CHUNK // SUB):
        sl = slice(i * SUB, (i + 1) * SUB)
        g_i, q_i, k_i = G[sl], q[sl], k[sl]
        a_d = jnp.zeros((SUB, CHUNK), f32)
        for s in range(SUB):
            e = jnp.where(trow >= s, jnp.exp(g_i - g_i[s:s + 1, :]), 0.0)
            col = jnp.sum(q_i * k_i[s:s + 1, :] * e, axis=-1, keepdims=True)
            a_d = jnp.where(lane == i * SUB + s, col, a_d)
        blocks.append(a_d)
    return jnp.concatenate(blocks, axis=0)


def _hg_intra_off(G, q, k):
    nsub = CHUNK // SUB
    row = lax.broadcasted_iota(jnp.int32, (CHUNK, HEAD), 0)
    q_parts, k_parts = [], []
    for j in range(nsub - 1):
        g_e = G[(j + 1) * SUB - 1:(j + 1) * SUB, :]
        q_parts.append(jnp.where(row >= (j + 1) * SUB, q * jnp.exp(G - g_e), 0.0))
        in_j = jnp.logical_and(row >= j * SUB, row < (j + 1) * SUB)
        k_parts.append(jnp.where(in_j, k * jnp.exp(g_e - G), 0.0))
    return _mm(jnp.concatenate(q_parts, axis=1), jnp.concatenate(k_parts, axis=1), _NT)


def _hg_prompt_body(scale, HB, q_ref, f_ref, i_ref, g_ref, lb_ref, nw_ref, y_ref, s_ref, S_scr):
    c = pl.program_id(2)
    nchunk = q_ref.shape[0] // CHUNK

    @pl.when(c == 0)
    def _():
        S_scr[...] = jnp.zeros_like(S_scr)

    nw = nw_ref[...]
    tril = _chunk_tril(nchunk * CHUNK)
    work = []
    for hh in range(HB):
        ls = slice(hh * HEAD, (hh + 1) * HEAD)
        f_all, k_all = _hg_gates(f_ref[:, ls], lb_ref[:, ls])
        q_all = q_ref[:, ls] * scale
        G_all = _mm_exact_lhs(tril, jnp.log(f_all))
        for ci in range(nchunk):
            rows = slice(ci * CHUNK, (ci + 1) * CHUNK)
            work.append(dict(hh=hh, ls=ls, rows=rows, G=G_all[rows], q=q_all[rows], k=k_all[rows]))
    for w in work:
        w["A"] = _hg_intra_diag(w["G"], w["q"], w["k"])
    for w in work:
        w["A"] = w["A"] + _hg_intra_off(w["G"], w["q"], w["k"])
    for w in work:
        G = w["G"]
        kT, GT = w["k"].T, G.T
        g_last = GT[:, CHUNK - 1:CHUNK]
        w["dec"] = jnp.exp(g_last)
        w["upd"] = _mm(kT * jnp.exp(g_last - GT), i_ref[w["rows"], w["ls"]])
        w["lhs"] = jnp.concatenate([w["A"], w["q"] * jnp.exp(G)], axis=1)
    S = [S_scr[hh] for hh in range(HB)]
    for ci in range(nchunk):
        for hh in range(HB):
            w = work[hh * nchunk + ci]
            rows, ls = w["rows"], w["ls"]
            o = _mm(w["lhs"], jnp.concatenate([i_ref[rows, ls], S[hh]], axis=0))
            S[hh] = S[hh] * w["dec"] + w["upd"]
            y_ref[rows, ls] = _gated_rms(o, nw, g_ref[rows, ls]).astype(y_ref.dtype)
    for hh in range(HB):
        S_scr[hh] = S[hh]

    @pl.when(c == pl.num_programs(2) - 1)
    def _():
        for hh in range(HB):
            s_ref[hh] = S[hh]


def _hg_prompt(proj, B, T, M, H, col0, lb, norm_w, layer):
    tc = _pick(T, (256, 128, 64))
    nT = T // tc
    HB = HEADS_PER_STEP
    assert H % HB == 0 and col0 % HB == 0
    wb = HB * HEAD
    col = lambda j: pl.BlockSpec((tc, wb), lambda b, h, c: (b * nT + c, (col0 + j * H) // HB + h))
    y, S = pl.pallas_call(
        functools.partial(_hg_prompt_body, HEAD ** -0.5, HB),
        grid=(B, H // HB, nT),
        in_specs=[col(0), col(1), col(2), col(3),
                  pl.BlockSpec((None, 1, wb), lambda b, h, c: (layer, 0, h)),
                  pl.BlockSpec((None, 1, HEAD), lambda b, h, c: (layer, 0, 0))],
        out_specs=[pl.BlockSpec((tc, wb), lambda b, h, c: (b * nT + c, h)),
                   pl.BlockSpec((None, HB, HEAD, HEAD), lambda b, h, c: (b, h, 0, 0))],
        out_shape=[jax.ShapeDtypeStruct((M, H * HEAD), bf16), jax.ShapeDtypeStruct((B, H, HEAD, HEAD), f32)],
        scratch_shapes=[pltpu.VMEM((HB, HEAD, HEAD), f32)],
        compiler_params=_params("arbitrary", "arbitrary", "arbitrary"), name="hg_prompt")(
            proj, proj, proj, proj, lb, norm_w)
    return y, S


SB = 16


def _state_step(s_ref, so_ref, o_scr, d_rows, kT, qT, vnew_fn):
    for j in range(SB):
        S = s_ref[j]
        kcol = kT[:, j:j + 1]
        d, vnew = vnew_fn(j, S, kcol)
        Sn = d * S + kcol * vnew
        so_ref[j] = Sn
        o_scr[j:j + 1, :] = jnp.sum(qT[:, j:j + 1] * Sn, axis=0, keepdims=True)


def _hg_sample_body(scale, q_ref, f_ref, i_ref, g_ref, lb_ref, nw_ref, s_ref, *rest):
    y_ref, so_ref, o_scr = rest[-3:]
    f, k = _hg_gates(f_ref[...], lb_ref[...])
    q = q_ref[...] * scale
    v = i_ref[...]
    fT, kT, qT = f.T, k.T, q.T

    def vnew(j, S, kcol):
        return fT[:, j:j + 1], v[j:j + 1, :]

    _state_step(s_ref, so_ref, o_scr, None, kT, qT, vnew)
    y_ref[...] = _gated_rms(o_scr[...], nw_ref[...], g_ref[...]).astype(y_ref.dtype)


def _state_out(state, stacked_prev, n_in):
    extra_in, extra_specs, aliases = [], [], {}
    if stacked_prev is not None:
        extra_in, extra_specs, aliases = [stacked_prev], [pl.BlockSpec(memory_space=pl.ANY)], {n_in: 1}
    return jax.ShapeDtypeStruct(state.shape, f32), extra_in, extra_specs, aliases


def _hg_sample(proj, y_all, row0, DB, H, col0, lb, norm_w, state, stacked_prev, layer):
    assert row0 % SB == 0 and DB % SB == 0
    rb = row0 // SB
    col = lambda j: pl.BlockSpec((SB, HEAD), lambda h, b: (rb + b, col0 + j * H + h))
    st_spec = pl.BlockSpec((None, SB, None, HEAD, HEAD), lambda h, b: (layer, b, h, 0, 0))
    s_shape, extra_in, extra_specs, aliases = _state_out(state, stacked_prev, 8)
    y, S = pl.pallas_call(
        functools.partial(_hg_sample_body, HEAD ** -0.5),
        grid=(H, DB // SB),
        in_specs=[col(0), col(1), col(2), col(3),
                  pl.BlockSpec((None, 1, HEAD), lambda h, b: (layer, 0, h)),
                  pl.BlockSpec((None, 1, HEAD), lambda h, b: (layer, 0, 0)),
                  st_spec,
                  pl.BlockSpec(memory_space=pl.ANY)] + extra_specs,
        out_specs=[pl.BlockSpec((SB, HEAD), lambda h, b: (rb + b, h)), st_spec],
        out_shape=[jax.ShapeDtypeStruct(y_all.shape, y_all.dtype), s_shape],
        scratch_shapes=[pltpu.VMEM((SB, HEAD), f32)],
        input_output_aliases={7: 0, **aliases},
        compiler_params=_params("arbitrary", "arbitrary"), name="hg_sample")(
            proj, proj, proj, proj, lb, norm_w, state, y_all, *extra_in)
    return y, S


def _pick_lane(x, idx):
    lane = lax.broadcasted_iota(jnp.int32, x.shape, 1)
    col = jnp.sum(jnp.where(lane == idx, x, 0.0), axis=1, keepdims=True)
    return jnp.broadcast_to(col, (x.shape[0], HEAD))


def _gd_gate_body(H, n_prompt_tiles, ab_ref, alog_ref, dtb_ref, o_ref):
    i = pl.program_id(0)
    x = ab_ref[...]
    lane = lax.broadcasted_iota(jnp.int32, x.shape, 1)
    g = jnp.where(lane < H, -jnp.exp(alog_ref[...]) * _softplus(x + dtb_ref[...]), 0.0)
    G = _mm_exact_lhs(_chunk_tril(x.shape[0]), g)
    G = jnp.where(i < n_prompt_tiles, G, g)
    o_ref[...] = jnp.where(lane < H, G, jax.nn.sigmoid(x))


def _gd_gate_prep(pab, alog_pad, dtb_pad, PT, H, layer):
    M = pab.shape[0]
    tr = 2 * CHUNK
    assert PT % tr == 0 and M % tr == 0
    row = pl.BlockSpec((tr, LANE), lambda i: (i, 0))
    vec = pl.BlockSpec((None, 1, LANE), lambda i: (layer, 0, 0))
    return pl.pallas_call(
        functools.partial(_gd_gate_body, H, PT // tr),
        grid=(M // tr,), in_specs=[row, vec, vec], out_specs=row,
        out_shape=jax.ShapeDtypeStruct((M, LANE), f32),
        compiler_params=_params("arbitrary"), name="gd_gates")(pab, alog_pad, dtb_pad)


def _unit_lower_inverses(Ns):
    r = lax.broadcasted_iota(jnp.int32, (CHUNK, CHUNK), 0)
    c = lax.broadcasted_iota(jnp.int32, (CHUNK, CHUNK), 1)
    same = lambda n: (r // n) == (c // n)
    assert CHUNK == 4 * SUB
    Rs = [jnp.where(same(SUB), N, 0.0) for N in Ns]
    Ps = [_mm(R, R) for R in Rs]
    p = 2
    while p < SUB:
        if 2 * p < SUB:
            PMs = [_mm(P, jnp.concatenate([R, P], axis=1)) for R, P in zip(Rs, Ps)]
            Rs = [R + P + PM[:, :CHUNK] for R, P, PM in zip(Rs, Ps, PMs)]
            Ps = [PM[:, CHUNK:] for PM in PMs]
        else:
            PRs = [_mm(P, R) for R, P in zip(Rs, Ps)]
            Rs = [R + P + PR for R, P, PR in zip(Rs, Ps, PRs)]
        p *= 2
    eye = jnp.where(r == c, 1.0, 0.0)
    for n in (2 * SUB, 4 * SUB):
        off = jnp.logical_and(same(n), jnp.logical_not(same(n // 2)))
        Ds = [eye + R for R in Rs]
        DCs = [_mm(D, jnp.where(off, N, 0.0)) for D, N in zip(Ds, Ns)]
        DCDs = [_mm(DC, D) for DC, D in zip(DCs, Ds)]
        Rs = [R + DCD for R, DCD in zip(Rs, DCDs)]
    return Rs


def _gd_prompt_body(scale, H, HB, q_ref, k_ref, v_ref, z_ref, gt_ref, cwq_ref, cwk_ref, cwv_ref, nw_ref,
                    y_ref, s_ref, xq, xk, xv, S_scr):
    hb = pl.program_id(1)
    c = pl.program_id(2)
    tc = q_ref.shape[0]
    nchunk = tc // CHUNK

    @pl.when(c == 0)
    def _():
        for xb in (xq, xk, xv):
            xb[0:8, :] = jnp.zeros((8, xb.shape[1]), f32)
        S_scr[...] = jnp.zeros_like(S_scr)

    def conv_silu(x_ref, xb, cw_ref):
        xb[8:8 + tc, :] = x_ref[...]
        y = cw_ref[0:1, :] * xb[pl.ds(8 - (CONV_W - 1), tc), :]
        for j in range(1, CONV_W):
            y = y + cw_ref[j:j + 1, :] * xb[pl.ds(8 - (CONV_W - 1) + j, tc), :]
        xb[0:8, :] = xb[tc:tc + 8, :]
        return _silu(y)

    qc = conv_silu(q_ref, xq, cwq_ref)
    kc = conv_silu(k_ref, xk, cwk_ref)
    vc = conv_silu(v_ref, xv, cwv_ref)
    gt = gt_ref[...]
    nw = nw_ref[...]
    tril_b = _tril_mask(CHUNK)
    strict_b = _tril_mask(CHUNK, strict=True)

    work = []
    for hh in range(HB):
        ls = slice(hh * HEAD, (hh + 1) * HEAD)
        h = hb * HB + hh
        q_all = _l2norm(qc[:, ls]) * scale
        k_all = _l2norm(kc[:, ls])
        G_all = _pick_lane(gt, h)
        beta = _pick_lane(gt, H + h)
        eG = jnp.exp(G_all)
        kb_all = k_all * beta
        rhs_all = jnp.concatenate([vc[:, ls] * beta, kb_all * eG], axis=1)
        qe_all = q_all * eG
        for ci in range(nchunk):
            rows = slice(ci * CHUNK, (ci + 1) * CHUNK)
            work.append(dict(hh=hh, ls=ls, rows=rows, G=G_all[rows], q=q_all[rows], k=k_all[rows], kb=kb_all[rows],
                             X=rhs_all[rows], qe=qe_all[rows]))
    for w in work:
        G = w["G"]
        w["decay"] = jnp.where(tril_b, jnp.exp(G[:, :CHUNK] - G.T[:CHUNK, :]), 0.0)
        w["KQ"] = _mm(jnp.concatenate([w["kb"], w["q"]], axis=0), w["k"], _NT)
    Rs = _unit_lower_inverses([jnp.where(strict_b, -(w["KQ"][:CHUNK] * w["decay"]), 0.0) for w in work])
    for w, R in zip(work, Rs):
        w["R"] = R
        w["qk"] = jnp.where(tril_b, w["KQ"][CHUNK:] * w["decay"], 0.0)
    for w in work:
        X = w["X"]
        w["X"] = X + _mm(w["R"], X)
        G = w["G"]
        g_last = G[CHUNK - 1:CHUNK, :]
        w["egl"] = jnp.exp(g_last)
        w["rhs2"] = jnp.concatenate([w["qk"], (w["k"] * jnp.exp(g_last - G)).T], axis=0)
    S = [S_scr[hh] for hh in range(HB)]
    for ci in range(nchunk):
        for hh in range(HB):
            w = work[hh * nchunk + ci]
            rows, ls = w["rows"], w["ls"]
            WS = _mm(jnp.concatenate([w["X"][:, HEAD:], w["qe"]], axis=0), S[hh])
            v_new = w["X"][:, :HEAD] - WS[:CHUNK]
            OS = _mm(w["rhs2"], v_new)
            S[hh] = w["egl"] * S[hh] + OS[CHUNK:]
            y_ref[rows, ls] = _gated_rms(WS[CHUNK:] + OS[:CHUNK], nw, z_ref[rows, ls]).astype(y_ref.dtype)
    for hh in range(HB):
        S_scr[hh] = S[hh]

    @pl.when(c == pl.num_programs(2) - 1)
    def _():
        for hh in range(HB):
            s_ref[hh] = S[hh]


def _gd_prompt(proj, gates, B, T, M, H, col0, lp, layer):
    tc = _pick(T, (256, 128, 64))
    nT = T // tc
    HB = HEADS_PER_STEP
    assert H % HB == 0 and col0 % HB == 0
    wb = HB * HEAD
    col = lambda j: pl.BlockSpec((tc, wb), lambda b, h, c: (b * nT + c, (col0 + j * H) // HB + h))
    cw = lambda j: pl.BlockSpec((None, CONV_W, wb), lambda b, h, c: (layer, 0, j * H // HB + h))
    buf = pltpu.VMEM((8 + tc, wb), f32)
    y, S = pl.pallas_call(
        functools.partial(_gd_prompt_body, HEAD ** -0.5, H, HB),
        grid=(B, H // HB, nT),
        in_specs=[col(0), col(1), col(2), col(3),
                  pl.BlockSpec((tc, LANE), lambda b, h, c: (b * nT + c, 0)),
                  cw(0), cw(1), cw(2),
                  pl.BlockSpec((None, 1, HEAD), lambda b, h, c: (layer, 0, 0))],
        out_specs=[pl.BlockSpec((tc, wb), lambda b, h, c: (b * nT + c, h)),
                   pl.BlockSpec((None, HB, HEAD, HEAD), lambda b, h, c: (b, h, 0, 0))],
        out_shape=[jax.ShapeDtypeStruct((M, H * HEAD), bf16), jax.ShapeDtypeStruct((B, H, HEAD, HEAD), f32)],
        scratch_shapes=[buf, buf, buf, pltpu.VMEM((HB, HEAD, HEAD), f32)],
        compiler_params=_params("arbitrary", "arbitrary", "arbitrary"), name="gd_prompt")(
            proj, proj, proj, proj, gates, lp["gd_conv_w"], lp["gd_conv_w"], lp["gd_conv_w"], lp["gd_norm_w"])
    return y, S


def _gd_sample_body(scale, H, q_ref, k_ref, v_ref, z_ref, gt_ref, csq_ref, csk_ref, csv_ref, cwq_ref, cwk_ref, cwv_ref,
                    nw_ref, s_ref, *rest):
    y_ref, so_ref, o_scr = rest[-3:]
    h = pl.program_id(0)

    def conv_silu(x_ref, cs_ref, cw_ref):
        y = cw_ref[CONV_W - 1:CONV_W, :] * x_ref[...]
        for j in range(CONV_W - 1):
            y = y + cw_ref[j:j + 1, :] * cs_ref[j]
        return _silu(y)

    q = _l2norm(conv_silu(q_ref, csq_ref, cwq_ref)) * scale
    k = _l2norm(conv_silu(k_ref, csk_ref, cwk_ref))
    v = conv_silu(v_ref, csv_ref, cwv_ref)
    gt = gt_ref[...]
    eg = jnp.exp(_pick_lane(gt, h))
    beta = _pick_lane(gt, H + h)
    kT, qT = k.T, q.T

    def vnew(j, S, kcol):
        egj = eg[j:j + 1, :]
        kS = jnp.sum(kcol * S, axis=0, keepdims=True)
        return egj, beta[j:j + 1, :] * (v[j:j + 1, :] - egj * kS)

    _state_step(s_ref, so_ref, o_scr, None, kT, qT, vnew)
    y_ref[...] = _gated_rms(o_scr[...], nw_ref[...], z_ref[...]).astype(y_ref.dtype)


def _gd_sample(proj, gates, y_all, row0, DB, H, col0, conv_state_t, state, stacked_prev, lp, layer):
    assert row0 % SB == 0 and DB % SB == 0
    rb = row0 // SB
    col = lambda j: pl.BlockSpec((SB, HEAD), lambda h, b: (rb + b, col0 + j * H + h))
    cs = lambda j: pl.BlockSpec((None, CONV_W - 1, SB, HEAD), lambda h, b: (layer, 0, b, j * H + h))
    cw = lambda j: pl.BlockSpec((None, CONV_W, HEAD), lambda h, b: (layer, 0, j * H + h))
    st_spec = pl.BlockSpec((None, SB, None, HEAD, HEAD), lambda h, b: (layer, b, h, 0, 0))
    s_shape, extra_in, extra_specs, aliases = _state_out(state, stacked_prev, 14)
    y, S = pl.pallas_call(
        functools.partial(_gd_sample_body, HEAD ** -0.5, H),
        grid=(H, DB // SB),
        in_specs=[col(0), col(1), col(2), col(3),
                  pl.BlockSpec((SB, LANE), lambda h, b: (rb + b, 0)),
                  cs(0), cs(1), cs(2), cw(0), cw(1), cw(2),
                  pl.BlockSpec((None, 1, HEAD), lambda h, b: (layer, 0, 0)),
                  st_spec,
                  pl.BlockSpec(memory_space=pl.ANY)] + extra_specs,
        out_specs=[pl.BlockSpec((SB, HEAD), lambda h, b: (rb + b, h)), st_spec],
        out_shape=[jax.ShapeDtypeStruct(y_all.shape, y_all.dtype), s_shape],
        scratch_shapes=[pltpu.VMEM((SB, HEAD), f32)],
        input_output_aliases={13: 0, **aliases},
        compiler_params=_params("arbitrary", "arbitrary"), name="gd_sample")(
            proj, proj, proj, proj, gates, conv_state_t, conv_state_t, conv_state_t,
            lp["gd_conv_w"], lp["gd_conv_w"], lp["gd_conv_w"], lp["gd_norm_w"],
            state, y_all, *extra_in)
    return y, S


def kernel(x_prompt, x_sample, state_rg_h, state_rg_conv, state_hg_S, state_gd_S, state_gd_conv, norm_mix_w, norm_mlp_w, norm_final_w, w_in, rg_conv_w, rg_conv_b, rg_wa, rg_ba, rg_wx, rg_bx, rg_a_param, hg_lb_logits, hg_norm_w, gd_conv_w, gd_A_log, gd_dt_bias, gd_norm_w, w_br_rg, w_br_hg, w_br_gd, w_out, w_up, w_down):
    B, T, D = x_prompt.shape
    DB, DT, _ = x_sample.shape
    assert DT == 1
    depth = w_in.shape[0]
    RW = rg_ba.shape[-1]
    H = gd_A_log.shape[-1]
    PT = B * T
    M = PT + DB
    assert RW % HEAD == 0 and hg_norm_w.shape[-1] == HEAD and gd_norm_w.shape[-1] == HEAD
    HW = H * HEAD
    n_main = 2 * RW + 8 * HW
    merge_col0 = n_main + 2 * H
    assert w_in.shape[-1] == merge_col0 + 3 * D and n_main % 1024 == 0 and n_main % LANE == 0
    hg_col0 = 2 * RW // HEAD
    gd_col0 = hg_col0 + 4 * H

    row3 = lambda a: a.reshape(depth, 1, a.shape[-1])
    lane_pad = lambda a: row3(jnp.pad(a.astype(f32), ((0, 0), (0, LANE - a.shape[-1]))))
    lp = dict(rg_conv_w=rg_conv_w, rg_conv_b=row3(rg_conv_b), rg_wa=rg_wa, rg_wx=rg_wx, rg_ba=row3(rg_ba),
              rg_bx=row3(rg_bx), rg_a_param=row3(rg_a_param), gd_conv_w=gd_conv_w, gd_norm_w=row3(gd_norm_w))
    alog_pad, dtb_pad = lane_pad(gd_A_log), lane_pad(gd_dt_bias)
    hg_nw = row3(hg_norm_w)
    lb = row3(_lower_bounds(hg_lb_logits.astype(f32)))
    rg_conv_t = jnp.swapaxes(state_rg_conv, 1, 2)
    gd_conv_t = jnp.swapaxes(state_gd_conv, 1, 2)

    x = jnp.concatenate([x_prompt.reshape(PT, D), x_sample.reshape(DB, D)], axis=0)
    _, u = _add_norm(x, None, norm_mix_w[0:1], bf16, emit_x=False)

    p_states, s_states = [], []
    y_final = None
    s_hgS = s_gdS = None
    for l in range(depth):
        proj = _gemm_wres(u, w_in, l, 0, n_main, 1024, name="in_proj")
        pab = _gemm_wres(u, w_in, l, n_main // LANE, LANE, LANE, name="in_proj_ab")
        gates = _gd_gate_prep(pab, alog_pad, dtb_pad, PT, H, l)

        y_rg, p_h = _rg_prompt(proj, B, T, M, lp, l)
        y_rg, s_h = _rg_sample(proj, y_rg, PT, DB, rg_conv_t, state_rg_h, lp, l)
        y_hg, p_hgS = _hg_prompt(proj, B, T, M, H, hg_col0, lb, hg_nw, l)
        y_hg, s_hgS = _hg_sample(proj, y_hg, PT, DB, H, hg_col0, lb, hg_nw, state_hg_S, s_hgS, l)
        y_gd, p_gdS = _gd_prompt(proj, gates, B, T, M, H, gd_col0, lp, l)
        y_gd, s_gdS = _gd_sample(proj, gates, y_gd, PT, DB, H, gd_col0, gd_conv_t, state_gd_S, s_gdS, lp, l)

        gq = gd_col0 * HEAD
        tail = lambda c0, w: jnp.stack([lax.slice(proj, (b * T + T - (CONV_W - 1), c0), (b * T + T, c0 + w))
                                        for b in range(B)], axis=0)
        last = lambda c0, w: lax.slice(proj, (PT, c0), (M, c0 + w))[:, None, :]
        p_states.append((p_h, tail(0, RW), p_hgS, p_gdS, tail(gq, 3 * HW)))
        s_states.append((s_h,
                         jnp.concatenate([state_rg_conv[l][:, 1:], last(0, RW)], axis=1),
                         None, None,
                         jnp.concatenate([state_gd_conv[l][:, 1:], last(gq, 3 * HW)], axis=1)))

        mixed = _mix(u, y_rg, y_hg, y_gd, w_in, w_br_rg, w_br_hg, w_br_gd, l, merge_col0)
        d1 = _gemm_wres(mixed, w_out, l, 0, D, 1024, name="out_proj")
        x, hmid = _add_norm(x, d1, norm_mlp_w[l:l + 1], bf16, emit_x=True)
        hh = _gemm_wres(hmid, w_up, l, 0, w_up.shape[-1], 1024, epi=lambda a: jnp.square(jnp.maximum(a, 0.0)),
                        out_dtype=bf16, name="mlp_up")
        d2 = _gemm_ksplit(hh, w_down, l, name="mlp_down")
        if l + 1 < depth:
            x, u = _add_norm(x, d2, norm_mix_w[l + 1:l + 2], bf16, emit_x=True)
        else:
            _, y_final = _add_norm(x, d2, norm_final_w.reshape(1, D), f32, emit_x=False)

    def stack(sts, j, like):
        return jnp.stack([s[j] for s in sts], axis=0).astype(like.dtype)

    return (y_final[:PT].reshape(B, T, D), y_final[PT:].reshape(DB, DT, D),
            stack(p_states, 0, state_rg_h), stack(p_states, 1, state_rg_conv), stack(p_states, 2, state_hg_S),
            stack(p_states, 3, state_gd_S), stack(p_states, 4, state_gd_conv),
            stack(s_states, 0, state_rg_h), stack(s_states, 1, state_rg_conv), s_hgS.astype(state_hg_S.dtype),
            s_gdS.astype(state_gd_S.dtype), stack(s_states, 4, state_gd_conv))
```

```python
import functools

import jax
import jax.numpy as jnp
from jax import lax
from jax.experimental import pallas as pl
from jax.experimental.pallas import tpu as pltpu

f32 = jnp.float32
bf16 = jnp.bfloat16

EPS = 1e-6
RG_C = 8.0
HEAD = 128
LANE = 128
CHUNK = 64
SUB = 16
HEADS_PER_STEP = 4
CONV_W = 4
VMEM_LIMIT = 56 * 1024 * 1024

_NT = (((1,), (1,)), ((), ()))
_TN = (((0,), (0,)), ((), ()))


def _params(*sem):
    return pltpu.CompilerParams(dimension_semantics=sem, vmem_limit_bytes=VMEM_LIMIT)


def _pick(n, cands):
    for c in cands:
        if n % c == 0:
            return c
    raise ValueError(f"no tile for {n} among {cands}")


def _mm(a, b, dims=None):
    a = a.astype(bf16)
    b = b.astype(bf16)
    if dims is None:
        return jnp.dot(a, b, preferred_element_type=f32)
    return lax.dot_general(a, b, dims, preferred_element_type=f32)


def _split3(x):
    hi = x.astype(bf16)
    r = x - hi.astype(f32)
    mid = r.astype(bf16)
    lo = (r - mid.astype(f32)).astype(bf16)
    return hi, mid, lo


def _mm_exact_lhs(a_bf16, x):
    hi, mid, lo = _split3(x)
    return (jnp.dot(a_bf16, hi, preferred_element_type=f32) + jnp.dot(a_bf16, mid, preferred_element_type=f32)
            + jnp.dot(a_bf16, lo, preferred_element_type=f32))


def _mm_hi(a, b):
    ah, am, _ = _split3(a)
    bh, bm, _ = _split3(b)
    d = functools.partial(jnp.dot, preferred_element_type=f32)
    return d(ah, bh) + (d(ah, bm) + d(am, bh))


def _expm1_neg(x, ex):
    return -jnp.tanh(0.5 * x) * (ex + 1.0)


def _softplus(x):
    return jnp.maximum(x, 0.0) + jnp.log1p(jnp.exp(-jnp.abs(x)))


def _silu(x):
    return x * jax.nn.sigmoid(x)


def _gated_rms(o, w, z):
    o = o * lax.rsqrt(jnp.mean(o * o, axis=-1, keepdims=True) + EPS) * w
    return o * _silu(z)


def _l2norm(x):
    return x * lax.rsqrt(jnp.sum(x * x, axis=-1, keepdims=True) + EPS)


def _tril_mask(n, strict=False):
    r = lax.broadcasted_iota(jnp.int32, (n, n), 0)
    c = lax.broadcasted_iota(jnp.int32, (n, n), 1)
    return (r > c) if strict else (r >= c)


def _chunk_tril(n):
    r = lax.broadcasted_iota(jnp.int32, (n, n), 0)
    c = lax.broadcasted_iota(jnp.int32, (n, n), 1)
    same = (r // CHUNK) == (c // CHUNK)
    return jnp.where(jnp.logical_and(r >= c, same), 1.0, 0.0).astype(bf16)


def _norm_body(has_delta, emit_x, *refs):
    refs = list(refs)
    x_ref = refs.pop(0)
    d_ref = refs.pop(0) if has_delta else None
    w_ref = refs.pop(0)
    xo_ref = refs.pop(0) if emit_x else None
    n_ref = refs.pop(0)
    x = x_ref[...]
    if has_delta:
        x = x + d_ref[...]
    if emit_x:
        xo_ref[...] = x
    y = x * lax.rsqrt(jnp.mean(x * x, axis=-1, keepdims=True) + EPS)
    n_ref[...] = (y * w_ref[...]).astype(n_ref.dtype)


def _add_norm(x, delta, w_row, out_dtype, emit_x):
    M, D = x.shape
    tm = _pick(M, (416, 320, 256, 128, 64, 16))
    row = pl.BlockSpec((tm, D), lambda m: (m, 0))
    in_specs = [row] + ([row] if delta is not None else []) + [pl.BlockSpec((1, D), lambda m: (0, 0))]
    out_shape = ([jax.ShapeDtypeStruct((M, D), f32)] if emit_x else []) + [jax.ShapeDtypeStruct((M, D), out_dtype)]
    out_specs = ([row] if emit_x else []) + [row]
    args = [x] + ([delta] if delta is not None else []) + [w_row]
    out = pl.pallas_call(
        functools.partial(_norm_body, delta is not None, emit_x),
        grid=(M // tm,), in_specs=in_specs, out_specs=out_specs, out_shape=out_shape,
        compiler_params=_params("arbitrary"), name="add_norm")(*args)
    return out if emit_x else (None, out[0])


def _gemm_wres_body(epi, w_is_nk, a_ref, w_ref, o_ref, wb):
    @pl.when(pl.program_id(1) == 0)
    def _():
        w = w_ref[...]
        wb[...] = (w.T if w_is_nk else w).astype(bf16)
    acc = jnp.dot(a_ref[...], wb[...], preferred_element_type=f32)
    if epi is not None:
        acc = epi(acc)
    o_ref[...] = acc.astype(o_ref.dtype)


def _gemm_wres(a, w, layer, col_block0, n_out, tn, epi=None, out_dtype=f32, name="gemm", w_is_nk=False,
               tm_cands=(640, 512, 320, 256, 128, 64, 16), single_buffer_w=False):
    M, K = a.shape
    tm = _pick(M, tm_cands)
    mode = dict(pipeline_mode=pl.Buffered(1)) if single_buffer_w else {}
    if w_is_nk:
        w_spec = pl.BlockSpec((None, tn, K), lambda n, m: (layer, n + col_block0, 0), **mode)
    else:
        w_spec = pl.BlockSpec((None, K, tn), lambda n, m: (layer, 0, n + col_block0), **mode)
    return pl.pallas_call(
        functools.partial(_gemm_wres_body, epi, w_is_nk),
        grid=(n_out // tn, M // tm),
        in_specs=[pl.BlockSpec((tm, K), lambda n, m: (m, 0)), w_spec],
        out_specs=pl.BlockSpec((tm, tn), lambda n, m: (m, n)),
        out_shape=jax.ShapeDtypeStruct((M, n_out), out_dtype),
        scratch_shapes=[pltpu.VMEM((K, tn), bf16)],
        compiler_params=_params("arbitrary", "arbitrary"), name=name)(a, w)


MERGE_SHIFT = 16


def _mix_body(u_ref, yr_ref, yh_ref, yg_ref, wm0, wm1, wm2, wx0, wx1, wx2, wr_ref, wh_ref, wg_ref, o_ref, wmb, wbb):
    tn = o_ref.shape[1]

    @pl.when(pl.program_id(1) == 0)
    def _():
        for b, (wm, wx) in enumerate(((wm0, wx0), (wm1, wx1), (wm2, wx2))):
            wcat = jnp.concatenate([wm[...], wx[...]], axis=0)
            wmb[b] = wcat[MERGE_SHIFT:MERGE_SHIFT + tn].T.astype(bf16)
        for b, wr in enumerate((wr_ref, wh_ref, wg_ref)):
            wbb[b] = wr[...].astype(bf16)

    u = u_ref[...]
    acc = None
    for b, y_ref in enumerate((yr_ref, yh_ref, yg_ref)):
        gate = jax.nn.sigmoid(jnp.dot(u, wmb[b], preferred_element_type=f32))
        p = jnp.dot(y_ref[...], wbb[b], preferred_element_type=f32)
        acc = gate * p if acc is None else acc + gate * p
    o_ref[...] = acc.astype(o_ref.dtype)


def _mix(u, y_rg, y_hg, y_gd, w_in_t, w_br_rg, w_br_hg, w_br_gd, layer, merge_col0):
    M, D = u.shape
    W = y_rg.shape[1]
    tn = 256
    tm = _pick(M, (416, 320, 256, 128, 64, 16))
    nt = D // tn
    assert (merge_col0 - MERGE_SHIFT) % tn == 0 and tn % MERGE_SHIFT == 0
    base = (merge_col0 - MERGE_SHIFT) // tn
    r = tn // MERGE_SHIFT

    def wm_spec(b):
        return pl.BlockSpec((None, tn, D), lambda n, m: (layer, base + b * nt + n, 0))

    def wx_spec(b):
        return pl.BlockSpec((None, MERGE_SHIFT, D), lambda n, m: (layer, (base + b * nt + n + 1) * r, 0))

    row = lambda w: pl.BlockSpec((tm, w), lambda n, m: (m, 0))
    br = pl.BlockSpec((None, W, tn), lambda n, m: (layer, 0, n))
    return pl.pallas_call(
        _mix_body,
        grid=(nt, M // tm),
        in_specs=[row(D), row(W), row(W), row(W), wm_spec(0), wm_spec(1), wm_spec(2),
                  wx_spec(0), wx_spec(1), wx_spec(2), br, br, br],
        out_specs=pl.BlockSpec((tm, tn), lambda n, m: (m, n)),
        out_shape=jax.ShapeDtypeStruct((M, D), bf16),
        scratch_shapes=[pltpu.VMEM((3, D, tn), bf16), pltpu.VMEM((3, W, tn), bf16)],
        compiler_params=_params("arbitrary", "arbitrary"), name="mix")(
            u, y_rg, y_hg, y_gd, w_in_t, w_in_t, w_in_t, w_in_t, w_in_t, w_in_t, w_br_rg, w_br_hg, w_br_gd)


def _lb_body(x_ref, o_ref):
    x = x_ref[...]
    depth = x.shape[0]
    m = jnp.max(x, axis=0, keepdims=True)
    e = jnp.exp(x - m)
    p = e / jnp.sum(e, axis=0, keepdims=True)
    acc = jnp.zeros_like(p[0:1])
    o_ref[0:1, :] = acc
    for l in range(1, depth):
        acc = acc + p[l:l + 1]
        o_ref[l:l + 1, :] = acc


def _lower_bounds(logits):
    return pl.pallas_call(_lb_body, out_shape=jax.ShapeDtypeStruct(logits.shape, f32), name="hg_lower_bounds")(logits)


def _rg_gates(xc, wa, wx, ba, bx, sp):
    xb = xc.astype(bf16)
    r = jax.nn.sigmoid(jnp.dot(xb, wa.astype(bf16), preferred_element_type=f32) + ba)
    i = jax.nn.sigmoid(jnp.dot(xb, wx.astype(bf16), preferred_element_type=f32) + bx)
    log_a = (-RG_C) * r * sp
    a = jnp.exp(log_a)
    mult = jnp.sqrt(_expm1_neg(2.0 * log_a, a * a))
    return a, mult, i


def _rg_prompt_body(x_ref, gate_ref, cw_ref, cb_ref, wa_ref, wx_ref, ba_ref, bx_ref, ap_ref, y_ref, h_ref, xbuf, hprev):
    c = pl.program_id(1)
    tc = x_ref.shape[0]
    nblk = x_ref.shape[1] // HEAD

    @pl.when(c == 0)
    def _():
        xbuf[0:8, :] = jnp.zeros((8, xbuf.shape[1]), f32)
        hprev[...] = jnp.zeros_like(hprev)

    xbuf[8:8 + tc, :] = x_ref[...]
    row = lax.broadcasted_iota(jnp.int32, (tc, HEAD), 0)
    first = jnp.logical_and(row == 0, c == 0)
    for n in range(nblk):
        ls = slice(n * HEAD, (n + 1) * HEAD)
        xc = cb_ref[:, ls]
        for j in range(CONV_W):
            xc = xc + cw_ref[j:j + 1, ls] * xbuf[pl.ds(8 - (CONV_W - 1) + j, tc), ls]
        sp = _softplus(-ap_ref[:, ls])
        a, mult, i = _rg_gates(xc, wa_ref[n], wx_ref[n], ba_ref[:, ls], bx_ref[:, ls], sp)
        mult = jnp.where(first, 1.0, mult)
        b = mult * (i * xc)
        s = 1
        while s < tc:
            keep = row >= s
            a_sh = jnp.where(keep, pltpu.roll(a, s, axis=0), 1.0)
            b_sh = jnp.where(keep, pltpu.roll(b, s, axis=0), 0.0)
            b = a * b_sh + b
            a = a * a_sh
            s *= 2
        h = b + a * hprev[:, ls]
        hprev[:, ls] = h[tc - 1:tc, :]
        y_ref[:, ls] = (h * jax.nn.gelu(gate_ref[:, ls], approximate=True)).astype(y_ref.dtype)
    xbuf[0:8, :] = xbuf[tc:tc + 8, :]
    h_ref[...] = hprev[...]


def _rg_prompt(proj, B, T, M, lp, layer):
    W = lp["rg_ba"].shape[-1]
    tc = _pick(T, (256, 128, 64))
    nT = T // tc
    nblk = W // HEAD
    vec = pl.BlockSpec((None, 1, W), lambda b, c: (layer, 0, 0))
    blk = pl.BlockSpec((None, nblk, HEAD, HEAD), lambda b, c: (layer, 0, 0, 0))
    y, h = pl.pallas_call(
        _rg_prompt_body,
        grid=(B, nT),
        in_specs=[pl.BlockSpec((tc, W), lambda b, c: (b * nT + c, 0)),
                  pl.BlockSpec((tc, W), lambda b, c: (b * nT + c, 1)),
                  pl.BlockSpec((None, CONV_W, W), lambda b, c: (layer, 0, 0)),
                  vec, blk, blk, vec, vec, vec],
        out_specs=[pl.BlockSpec((tc, W), lambda b, c: (b * nT + c, 0)),
                   pl.BlockSpec((None, 1, W), lambda b, c: (b, 0, 0))],
        out_shape=[jax.ShapeDtypeStruct((M, W), bf16), jax.ShapeDtypeStruct((B, 1, W), f32)],
        scratch_shapes=[pltpu.VMEM((8 + tc, W), f32), pltpu.VMEM((1, W), f32)],
        compiler_params=_params("arbitrary", "arbitrary"), name="rg_prompt")(
            proj, proj, lp["rg_conv_w"], lp["rg_conv_b"], lp["rg_wa"], lp["rg_wx"], lp["rg_ba"], lp["rg_bx"],
            lp["rg_a_param"])
    return y, h[:, 0]


def _rg_sample_body(x_ref, gate_ref, cs_ref, h0_ref, cw_ref, cb_ref, wa_ref, wx_ref, ba_ref, bx_ref, ap_ref,
                    yin_ref, y_ref, h_ref):
    del yin_ref
    nblk = x_ref.shape[1] // HEAD
    for n in range(nblk):
        ls = slice(n * HEAD, (n + 1) * HEAD)
        xc = cb_ref[:, ls] + cw_ref[CONV_W - 1:CONV_W, ls] * x_ref[:, ls]
        for j in range(CONV_W - 1):
            xc = xc + cw_ref[j:j + 1, ls] * cs_ref[j, :, ls]
        sp = _softplus(-ap_ref[:, ls])
        a, mult, i = _rg_gates(xc, wa_ref[n], wx_ref[n], ba_ref[:, ls], bx_ref[:, ls], sp)
        h = a * h0_ref[:, ls] + mult * (i * xc)
        h_ref[:, ls] = h
        y_ref[:, ls] = (h * jax.nn.gelu(gate_ref[:, ls], approximate=True)).astype(y_ref.dtype)


def _rg_sample(proj, y_all, row0, DB, conv_state_t, h0, lp, layer):
    W = lp["rg_ba"].shape[-1]
    nblk = W // HEAD
    assert row0 % DB == 0
    rb = row0 // DB
    vec = pl.BlockSpec((None, 1, W), lambda i: (layer, 0, 0))
    blk = pl.BlockSpec((None, nblk, HEAD, HEAD), lambda i: (layer, 0, 0, 0))
    y, h = pl.pallas_call(
        _rg_sample_body,
        grid=(1,),
        in_specs=[pl.BlockSpec((DB, W), lambda i: (rb, 0)),
                  pl.BlockSpec((DB, W), lambda i: (rb, 1)),
                  pl.BlockSpec((None, CONV_W - 1, DB, W), lambda i: (layer, 0, 0, 0)),
                  pl.BlockSpec((None, DB, W), lambda i: (layer, 0, 0)),
                  pl.BlockSpec((None, CONV_W, W), lambda i: (layer, 0, 0)),
                  vec, blk, blk, vec, vec, vec,
                  pl.BlockSpec(memory_space=pl.ANY)],
        out_specs=[pl.BlockSpec((DB, W), lambda i: (rb, 0)),
                   pl.BlockSpec((DB, W), lambda i: (0, 0))],
        out_shape=[jax.ShapeDtypeStruct(y_all.shape, y_all.dtype), jax.ShapeDtypeStruct((DB, W), f32)],
        input_output_aliases={11: 0},
        compiler_params=_params("arbitrary"), name="rg_sample")(
            proj, proj, conv_state_t, h0, lp["rg_conv_w"], lp["rg_conv_b"], lp["rg_wa"], lp["rg_wx"],
            lp["rg_ba"], lp["rg_bx"], lp["rg_a_param"], y_all)
    return y, h


def _hg_gates(fx, lb):
    f = lb + (1.0 - lb) * jax.nn.sigmoid(fx)
    k = (1.0 - lb) * jax.nn.sigmoid(-fx)
    return f, k


def _hg_intra_diag(G, q, k):
    trow = lax.broadcasted_iota(jnp.int32, (SUB, HEAD), 0)
    lane = lax.broadcasted_iota(jnp.int32, (SUB, CHUNK), 1)
    blocks = []
    for i in range(CHUNK // SUB):
        sl = slice(i * SUB, (i + 1) * SUB)
        g_i, q_i, k_i = G[sl], q[sl], k[sl]
        a_d = jnp.zeros((SUB, CHUNK), f32)
        for s in range(SUB):
            e = jnp.where(trow >= s, jnp.exp(g_i - g_i[s:s + 1, :]), 0.0)
            col = jnp.sum(q_i * k_i[s:s + 1, :] * e, axis=-1, keepdims=True)
            a_d = jnp.where(lane == i * SUB + s, col, a_d)
        blocks.append(a_d)
    return jnp.concatenate(blocks, axis=0)


def _hg_intra_off(G, q, k):
    nsub = CHUNK // SUB
    row = lax.broadcasted_iota(jnp.int32, (CHUNK, HEAD), 0)
    q_parts, k_parts = [], []
    for j in range(nsub - 1):
        g_e = G[(j + 1) * SUB - 1:(j + 1) * SUB, :]
        q_parts.append(jnp.where(row >= (j + 1) * SUB, q * jnp.exp(G - g_e), 0.0))
        in_j = jnp.logical_and(row >= j * SUB, row < (j + 1) * SUB)
        k_parts.append(jnp.where(in_j, k * jnp.exp(g_e - G), 0.0))
    return _mm(jnp.concatenate(q_parts, axis=1), jnp.concatenate(k_parts, axis=1), _NT)


def _hg_prompt_body(scale, HB, q_ref, f_ref, i_ref, g_ref, lb_ref, nw_ref, y_ref, s_ref, S_scr):
    c = pl.program_id(2)
    nchunk = q_ref.shape[0] // CHUNK

    @pl.when(c == 0)
    def _():
        S_scr[...] = jnp.zeros_like(S_scr)

    nw = nw_ref[...]
    tril = _chunk_tril(nchunk * CHUNK)
    work = []
    for hh in range(HB):
        ls = slice(hh * HEAD, (hh + 1) * HEAD)
        f_all, k_all = _hg_gates(f_ref[:, ls], lb_ref[:, ls])
        q_all = q_ref[:, ls] * scale
        G_all = _mm_exact_lhs(tril, jnp.log(f_all))
        for ci in range(nchunk):
            rows = slice(ci * CHUNK, (ci + 1) * CHUNK)
            work.append(dict(hh=hh, ls=ls, rows=rows, G=G_all[rows], q=q_all[rows], k=k_all[rows]))
    for w in work:
        w["A"] = _hg_intra_diag(w["G"], w["q"], w["k"])
    for w in work:
        w["A"] = w["A"] + _hg_intra_off(w["G"], w["q"], w["k"])
    for w in work:
        G = w["G"]
        kT, GT = w["k"].T, G.T
        g_last = GT[:, CHUNK - 1:CHUNK]
        w["dec"] = jnp.exp(g_last)
        w["upd"] = _mm(kT * jnp.exp(g_last - GT), i_ref[w["rows"], w["ls"]])
        w["lhs"] = jnp.concatenate([w["A"], w["q"] * jnp.exp(G)], axis=1)
    S = [S_scr[hh] for hh in range(HB)]
    for ci in range(nchunk):
        for hh in range(HB):
            w = work[hh * nchunk + ci]
            rows, ls = w["rows"], w["ls"]
            o = _mm(w["lhs"], jnp.concatenate([i_ref[rows, ls], S[hh]], axis=0))
            S[hh] = S[hh] * w["dec"] + w["upd"]
            y_ref[rows, ls] = _gated_rms(o, nw, g_ref[rows, ls]).astype(y_ref.dtype)
    for hh in range(HB):
        S_scr[hh] = S[hh]

    @pl.when(c == pl.num_programs(2) - 1)
    def _():
        for hh in range(HB):
            s_ref[hh] = S[hh]


def _hg_prompt(proj, B, T, M, H, col0, lb, norm_w, layer):
    tc = _pick(T, (256, 128, 64))
    nT = T // tc
    HB = HEADS_PER_STEP
    assert H % HB == 0 and col0 % HB == 0
    wb = HB * HEAD
    col = lambda j: pl.BlockSpec((tc, wb), lambda b, h, c: (b * nT + c, (col0 + j * H) // HB + h))
    y, S = pl.pallas_call(
        functools.partial(_hg_prompt_body, HEAD ** -0.5, HB),
        grid=(B, H // HB, nT),
        in_specs=[col(0), col(1), col(2), col(3),
                  pl.BlockSpec((None, 1, wb), lambda b, h, c: (layer, 0, h)),
                  pl.BlockSpec((None, 1, HEAD), lambda b, h, c: (layer, 0, 0))],
        out_specs=[pl.BlockSpec((tc, wb), lambda b, h, c: (b * nT + c, h)),
                   pl.BlockSpec((None, HB, HEAD, HEAD), lambda b, h, c: (b, h, 0, 0))],
        out_shape=[jax.ShapeDtypeStruct((M, H * HEAD), bf16), jax.ShapeDtypeStruct((B, H, HEAD, HEAD), f32)],
        scratch_shapes=[pltpu.VMEM((HB, HEAD, HEAD), f32)],
        compiler_params=_params("arbitrary", "arbitrary", "arbitrary"), name="hg_prompt")(
            proj, proj, proj, proj, lb, norm_w)
    return y, S


SB = 16


def _state_step(s_ref, so_ref, o_scr, d_rows, kT, qT, vnew_fn):
    for j in range(SB):
        S = s_ref[j]
        kcol = kT[:, j:j + 1]
        d, vnew = vnew_fn(j, S, kcol)
        Sn = d * S + kcol * vnew
        so_ref[j] = Sn
        o_scr[j:j + 1, :] = jnp.sum(qT[:, j:j + 1] * Sn, axis=0, keepdims=True)


def _hg_sample_body(scale, q_ref, f_ref, i_ref, g_ref, lb_ref, nw_ref, s_ref, *rest):
    y_ref, so_ref, o_scr = rest[-3:]
    f, k = _hg_gates(f_ref[...], lb_ref[...])
    q = q_ref[...] * scale
    v = i_ref[...]
    fT, kT, qT = f.T, k.T, q.T

    def vnew(j, S, kcol):
        return fT[:, j:j + 1], v[j:j + 1, :]

    _state_step(s_ref, so_ref, o_scr, None, kT, qT, vnew)
    y_ref[...] = _gated_rms(o_scr[...], nw_ref[...], g_ref[...]).astype(y_ref.dtype)


def _state_out(state, stacked_prev, n_in):
    extra_in, extra_specs, aliases = [], [], {}
    if stacked_prev is not None:
        extra_in, extra_specs, aliases = [stacked_prev], [pl.BlockSpec(memory_space=pl.ANY)], {n_in: 1}
    return jax.ShapeDtypeStruct(state.shape, f32), extra_in, extra_specs, aliases


def _hg_sample(proj, y_all, row0, DB, H, col0, lb, norm_w, state, stacked_prev, layer):
    assert row0 % SB == 0 and DB % SB == 0
    rb = row0 // SB
    col = lambda j: pl.BlockSpec((SB, HEAD), lambda h, b: (rb + b, col0 + j * H + h))
    st_spec = pl.BlockSpec((None, SB, None, HEAD, HEAD), lambda h, b: (layer, b, h, 0, 0))
    s_shape, extra_in, extra_specs, aliases = _state_out(state, stacked_prev, 8)
    y, S = pl.pallas_call(
        functools.partial(_hg_sample_body, HEAD ** -0.5),
        grid=(H, DB // SB),
        in_specs=[col(0), col(1), col(2), col(3),
                  pl.BlockSpec((None, 1, HEAD), lambda h, b: (layer, 0, h)),
                  pl.BlockSpec((None, 1, HEAD), lambda h, b: (layer, 0, 0)),
                  st_spec,
                  pl.BlockSpec(memory_space=pl.ANY)] + extra_specs,
        out_specs=[pl.BlockSpec((SB, HEAD), lambda h, b: (rb + b, h)), st_spec],
        out_shape=[jax.ShapeDtypeStruct(y_all.shape, y_all.dtype), s_shape],
        scratch_shapes=[pltpu.VMEM((SB, HEAD), f32)],
        input_output_aliases={7: 0, **aliases},
        compiler_params=_params("arbitrary", "arbitrary"), name="hg_sample")(
            proj, proj, proj, proj, lb, norm_w, state, y_all, *extra_in)
    return y, S


def _pick_lane(x, idx):
    lane = lax.broadcasted_iota(jnp.int32, x.shape, 1)
    col = jnp.sum(jnp.where(lane == idx, x, 0.0), axis=1, keepdims=True)
    return jnp.broadcast_to(col, (x.shape[0], HEAD))


def _gd_gate_body(H, n_prompt_tiles, ab_ref, alog_ref, dtb_ref, o_ref):
    i = pl.program_id(0)
    x = ab_ref[...]
    lane = lax.broadcasted_iota(jnp.int32, x.shape, 1)
    g = jnp.where(lane < H, -jnp.exp(alog_ref[...]) * _softplus(x + dtb_ref[...]), 0.0)
    G = _mm_exact_lhs(_chunk_tril(x.shape[0]), g)
    G = jnp.where(i < n_prompt_tiles, G, g)
    o_ref[...] = jnp.where(lane < H, G, jax.nn.sigmoid(x))


def _gd_gate_prep(pab, alog_pad, dtb_pad, PT, H, layer):
    M = pab.shape[0]
    tr = 2 * CHUNK
    assert PT % tr == 0 and M % tr == 0
    row = pl.BlockSpec((tr, LANE), lambda i: (i, 0))
    vec = pl.BlockSpec((None, 1, LANE), lambda i: (layer, 0, 0))
    return pl.pallas_call(
        functools.partial(_gd_gate_body, H, PT // tr),
        grid=(M // tr,), in_specs=[row, vec, vec], out_specs=row,
        out_shape=jax.ShapeDtypeStruct((M, LANE), f32),
        compiler_params=_params("arbitrary"), name="gd_gates")(pab, alog_pad, dtb_pad)


def _unit_lower_inverses(Ns):
    r = lax.broadcasted_iota(jnp.int32, (CHUNK, CHUNK), 0)
    c = lax.broadcasted_iota(jnp.int32, (CHUNK, CHUNK), 1)
    same = lambda n: (r // n) == (c // n)
    assert CHUNK == 4 * SUB
    Rs = [jnp.where(same(SUB), N, 0.0) for N in Ns]
    Ps = [_mm(R, R) for R in Rs]
    p = 2
    while p < SUB:
        if 2 * p < SUB:
            PMs = [_mm(P, jnp.concatenate([R, P], axis=1)) for R, P in zip(Rs, Ps)]
            Rs = [R + P + PM[:, :CHUNK] for R, P, PM in zip(Rs, Ps, PMs)]
            Ps = [PM[:, CHUNK:] for PM in PMs]
        else:
            PRs = [_mm(P, R) for R, P in zip(Rs, Ps)]
            Rs = [R + P + PR for R, P, PR in zip(Rs, Ps, PRs)]
        p *= 2
    eye = jnp.where(r == c, 1.0, 0.0)
    for n in (2 * SUB, 4 * SUB):
        off = jnp.logical_and(same(n), jnp.logical_not(same(n // 2)))
        Ds = [eye + R for R in Rs]
        DCs = [_mm(D, jnp.where(off, N, 0.0)) for D, N in zip(Ds, Ns)]
        DCDs = [_mm(DC, D) for DC, D in zip(DCs, Ds)]
        Rs = [R + DCD for R, DCD in zip(Rs, DCDs)]
    return Rs


def _gd_prompt_body(scale, H, HB, q_ref, k_ref, v_ref, z_ref, gt_ref, cwq_ref, cwk_ref, cwv_ref, nw_ref,
                    y_ref, s_ref, xq, xk, xv, S_scr):
    hb = pl.program_id(1)
    c = pl.program_id(2)
    tc = q_ref.shape[0]
    nchunk = tc // CHUNK

    @pl.when(c == 0)
    def _():
        for xb in (xq, xk, xv):
            xb[0:8, :] = jnp.zeros((8, xb.shape[1]), f32)
        S_scr[...] = jnp.zeros_like(S_scr)

    def conv_silu(x_ref, xb, cw_ref):
        xb[8:8 + tc, :] = x_ref[...]
        y = cw_ref[0:1, :] * xb[pl.ds(8 - (CONV_W - 1), tc), :]
        for j in range(1, CONV_W):
            y = y + cw_ref[j:j + 1, :] * xb[pl.ds(8 - (CONV_W - 1) + j, tc), :]
        xb[0:8, :] = xb[tc:tc + 8, :]
        return _silu(y)

    qc = conv_silu(q_ref, xq, cwq_ref)
    kc = conv_silu(k_ref, xk, cwk_ref)
    vc = conv_silu(v_ref, xv, cwv_ref)
    gt = gt_ref[...]
    nw = nw_ref[...]
    tril_b = _tril_mask(CHUNK)
    strict_b = _tril_mask(CHUNK, strict=True)

    work = []
    for hh in range(HB):
        ls = slice(hh * HEAD, (hh + 1) * HEAD)
        h = hb * HB + hh
        q_all = _l2norm(qc[:, ls]) * scale
        k_all = _l2norm(kc[:, ls])
        G_all = _pick_lane(gt, h)
        beta = _pick_lane(gt, H + h)
        eG = jnp.exp(G_all)
        kb_all = k_all * beta
        rhs_all = jnp.concatenate([vc[:, ls] * beta, kb_all * eG], axis=1)
        qe_all = q_all * eG
        for ci in range(nchunk):
            rows = slice(ci * CHUNK, (ci + 1) * CHUNK)
            work.append(dict(hh=hh, ls=ls, rows=rows, G=G_all[rows], q=q_all[rows], k=k_all[rows], kb=kb_all[rows],
                             X=rhs_all[rows], qe=qe_all[rows]))
    for w in work:
        G = w["G"]
        w["decay"] = jnp.where(tril_b, jnp.exp(G[:, :CHUNK] - G.T[:CHUNK, :]), 0.0)
        w["KQ"] = _mm(jnp.concatenate([w["kb"], w["q"]], axis=0), w["k"], _NT)
    Rs = _unit_lower_inverses([jnp.where(strict_b, -(w["KQ"][:CHUNK] * w["decay"]), 0.0) for w in work])
    for w, R in zip(work, Rs):
        w["R"] = R
        w["qk"] = jnp.where(tril_b, w["KQ"][CHUNK:] * w["decay"], 0.0)
    for w in work:
        X = w["X"]
        w["X"] = X + _mm(w["R"], X)
        G = w["G"]
        g_last = G[CHUNK - 1:CHUNK, :]
        w["egl"] = jnp.exp(g_last)
        w["rhs2"] = jnp.concatenate([w["qk"], (w["k"] * jnp.exp(g_last - G)).T], axis=0)
    S = [S_scr[hh] for hh in range(HB)]
    for ci in range(nchunk):
        for hh in range(HB):
            w = work[hh * nchunk + ci]
            rows, ls = w["rows"], w["ls"]
            WS = _mm(jnp.concatenate([w["X"][:, HEAD:], w["qe"]], axis=0), S[hh])
            v_new = w["X"][:, :HEAD] - WS[:CHUNK]
            OS = _mm(w["rhs2"], v_new)
            S[hh] = w["egl"] * S[hh] + OS[CHUNK:]
            y_ref[rows, ls] = _gated_rms(WS[CHUNK:] + OS[:CHUNK], nw, z_ref[rows, ls]).astype(y_ref.dtype)
    for hh in range(HB):
        S_scr[hh] = S[hh]

    @pl.when(c == pl.num_programs(2) - 1)
    def _():
        for hh in range(HB):
            s_ref[hh] = S[hh]


def _gd_prompt(proj, gates, B, T, M, H, col0, lp, layer):
    tc = _pick(T, (256, 128, 64))
    nT = T // tc
    HB = HEADS_PER_STEP
    assert H % HB == 0 and col0 % HB == 0
    wb = HB * HEAD
    col = lambda j: pl.BlockSpec((tc, wb), lambda b, h, c: (b * nT + c, (col0 + j * H) // HB + h))
    cw = lambda j: pl.BlockSpec((None, CONV_W, wb), lambda b, h, c: (layer, 0, j * H // HB + h))
    buf = pltpu.VMEM((8 + tc, wb), f32)
    y, S = pl.pallas_call(
        functools.partial(_gd_prompt_body, HEAD ** -0.5, H, HB),
        grid=(B, H // HB, nT),
        in_specs=[col(0), col(1), col(2), col(3),
                  pl.BlockSpec((tc, LANE), lambda b, h, c: (b * nT + c, 0)),
                  cw(0), cw(1), cw(2),
                  pl.BlockSpec((None, 1, HEAD), lambda b, h, c: (layer, 0, 0))],
        out_specs=[pl.BlockSpec((tc, wb), lambda b, h, c: (b * nT + c, h)),
                   pl.BlockSpec((None, HB, HEAD, HEAD), lambda b, h, c: (b, h, 0, 0))],
        out_shape=[jax.ShapeDtypeStruct((M, H * HEAD), bf16), jax.ShapeDtypeStruct((B, H, HEAD, HEAD), f32)],
        scratch_shapes=[buf, buf, buf, pltpu.VMEM((HB, HEAD, HEAD), f32)],
        compiler_params=_params("arbitrary", "arbitrary", "arbitrary"), name="gd_prompt")(
            proj, proj, proj, proj, gates, lp["gd_conv_w"], lp["gd_conv_w"], lp["gd_conv_w"], lp["gd_norm_w"])
    return y, S


def _gd_sample_body(scale, H, q_ref, k_ref, v_ref, z_ref, gt_ref, csq_ref, csk_ref, csv_ref, cwq_ref, cwk_ref, cwv_ref,
                    nw_ref, s_ref, *rest):
    y_ref, so_ref, o_scr = rest[-3:]
    h = pl.program_id(0)

    def conv_silu(x_ref, cs_ref, cw_ref):
        y = cw_ref[CONV_W - 1:CONV_W, :] * x_ref[...]
        for j in range(CONV_W - 1):
            y = y + cw_ref[j:j + 1, :] * cs_ref[j]
        return _silu(y)

    q = _l2norm(conv_silu(q_ref, csq_ref, cwq_ref)) * scale
    k = _l2norm(conv_silu(k_ref, csk_ref, cwk_ref))
    v = conv_silu(v_ref, csv_ref, cwv_ref)
    gt = gt_ref[...]
    eg = jnp.exp(_pick_lane(gt, h))
    beta = _pick_lane(gt, H + h)
    kT, qT = k.T, q.T

    def vnew(j, S, kcol):
        egj = eg[j:j + 1, :]
        kS = jnp.sum(kcol * S, axis=0, keepdims=True)
        return egj, beta[j:j + 1, :] * (v[j:j + 1, :] - egj * kS)

    _state_step(s_ref, so_ref, o_scr, None, kT, qT, vnew)
    y_ref[...] = _gated_rms(o_scr[...], nw_ref[...], z_ref[...]).astype(y_ref.dtype)


def _gd_sample(proj, gates, y_all, row0, DB, H, col0, conv_state_t, state, stacked_prev, lp, layer):
    assert row0 % SB == 0 and DB % SB == 0
    rb = row0 // SB
    col = lambda j: pl.BlockSpec((SB, HEAD), lambda h, b: (rb + b, col0 + j * H + h))
    cs = lambda j: pl.BlockSpec((None, CONV_W - 1, SB, HEAD), lambda h, b: (layer, 0, b, j * H + h))
    cw = lambda j: pl.BlockSpec((None, CONV_W, HEAD), lambda h, b: (layer, 0, j * H + h))
    st_spec = pl.BlockSpec((None, SB, None, HEAD, HEAD), lambda h, b: (layer, b, h, 0, 0))
    s_shape, extra_in, extra_specs, aliases = _state_out(state, stacked_prev, 14)
    y, S = pl.pallas_call(
        functools.partial(_gd_sample_body, HEAD ** -0.5, H),
        grid=(H, DB // SB),
        in_specs=[col(0), col(1), col(2), col(3),
                  pl.BlockSpec((SB, LANE), lambda h, b: (rb + b, 0)),
                  cs(0), cs(1), cs(2), cw(0), cw(1), cw(2),
                  pl.BlockSpec((None, 1, HEAD), lambda h, b: (layer, 0, 0)),
                  st_spec,
                  pl.BlockSpec(memory_space=pl.ANY)] + extra_specs,
        out_specs=[pl.BlockSpec((SB, HEAD), lambda h, b: (rb + b, h)), st_spec],
        out_shape=[jax.ShapeDtypeStruct(y_all.shape, y_all.dtype), s_shape],
        scratch_shapes=[pltpu.VMEM((SB, HEAD), f32)],
        input_output_aliases={13: 0, **aliases},
        compiler_params=_params("arbitrary", "arbitrary"), name="gd_sample")(
            proj, proj, proj, proj, gates, conv_state_t, conv_state_t, conv_state_t,
            lp["gd_conv_w"], lp["gd_conv_w"], lp["gd_conv_w"], lp["gd_norm_w"],
            state, y_all, *extra_in)
    return y, S


def kernel(x_prompt, x_sample, state_rg_h, state_rg_conv, state_hg_S, state_gd_S, state_gd_conv, norm_mix_w, norm_mlp_w, norm_final_w, w_in, rg_conv_w, rg_conv_b, rg_wa, rg_ba, rg_wx, rg_bx, rg_a_param, hg_lb_logits, hg_norm_w, gd_conv_w, gd_A_log, gd_dt_bias, gd_norm_w, w_br_rg, w_br_hg, w_br_gd, w_out, w_up, w_down):
    B, T, D = x_prompt.shape
    DB, DT, _ = x_sample.shape
    assert DT == 1
    depth = w_in.shape[0]
    RW = rg_ba.shape[-1]
    H = gd_A_log.shape[-1]
    PT = B * T
    M = PT + DB
    assert RW % HEAD == 0 and hg_norm_w.shape[-1] == HEAD and gd_norm_w.shape[-1] == HEAD
    HW = H * HEAD
    n_main = 2 * RW + 8 * HW
    merge_col0 = n_main + 2 * H
    assert w_in.shape[-1] == merge_col0 + 3 * D and n_main % 1024 == 0 and n_main % LANE == 0
    hg_col0 = 2 * RW // HEAD
    gd_col0 = hg_col0 + 4 * H

    row3 = lambda a: a.reshape(depth, 1, a.shape[-1])
    lane_pad = lambda a: row3(jnp.pad(a.astype(f32), ((0, 0), (0, LANE - a.shape[-1]))))
    lp = dict(rg_conv_w=rg_conv_w, rg_conv_b=row3(rg_conv_b), rg_wa=rg_wa, rg_wx=rg_wx, rg_ba=row3(rg_ba),
              rg_bx=row3(rg_bx), rg_a_param=row3(rg_a_param), gd_conv_w=gd_conv_w, gd_norm_w=row3(gd_norm_w))
    alog_pad, dtb_pad = lane_pad(gd_A_log), lane_pad(gd_dt_bias)
    hg_nw = row3(hg_norm_w)
    lb = row3(_lower_bounds(hg_lb_logits.astype(f32)))
    rg_conv_t = jnp.swapaxes(state_rg_conv, 1, 2)
    gd_conv_t = jnp.swapaxes(state_gd_conv, 1, 2)

    w_in_t = jnp.swapaxes(w_in, 1, 2)
    x = jnp.concatenate([x_prompt.reshape(PT, D), x_sample.reshape(DB, D)], axis=0)
    _, u = _add_norm(x, None, norm_mix_w[0:1], bf16, emit_x=False)

    p_states, s_states = [], []
    y_final = None
    s_hgS = s_gdS = None
    for l in range(depth):
        proj = _gemm_wres(u, w_in_t, l, 0, n_main, 1024, name="in_proj", w_is_nk=True)
        pab = _gemm_wres(u, w_in_t, l, n_main // LANE, LANE, LANE, name="in_proj_ab", w_is_nk=True)
        gates = _gd_gate_prep(pab, alog_pad, dtb_pad, PT, H, l)

        y_rg, p_h = _rg_prompt(proj, B, T, M, lp, l)
        y_rg, s_h = _rg_sample(proj, y_rg, PT, DB, rg_conv_t, state_rg_h, lp, l)
        y_hg, p_hgS = _hg_prompt(proj, B, T, M, H, hg_col0, lb, hg_nw, l)
        y_hg, s_hgS = _hg_sample(proj, y_hg, PT, DB, H, hg_col0, lb, hg_nw, state_hg_S, s_hgS, l)
        y_gd, p_gdS = _gd_prompt(proj, gates, B, T, M, H, gd_col0, lp, l)
        y_gd, s_gdS = _gd_sample(proj, gates, y_gd, PT, DB, H, gd_col0, gd_conv_t, state_gd_S, s_gdS, lp, l)

        gq = gd_col0 * HEAD
        tail = lambda c0, w: jnp.stack([lax.slice(proj, (b * T + T - (CONV_W - 1), c0), (b * T + T, c0 + w))
                                        for b in range(B)], axis=0)
        last = lambda c0, w: lax.slice(proj, (PT, c0), (M, c0 + w))[:, None, :]
        p_states.append((p_h, tail(0, RW), p_hgS, p_gdS, tail(gq, 3 * HW)))
        s_states.append((s_h,
                         jnp.concatenate([state_rg_conv[l][:, 1:], last(0, RW)], axis=1),
                         None, None,
                         jnp.concatenate([state_gd_conv[l][:, 1:], last(gq, 3 * HW)], axis=1)))

        mixed = _mix(u, y_rg, y_hg, y_gd, w_in_t, w_br_rg, w_br_hg, w_br_gd, l, merge_col0)
        d1 = _gemm_wres(mixed, w_out, l, 0, D, 1024, name="out_proj")
        x, hmid = _add_norm(x, d1, norm_mlp_w[l:l + 1], bf16, emit_x=True)
        hh = _gemm_wres(hmid, w_up, l, 0, w_up.shape[-1], 1024, epi=lambda a: jnp.square(jnp.maximum(a, 0.0)),
                        out_dtype=bf16, name="mlp_up")
        d2 = _gemm_wres(hh, w_down, l, 0, D, 512, name="mlp_down", tm_cands=(208, 128, 64, 16), single_buffer_w=True)
        if l + 1 < depth:
            x, u = _add_norm(x, d2, norm_mix_w[l + 1:l + 2], bf16, emit_x=True)
        else:
            _, y_final = _add_norm(x, d2, norm_final_w.reshape(1, D), f32, emit_x=False)

    def stack(sts, j, like):
        return jnp.stack([s[j] for s in sts], axis=0).astype(like.dtype)

    return (y_final[:PT].reshape(B, T, D), y_final[PT:].reshape(DB, DT, D),
            stack(p_states, 0, state_rg_h), stack(p_states, 1, state_rg_conv), stack(p_states, 2, state_hg_S),
            stack(p_states, 3, state_gd_S), stack(p_states, 4, state_gd_conv),
            stack(s_states, 0, state_rg_h), stack(s_states, 1, state_rg_conv), s_hgS.astype(state_hg_S.dtype),
            s_gdS.astype(state_gd_S.dtype), stack(s_states, 4, state_gd_conv))
```

```python
import functools

import jax
import jax.numpy as jnp
from jax import lax
from jax.experimental import pallas as pl
from jax.experimental.pallas import tpu as pltpu

f32 = jnp.float32
bf16 = jnp.bfloat16

EPS = 1e-6
RG_C = 8.0
HEAD = 128
LANE = 128
CHUNK = 64
SUB = 16
HEADS_PER_STEP = 8
CONV_W = 4
VMEM_LIMIT = 56 * 1024 * 1024

_NT = (((1,), (1,)), ((), ()))
_TN = (((0,), (0,)), ((), ()))


def _params(*sem):
    return pltpu.CompilerParams(dimension_semantics=sem, vmem_limit_bytes=VMEM_LIMIT)


def _pick(n, cands):
    for c in cands:
        if n % c == 0:
            return c
    raise ValueError(f"no tile for {n} among {cands}")


def _mm(a, b, dims=None):
    a = a.astype(bf16)
    b = b.astype(bf16)
    if dims is None:
        return jnp.dot(a, b, preferred_element_type=f32)
    return lax.dot_general(a, b, dims, preferred_element_type=f32)


def _split3(x):
    hi = x.astype(bf16)
    r = x - hi.astype(f32)
    mid = r.astype(bf16)
    lo = (r - mid.astype(f32)).astype(bf16)
    return hi, mid, lo


def _mm_exact_lhs(a_bf16, x):
    hi, mid, lo = _split3(x)
    return (jnp.dot(a_bf16, hi, preferred_element_type=f32) + jnp.dot(a_bf16, mid, preferred_element_type=f32)
            + jnp.dot(a_bf16, lo, preferred_element_type=f32))


def _mm_hi(a, b):
    ah, am, _ = _split3(a)
    bh, bm, _ = _split3(b)
    d = functools.partial(jnp.dot, preferred_element_type=f32)
    return d(ah, bh) + (d(ah, bm) + d(am, bh))


def _expm1_neg(x, ex):
    return -jnp.tanh(0.5 * x) * (ex + 1.0)


def _softplus(x):
    return jnp.maximum(x, 0.0) + jnp.log1p(jnp.exp(-jnp.abs(x)))


def _silu(x):
    return x * jax.nn.sigmoid(x)


def _gated_rms(o, w, z):
    o = o * lax.rsqrt(jnp.mean(o * o, axis=-1, keepdims=True) + EPS) * w
    return o * _silu(z)


def _l2norm(x):
    return x * lax.rsqrt(jnp.sum(x * x, axis=-1, keepdims=True) + EPS)


def _tril_mask(n, strict=False):
    r = lax.broadcasted_iota(jnp.int32, (n, n), 0)
    c = lax.broadcasted_iota(jnp.int32, (n, n), 1)
    return (r > c) if strict else (r >= c)


def _chunk_tril(n):
    r = lax.broadcasted_iota(jnp.int32, (n, n), 0)
    c = lax.broadcasted_iota(jnp.int32, (n, n), 1)
    same = (r // CHUNK) == (c // CHUNK)
    return jnp.where(jnp.logical_and(r >= c, same), 1.0, 0.0).astype(bf16)


def _norm_body(has_delta, emit_x, *refs):
    refs = list(refs)
    x_ref = refs.pop(0)
    d_ref = refs.pop(0) if has_delta else None
    w_ref = refs.pop(0)
    xo_ref = refs.pop(0) if emit_x else None
    n_ref = refs.pop(0)
    x = x_ref[...]
    if has_delta:
        x = x + d_ref[...]
    if emit_x:
        xo_ref[...] = x
    y = x * lax.rsqrt(jnp.mean(x * x, axis=-1, keepdims=True) + EPS)
    n_ref[...] = (y * w_ref[...]).astype(n_ref.dtype)


def _add_norm(x, delta, w_row, out_dtype, emit_x):
    M, D = x.shape
    tm = _pick(M, (416, 320, 256, 128, 64, 16))
    row = pl.BlockSpec((tm, D), lambda m: (m, 0))
    in_specs = [row] + ([row] if delta is not None else []) + [pl.BlockSpec((1, D), lambda m: (0, 0))]
    out_shape = ([jax.ShapeDtypeStruct((M, D), f32)] if emit_x else []) + [jax.ShapeDtypeStruct((M, D), out_dtype)]
    out_specs = ([row] if emit_x else []) + [row]
    args = [x] + ([delta] if delta is not None else []) + [w_row]
    out = pl.pallas_call(
        functools.partial(_norm_body, delta is not None, emit_x),
        grid=(M // tm,), in_specs=in_specs, out_specs=out_specs, out_shape=out_shape,
        compiler_params=_params("arbitrary"), name="add_norm")(*args)
    return out if emit_x else (None, out[0])


def _gemm_wres_body(epi, w_is_nk, a_ref, w_ref, o_ref, wb):
    @pl.when(pl.program_id(1) == 0)
    def _():
        w = w_ref[...]
        wb[...] = (w.T if w_is_nk else w).astype(bf16)
    acc = jnp.dot(a_ref[...], wb[...], preferred_element_type=f32)
    if epi is not None:
        acc = epi(acc)
    o_ref[...] = acc.astype(o_ref.dtype)


def _gemm_wres(a, w, layer, col_block0, n_out, tn, epi=None, out_dtype=f32, name="gemm", w_is_nk=False,
               tm_cands=(640, 512, 320, 256, 128, 64, 16), single_buffer_w=False):
    M, K = a.shape
    tm = _pick(M, tm_cands)
    mode = dict(pipeline_mode=pl.Buffered(1)) if single_buffer_w else {}
    if w_is_nk:
        w_spec = pl.BlockSpec((None, tn, K), lambda n, m: (layer, n + col_block0, 0), **mode)
    else:
        w_spec = pl.BlockSpec((None, K, tn), lambda n, m: (layer, 0, n + col_block0), **mode)
    return pl.pallas_call(
        functools.partial(_gemm_wres_body, epi, w_is_nk),
        grid=(n_out // tn, M // tm),
        in_specs=[pl.BlockSpec((tm, K), lambda n, m: (m, 0)), w_spec],
        out_specs=pl.BlockSpec((tm, tn), lambda n, m: (m, n)),
        out_shape=jax.ShapeDtypeStruct((M, n_out), out_dtype),
        scratch_shapes=[pltpu.VMEM((K, tn), bf16)],
        compiler_params=_params("arbitrary", "arbitrary"), name=name)(a, w)


def _proj_norm_body(a_ref, w_ref, x_ref, nw_ref, xo_ref, n_ref, wb):
    @pl.when(pl.program_id(0) == 0)
    def _():
        wb[...] = w_ref[...].astype(bf16)
    x = x_ref[...] + jnp.dot(a_ref[...], wb[...], preferred_element_type=f32)
    xo_ref[...] = x
    y = x * lax.rsqrt(jnp.mean(x * x, axis=-1, keepdims=True) + EPS)
    n_ref[...] = (y * nw_ref[...]).astype(n_ref.dtype)


def _proj_add_norm(a, w, layer, x, nw_row):
    M, K = a.shape
    D = w.shape[-1]
    tm = _pick(M, (320, 256, 128, 64, 16))
    row = lambda width: pl.BlockSpec((tm, width), lambda m: (m, 0))
    return pl.pallas_call(
        _proj_norm_body,
        grid=(M // tm,),
        in_specs=[row(K), pl.BlockSpec((None, K, D), lambda m: (layer, 0, 0), pipeline_mode=pl.Buffered(1)),
                  row(D), pl.BlockSpec((1, D), lambda m: (0, 0))],
        out_specs=[row(D), row(D)],
        out_shape=[jax.ShapeDtypeStruct((M, D), f32), jax.ShapeDtypeStruct((M, D), bf16)],
        scratch_shapes=[pltpu.VMEM((K, D), bf16)],
        compiler_params=_params("arbitrary"), name="out_proj_norm")(a, w, x, nw_row)


MERGE_SHIFT = 16


def _mix_body(u_ref, yr_ref, yh_ref, yg_ref, wm0, wm1, wm2, wx0, wx1, wx2, wr_ref, wh_ref, wg_ref, o_ref, wmb, wbb):
    tn = o_ref.shape[1]

    @pl.when(pl.program_id(1) == 0)
    def _():
        for b, (wm, wx) in enumerate(((wm0, wx0), (wm1, wx1), (wm2, wx2))):
            wcat = jnp.concatenate([wm[...], wx[...]], axis=0)
            wmb[b] = wcat[MERGE_SHIFT:MERGE_SHIFT + tn].T.astype(bf16)
        for b, wr in enumerate((wr_ref, wh_ref, wg_ref)):
            wbb[b] = wr[...].astype(bf16)

    u = u_ref[...]
    acc = None
    for b, y_ref in enumerate((yr_ref, yh_ref, yg_ref)):
        gate = jax.nn.sigmoid(jnp.dot(u, wmb[b], preferred_element_type=f32))
        p = jnp.dot(y_ref[...], wbb[b], preferred_element_type=f32)
        acc = gate * p if acc is None else acc + gate * p
    o_ref[...] = acc.astype(o_ref.dtype)


def _mix(u, y_rg, y_hg, y_gd, w_in_t, w_br_rg, w_br_hg, w_br_gd, layer, merge_col0):
    M, D = u.shape
    W = y_rg.shape[1]
    tn = 256
    tm = _pick(M, (416, 320, 256, 128, 64, 16))
    nt = D // tn
    assert (merge_col0 - MERGE_SHIFT) % tn == 0 and tn % MERGE_SHIFT == 0
    base = (merge_col0 - MERGE_SHIFT) // tn
    r = tn // MERGE_SHIFT

    def wm_spec(b):
        return pl.BlockSpec((None, tn, D), lambda n, m: (layer, base + b * nt + n, 0))

    def wx_spec(b):
        return pl.BlockSpec((None, MERGE_SHIFT, D), lambda n, m: (layer, (base + b * nt + n + 1) * r, 0))

    row = lambda w: pl.BlockSpec((tm, w), lambda n, m: (m, 0))
    br = pl.BlockSpec((None, W, tn), lambda n, m: (layer, 0, n))
    return pl.pallas_call(
        _mix_body,
        grid=(nt, M // tm),
        in_specs=[row(D), row(W), row(W), row(W), wm_spec(0), wm_spec(1), wm_spec(2),
                  wx_spec(0), wx_spec(1), wx_spec(2), br, br, br],
        out_specs=pl.BlockSpec((tm, tn), lambda n, m: (m, n)),
        out_shape=jax.ShapeDtypeStruct((M, D), bf16),
        scratch_shapes=[pltpu.VMEM((3, D, tn), bf16), pltpu.VMEM((3, W, tn), bf16)],
        compiler_params=_params("arbitrary", "arbitrary"), name="mix")(
            u, y_rg, y_hg, y_gd, w_in_t, w_in_t, w_in_t, w_in_t, w_in_t, w_in_t, w_br_rg, w_br_hg, w_br_gd)


def _lb_body(x_ref, o_ref):
    x = x_ref[...]
    depth = x.shape[0]
    m = jnp.max(x, axis=0, keepdims=True)
    e = jnp.exp(x - m)
    p = e / jnp.sum(e, axis=0, keepdims=True)
    acc = jnp.zeros_like(p[0:1])
    o_ref[0:1, :] = acc
    for l in range(1, depth):
        acc = acc + p[l:l + 1]
        o_ref[l:l + 1, :] = acc


def _lower_bounds(logits):
    return pl.pallas_call(_lb_body, out_shape=jax.ShapeDtypeStruct(logits.shape, f32), name="hg_lower_bounds")(logits)


def _rg_gates(xc, wa, wx, ba, bx, sp):
    xb = xc.astype(bf16)
    r = jax.nn.sigmoid(jnp.dot(xb, wa.astype(bf16), preferred_element_type=f32) + ba)
    i = jax.nn.sigmoid(jnp.dot(xb, wx.astype(bf16), preferred_element_type=f32) + bx)
    log_a = (-RG_C) * r * sp
    a = jnp.exp(log_a)
    mult = jnp.sqrt(_expm1_neg(2.0 * log_a, a * a))
    return a, mult, i


def _rg_prompt_body(x_ref, gate_ref, cw_ref, cb_ref, wa_ref, wx_ref, ba_ref, bx_ref, ap_ref, y_ref, h_ref, xbuf, hprev):
    c = pl.program_id(1)
    tc = x_ref.shape[0]
    nblk = x_ref.shape[1] // HEAD

    @pl.when(c == 0)
    def _():
        xbuf[0:8, :] = jnp.zeros((8, xbuf.shape[1]), f32)
        hprev[...] = jnp.zeros_like(hprev)

    xbuf[8:8 + tc, :] = x_ref[...]
    row = lax.broadcasted_iota(jnp.int32, (tc, HEAD), 0)
    first = jnp.logical_and(row == 0, c == 0)
    for n in range(nblk):
        ls = slice(n * HEAD, (n + 1) * HEAD)
        xc = cb_ref[:, ls]
        for j in range(CONV_W):
            xc = xc + cw_ref[j:j + 1, ls] * xbuf[pl.ds(8 - (CONV_W - 1) + j, tc), ls]
        sp = _softplus(-ap_ref[:, ls])
        a, mult, i = _rg_gates(xc, wa_ref[n], wx_ref[n], ba_ref[:, ls], bx_ref[:, ls], sp)
        mult = jnp.where(first, 1.0, mult)
        b = mult * (i * xc)
        s = 1
        while s < tc:
            keep = row >= s
            a_sh = jnp.where(keep, pltpu.roll(a, s, axis=0), 1.0)
            b_sh = jnp.where(keep, pltpu.roll(b, s, axis=0), 0.0)
            b = a * b_sh + b
            a = a * a_sh
            s *= 2
        h = b + a * hprev[:, ls]
        hprev[:, ls] = h[tc - 1:tc, :]
        y_ref[:, ls] = (h * jax.nn.gelu(gate_ref[:, ls], approximate=True)).astype(y_ref.dtype)
    xbuf[0:8, :] = xbuf[tc:tc + 8, :]
    h_ref[...] = hprev[...]


def _rg_prompt(proj, B, T, M, lp, layer):
    W = lp["rg_ba"].shape[-1]
    tc = _pick(T, (256, 128, 64))
    nT = T // tc
    nblk = W // HEAD
    vec = pl.BlockSpec((None, 1, W), lambda b, c: (layer, 0, 0))
    blk = pl.BlockSpec((None, nblk, HEAD, HEAD), lambda b, c: (layer, 0, 0, 0))
    y, h = pl.pallas_call(
        _rg_prompt_body,
        grid=(B, nT),
        in_specs=[pl.BlockSpec((tc, W), lambda b, c: (b * nT + c, 0)),
                  pl.BlockSpec((tc, W), lambda b, c: (b * nT + c, 1)),
                  pl.BlockSpec((None, CONV_W, W), lambda b, c: (layer, 0, 0)),
                  vec, blk, blk, vec, vec, vec],
        out_specs=[pl.BlockSpec((tc, W), lambda b, c: (b * nT + c, 0)),
                   pl.BlockSpec((None, 1, W), lambda b, c: (b, 0, 0))],
        out_shape=[jax.ShapeDtypeStruct((M, W), bf16), jax.ShapeDtypeStruct((B, 1, W), f32)],
        scratch_shapes=[pltpu.VMEM((8 + tc, W), f32), pltpu.VMEM((1, W), f32)],
        compiler_params=_params("arbitrary", "arbitrary"), name="rg_prompt")(
            proj, proj, lp["rg_conv_w"], lp["rg_conv_b"], lp["rg_wa"], lp["rg_wx"], lp["rg_ba"], lp["rg_bx"],
            lp["rg_a_param"])
    return y, h[:, 0]


def _rg_sample_body(x_ref, gate_ref, cs_ref, h0_ref, cw_ref, cb_ref, wa_ref, wx_ref, ba_ref, bx_ref, ap_ref,
                    yin_ref, y_ref, h_ref):
    del yin_ref
    nblk = x_ref.shape[1] // HEAD
    for n in range(nblk):
        ls = slice(n * HEAD, (n + 1) * HEAD)
        xc = cb_ref[:, ls] + cw_ref[CONV_W - 1:CONV_W, ls] * x_ref[:, ls]
        for j in range(CONV_W - 1):
            xc = xc + cw_ref[j:j + 1, ls] * cs_ref[j, :, ls]
        sp = _softplus(-ap_ref[:, ls])
        a, mult, i = _rg_gates(xc, wa_ref[n], wx_ref[n], ba_ref[:, ls], bx_ref[:, ls], sp)
        h = a * h0_ref[:, ls] + mult * (i * xc)
        h_ref[:, ls] = h
        y_ref[:, ls] = (h * jax.nn.gelu(gate_ref[:, ls], approximate=True)).astype(y_ref.dtype)


def _rg_sample(proj, y_all, row0, DB, conv_state_t, h0, lp, layer):
    W = lp["rg_ba"].shape[-1]
    nblk = W // HEAD
    assert row0 % DB == 0
    rb = row0 // DB
    vec = pl.BlockSpec((None, 1, W), lambda i: (layer, 0, 0))
    blk = pl.BlockSpec((None, nblk, HEAD, HEAD), lambda i: (layer, 0, 0, 0))
    y, h = pl.pallas_call(
        _rg_sample_body,
        grid=(1,),
        in_specs=[pl.BlockSpec((DB, W), lambda i: (rb, 0)),
                  pl.BlockSpec((DB, W), lambda i: (rb, 1)),
                  pl.BlockSpec((None, CONV_W - 1, DB, W), lambda i: (layer, 0, 0, 0)),
                  pl.BlockSpec((None, DB, W), lambda i: (layer, 0, 0)),
                  pl.BlockSpec((None, CONV_W, W), lambda i: (layer, 0, 0)),
                  vec, blk, blk, vec, vec, vec,
                  pl.BlockSpec(memory_space=pl.ANY)],
        out_specs=[pl.BlockSpec((DB, W), lambda i: (rb, 0)),
                   pl.BlockSpec((DB, W), lambda i: (0, 0))],
        out_shape=[jax.ShapeDtypeStruct(y_all.shape, y_all.dtype), jax.ShapeDtypeStruct((DB, W), f32)],
        input_output_aliases={11: 0},
        compiler_params=_params("arbitrary"), name="rg_sample")(
            proj, proj, conv_state_t, h0, lp["rg_conv_w"], lp["rg_conv_b"], lp["rg_wa"], lp["rg_wx"],
            lp["rg_ba"], lp["rg_bx"], lp["rg_a_param"], y_all)
    return y, h


def _hg_gates(fx, lb):
    f = lb + (1.0 - lb) * jax.nn.sigmoid(fx)
    k = (1.0 - lb) * jax.nn.sigmoid(-fx)
    return f, k


def _hg_intra_diag(G, q, k):
    trow = lax.broadcasted_iota(jnp.int32, (SUB, HEAD), 0)
    lane = lax.broadcasted_iota(jnp.int32, (SUB, CHUNK), 1)
    blocks = []
    for i in range(CHUNK // SUB):
        sl = slice(i * SUB, (i + 1) * SUB)
        g_i, q_i, k_i = G[sl], q[sl], k[sl]
        a_d = jnp.zeros((SUB, CHUNK), f32)
        for s in range(SUB):
            e = jnp.where(trow >= s, jnp.exp(g_i - g_i[s:s + 1, :]), 0.0)
            col = jnp.sum(q_i * k_i[s:s + 1, :] * e, axis=-1, keepdims=True)
            a_d = jnp.where(lane == i * SUB + s, col, a_d)
        blocks.append(a_d)
    return jnp.concatenate(blocks, axis=0)


def _hg_intra_off(G, q, k):
    nsub = CHUNK // SUB
    row = lax.broadcasted_iota(jnp.int32, (CHUNK, HEAD), 0)
    q_parts, k_parts = [], []
    for j in range(nsub - 1):
        g_e = G[(j + 1) * SUB - 1:(j + 1) * SUB, :]
        q_parts.append(jnp.where(row >= (j + 1) * SUB, q * jnp.exp(G - g_e), 0.0))
        in_j = jnp.logical_and(row >= j * SUB, row < (j + 1) * SUB)
        k_parts.append(jnp.where(in_j, k * jnp.exp(g_e - G), 0.0))
    return _mm(jnp.concatenate(q_parts, axis=1), jnp.concatenate(k_parts, axis=1), _NT)


def _hg_prompt_body(scale, HB, q_ref, f_ref, i_ref, g_ref, lb_ref, nw_ref, y_ref, s_ref, S_scr):
    c = pl.program_id(2)
    nchunk = q_ref.shape[0] // CHUNK

    @pl.when(c == 0)
    def _():
        S_scr[...] = jnp.zeros_like(S_scr)

    nw = nw_ref[...]
    tril = _chunk_tril(nchunk * CHUNK)
    work = []
    for hh in range(HB):
        ls = slice(hh * HEAD, (hh + 1) * HEAD)
        f_all, k_all = _hg_gates(f_ref[:, ls], lb_ref[:, ls])
        q_all = q_ref[:, ls] * scale
        G_all = _mm_exact_lhs(tril, jnp.log(f_all))
        for ci in range(nchunk):
            rows = slice(ci * CHUNK, (ci + 1) * CHUNK)
            work.append(dict(hh=hh, ls=ls, rows=rows, G=G_all[rows], q=q_all[rows], k=k_all[rows]))
    for w in work:
        w["A"] = _hg_intra_diag(w["G"], w["q"], w["k"])
    for w in work:
        w["A"] = w["A"] + _hg_intra_off(w["G"], w["q"], w["k"])
    for w in work:
        G = w["G"]
        kT, GT = w["k"].T, G.T
        g_last = GT[:, CHUNK - 1:CHUNK]
        w["dec"] = jnp.exp(g_last)
        w["upd"] = _mm(kT * jnp.exp(g_last - GT), i_ref[w["rows"], w["ls"]])
        w["lhs"] = jnp.concatenate([w["A"], w["q"] * jnp.exp(G)], axis=1)
    S = [S_scr[hh] for hh in range(HB)]
    for ci in range(nchunk):
        for hh in range(HB):
            w = work[hh * nchunk + ci]
            rows, ls = w["rows"], w["ls"]
            o = _mm(w["lhs"], jnp.concatenate([i_ref[rows, ls], S[hh]], axis=0))
            S[hh] = S[hh] * w["dec"] + w["upd"]
            y_ref[rows, ls] = _gated_rms(o, nw, g_ref[rows, ls]).astype(y_ref.dtype)
    for hh in range(HB):
        S_scr[hh] = S[hh]

    @pl.when(c == pl.num_programs(2) - 1)
    def _():
        for hh in range(HB):
            s_ref[hh] = S[hh]


def _hg_prompt(proj, B, T, M, H, col0, lb, norm_w, layer):
    tc = _pick(T, (256, 128, 64))
    nT = T // tc
    HB = HEADS_PER_STEP
    assert H % HB == 0 and col0 % HB == 0
    wb = HB * HEAD
    col = lambda j: pl.BlockSpec((tc, wb), lambda b, h, c: (b * nT + c, (col0 + j * H) // HB + h))
    y, S = pl.pallas_call(
        functools.partial(_hg_prompt_body, HEAD ** -0.5, HB),
        grid=(B, H // HB, nT),
        in_specs=[col(0), col(1), col(2), col(3),
                  pl.BlockSpec((None, 1, wb), lambda b, h, c: (layer, 0, h)),
                  pl.BlockSpec((None, 1, HEAD), lambda b, h, c: (layer, 0, 0))],
        out_specs=[pl.BlockSpec((tc, wb), lambda b, h, c: (b * nT + c, h)),
                   pl.BlockSpec((None, HB, HEAD, HEAD), lambda b, h, c: (b, h, 0, 0))],
        out_shape=[jax.ShapeDtypeStruct((M, H * HEAD), bf16), jax.ShapeDtypeStruct((B, H, HEAD, HEAD), f32)],
        scratch_shapes=[pltpu.VMEM((HB, HEAD, HEAD), f32)],
        compiler_params=_params("arbitrary", "arbitrary", "arbitrary"), name="hg_prompt")(
            proj, proj, proj, proj, lb, norm_w)
    return y, S


SB = 16


def _state_step(s_ref, so_ref, o_scr, d_rows, kT, qT, vnew_fn):
    for j in range(SB):
        S = s_ref[j]
        kcol = kT[:, j:j + 1]
        d, vnew = vnew_fn(j, S, kcol)
        Sn = d * S + kcol * vnew
        so_ref[j] = Sn
        o_scr[j:j + 1, :] = jnp.sum(qT[:, j:j + 1] * Sn, axis=0, keepdims=True)


def _hg_sample_body(scale, q_ref, f_ref, i_ref, g_ref, lb_ref, nw_ref, s_ref, *rest):
    y_ref, so_ref, o_scr = rest[-3:]
    f, k = _hg_gates(f_ref[...], lb_ref[...])
    q = q_ref[...] * scale
    v = i_ref[...]
    fT, kT, qT = f.T, k.T, q.T

    def vnew(j, S, kcol):
        return fT[:, j:j + 1], v[j:j + 1, :]

    _state_step(s_ref, so_ref, o_scr, None, kT, qT, vnew)
    y_ref[...] = _gated_rms(o_scr[...], nw_ref[...], g_ref[...]).astype(y_ref.dtype)


def _state_out(state, stacked_prev, n_in):
    extra_in, extra_specs, aliases = [], [], {}
    if stacked_prev is not None:
        extra_in, extra_specs, aliases = [stacked_prev], [pl.BlockSpec(memory_space=pl.ANY)], {n_in: 1}
    return jax.ShapeDtypeStruct(state.shape, f32), extra_in, extra_specs, aliases


def _hg_sample(proj, y_all, row0, DB, H, col0, lb, norm_w, state, stacked_prev, layer):
    assert row0 % SB == 0 and DB % SB == 0
    rb = row0 // SB
    col = lambda j: pl.BlockSpec((SB, HEAD), lambda h, b: (rb + b, col0 + j * H + h))
    st_spec = pl.BlockSpec((None, SB, None, HEAD, HEAD), lambda h, b: (layer, b, h, 0, 0))
    s_shape, extra_in, extra_specs, aliases = _state_out(state, stacked_prev, 8)
    y, S = pl.pallas_call(
        functools.partial(_hg_sample_body, HEAD ** -0.5),
        grid=(H, DB // SB),
        in_specs=[col(0), col(1), col(2), col(3),
                  pl.BlockSpec((None, 1, HEAD), lambda h, b: (layer, 0, h)),
                  pl.BlockSpec((None, 1, HEAD), lambda h, b: (layer, 0, 0)),
                  st_spec,
                  pl.BlockSpec(memory_space=pl.ANY)] + extra_specs,
        out_specs=[pl.BlockSpec((SB, HEAD), lambda h, b: (rb + b, h)), st_spec],
        out_shape=[jax.ShapeDtypeStruct(y_all.shape, y_all.dtype), s_shape],
        scratch_shapes=[pltpu.VMEM((SB, HEAD), f32)],
        input_output_aliases={7: 0, **aliases},
        compiler_params=_params("arbitrary", "arbitrary"), name="hg_sample")(
            proj, proj, proj, proj, lb, norm_w, state, y_all, *extra_in)
    return y, S


def _pick_lane(x, idx):
    lane = lax.broadcasted_iota(jnp.int32, x.shape, 1)
    col = jnp.sum(jnp.where(lane == idx, x, 0.0), axis=1, keepdims=True)
    return jnp.broadcast_to(col, (x.shape[0], HEAD))


def _gd_gate_body(H, PT, u_ref, w_ref, alog_ref, dtb_ref, o_ref, wb):
    i = pl.program_id(0)
    tm = u_ref.shape[0]

    @pl.when(i == 0)
    def _():
        wb[...] = w_ref[...].T.astype(bf16)

    x = jnp.dot(u_ref[...], wb[...], preferred_element_type=f32)
    lane = lax.broadcasted_iota(jnp.int32, x.shape, 1)
    row = lax.broadcasted_iota(jnp.int32, x.shape, 0) + i * tm
    g = jnp.where(lane < H, -jnp.exp(alog_ref[...]) * _softplus(x + dtb_ref[...]), 0.0)
    G = _mm_exact_lhs(_chunk_tril(tm), g)
    G = jnp.where(row < PT, G, g)
    o_ref[...] = jnp.where(lane < H, G, jax.nn.sigmoid(x))


def _gd_gate_proj(u, w_in_t, row_block, alog_pad, dtb_pad, PT, H, layer):
    M, D = u.shape
    tm = _pick(M, (640, 320, 128, 64))
    assert PT % CHUNK == 0 and tm % CHUNK == 0
    vec = pl.BlockSpec((None, 1, LANE), lambda i: (layer, 0, 0))
    return pl.pallas_call(
        functools.partial(_gd_gate_body, H, PT),
        grid=(M // tm,),
        in_specs=[pl.BlockSpec((tm, D), lambda i: (i, 0)),
                  pl.BlockSpec((None, LANE, D), lambda i: (layer, row_block, 0)), vec, vec],
        out_specs=pl.BlockSpec((tm, LANE), lambda i: (i, 0)),
        out_shape=jax.ShapeDtypeStruct((M, LANE), f32),
        scratch_shapes=[pltpu.VMEM((D, LANE), bf16)],
        compiler_params=_params("arbitrary"), name="gd_gate_proj")(u, w_in_t, alog_pad, dtb_pad)


def _unit_lower_inverses(Ns):
    r = lax.broadcasted_iota(jnp.int32, (CHUNK, CHUNK), 0)
    c = lax.broadcasted_iota(jnp.int32, (CHUNK, CHUNK), 1)
    same = lambda n: (r // n) == (c // n)
    assert CHUNK == 4 * SUB
    dot = functools.partial(jnp.dot, preferred_element_type=f32)
    cast = lambda xs: [x.astype(bf16) for x in xs]
    Rs = [jnp.where(same(SUB), N, 0.0) for N in Ns]
    Rb = cast(Rs)
    Ps = [dot(rb, rb) for rb in Rb]
    p = 2
    while p < SUB:
        Pb = cast(Ps)
        if 2 * p < SUB:
            PMs = [dot(pb, jnp.concatenate([rb, pb], axis=1)) for rb, pb in zip(Rb, Pb)]
            Rs = [R + P + PM[:, :CHUNK] for R, P, PM in zip(Rs, Ps, PMs)]
            Ps = [PM[:, CHUNK:] for PM in PMs]
            Rb = cast(Rs)
        else:
            PRs = [dot(pb, rb) for rb, pb in zip(Rb, Pb)]
            Rs = [R + P + PR for R, P, PR in zip(Rs, Ps, PRs)]
        p *= 2
    eye = jnp.where(r == c, 1.0, 0.0)
    Nb = cast(Ns)
    for n in (2 * SUB, 4 * SUB):
        off = jnp.logical_and(same(n), jnp.logical_not(same(n // 2)))
        Db = cast([eye + R for R in Rs])
        DCs = [dot(db, jnp.where(off, nb, jnp.zeros_like(nb))) for db, nb in zip(Db, Nb)]
        DCDs = [dot(dc, db) for dc, db in zip(cast(DCs), Db)]
        Rs = [R + DCD for R, DCD in zip(Rs, DCDs)]
    return Rs


def _gd_prompt_body(scale, H, HB, q_ref, k_ref, v_ref, z_ref, gt_ref, cwq_ref, cwk_ref, cwv_ref, nw_ref,
                    y_ref, s_ref, xq, xk, xv, S_scr):
    hb = pl.program_id(1)
    c = pl.program_id(2)
    tc = q_ref.shape[0]
    nchunk = tc // CHUNK

    @pl.when(c == 0)
    def _():
        for xb in (xq, xk, xv):
            xb[0:8, :] = jnp.zeros((8, xb.shape[1]), f32)
        S_scr[...] = jnp.zeros_like(S_scr)

    def conv_silu(x_ref, xb, cw_ref):
        xb[8:8 + tc, :] = x_ref[...]
        y = cw_ref[0:1, :] * xb[pl.ds(8 - (CONV_W - 1), tc), :]
        for j in range(1, CONV_W):
            y = y + cw_ref[j:j + 1, :] * xb[pl.ds(8 - (CONV_W - 1) + j, tc), :]
        xb[0:8, :] = xb[tc:tc + 8, :]
        return _silu(y)

    qc = conv_silu(q_ref, xq, cwq_ref)
    kc = conv_silu(k_ref, xk, cwk_ref)
    vc = conv_silu(v_ref, xv, cwv_ref)
    gt = gt_ref[...]
    nw = nw_ref[...]
    tril_b = _tril_mask(CHUNK)
    strict_b = _tril_mask(CHUNK, strict=True)

    work = []
    for hh in range(HB):
        ls = slice(hh * HEAD, (hh + 1) * HEAD)
        h = hb * HB + hh
        q_all = _l2norm(qc[:, ls]) * scale
        k_all = _l2norm(kc[:, ls])
        G_all = _pick_lane(gt, h)
        beta = _pick_lane(gt, H + h)
        eG = jnp.exp(G_all)
        kb_all = k_all * beta
        rhs_all = jnp.concatenate([vc[:, ls] * beta, kb_all * eG], axis=1)
        qe_all = q_all * eG
        for ci in range(nchunk):
            rows = slice(ci * CHUNK, (ci + 1) * CHUNK)
            work.append(dict(hh=hh, ls=ls, rows=rows, G=G_all[rows], q=q_all[rows], k=k_all[rows], kb=kb_all[rows],
                             X=rhs_all[rows], qe=qe_all[rows]))
    for w in work:
        G = w["G"]
        w["decay"] = jnp.where(tril_b, jnp.exp(G[:, :CHUNK] - G.T[:CHUNK, :]), 0.0)
        w["KQ"] = _mm(jnp.concatenate([w["kb"], w["q"]], axis=0), w["k"], _NT)
    Rs = _unit_lower_inverses([jnp.where(strict_b, -(w["KQ"][:CHUNK] * w["decay"]), 0.0) for w in work])
    for w, R in zip(work, Rs):
        w["R"] = R
        w["qk"] = jnp.where(tril_b, w["KQ"][CHUNK:] * w["decay"], 0.0)
    for w in work:
        X = w["X"]
        w["X"] = X + _mm(w["R"], X)
        G = w["G"]
        g_last = G[CHUNK - 1:CHUNK, :]
        w["egl"] = jnp.exp(g_last)
        w["rhs2"] = jnp.concatenate([w["qk"], (w["k"] * jnp.exp(g_last - G)).T], axis=0)
    S = [S_scr[hh] for hh in range(HB)]
    for ci in range(nchunk):
        for hh in range(HB):
            w = work[hh * nchunk + ci]
            rows, ls = w["rows"], w["ls"]
            WS = _mm(jnp.concatenate([w["X"][:, HEAD:], w["qe"]], axis=0), S[hh])
            v_new = w["X"][:, :HEAD] - WS[:CHUNK]
            OS = _mm(w["rhs2"], v_new)
            S[hh] = w["egl"] * S[hh] + OS[CHUNK:]
            y_ref[rows, ls] = _gated_rms(WS[CHUNK:] + OS[:CHUNK], nw, z_ref[rows, ls]).astype(y_ref.dtype)
    for hh in range(HB):
        S_scr[hh] = S[hh]

    @pl.when(c == pl.num_programs(2) - 1)
    def _():
        for hh in range(HB):
            s_ref[hh] = S[hh]


def _gd_prompt(proj, gates, B, T, M, H, col0, lp, layer):
    tc = _pick(T, (256, 128, 64))
    nT = T // tc
    HB = HEADS_PER_STEP
    assert H % HB == 0 and col0 % HB == 0
    wb = HB * HEAD
    col = lambda j: pl.BlockSpec((tc, wb), lambda b, h, c: (b * nT + c, (col0 + j * H) // HB + h))
    cw = lambda j: pl.BlockSpec((None, CONV_W, wb), lambda b, h, c: (layer, 0, j * H // HB + h))
    buf = pltpu.VMEM((8 + tc, wb), f32)
    y, S = pl.pallas_call(
        functools.partial(_gd_prompt_body, HEAD ** -0.5, H, HB),
        grid=(B, H // HB, nT),
        in_specs=[col(0), col(1), col(2), col(3),
                  pl.BlockSpec((tc, LANE), lambda b, h, c: (b * nT + c, 0)),
                  cw(0), cw(1), cw(2),
                  pl.BlockSpec((None, 1, HEAD), lambda b, h, c: (layer, 0, 0))],
        out_specs=[pl.BlockSpec((tc, wb), lambda b, h, c: (b * nT + c, h)),
                   pl.BlockSpec((None, HB, HEAD, HEAD), lambda b, h, c: (b, h, 0, 0))],
        out_shape=[jax.ShapeDtypeStruct((M, H * HEAD), bf16), jax.ShapeDtypeStruct((B, H, HEAD, HEAD), f32)],
        scratch_shapes=[buf, buf, buf, pltpu.VMEM((HB, HEAD, HEAD), f32)],
        compiler_params=_params("arbitrary", "arbitrary", "arbitrary"), name="gd_prompt")(
            proj, proj, proj, proj, gates, lp["gd_conv_w"], lp["gd_conv_w"], lp["gd_conv_w"], lp["gd_norm_w"])
    return y, S


def _gd_sample_body(scale, H, q_ref, k_ref, v_ref, z_ref, gt_ref, csq_ref, csk_ref, csv_ref, cwq_ref, cwk_ref, cwv_ref,
                    nw_ref, s_ref, *rest):
    y_ref, so_ref, o_scr = rest[-3:]
    h = pl.program_id(0)

    def conv_silu(x_ref, cs_ref, cw_ref):
        y = cw_ref[CONV_W - 1:CONV_W, :] * x_ref[...]
        for j in range(CONV_W - 1):
            y = y + cw_ref[j:j + 1, :] * cs_ref[j]
        return _silu(y)

    q = _l2norm(conv_silu(q_ref, csq_ref, cwq_ref)) * scale
    k = _l2norm(conv_silu(k_ref, csk_ref, cwk_ref))
    v = conv_silu(v_ref, csv_ref, cwv_ref)
    gt = gt_ref[...]
    eg = jnp.exp(_pick_lane(gt, h))
    beta = _pick_lane(gt, H + h)
    kT, qT = k.T, q.T

    def vnew(j, S, kcol):
        egj = eg[j:j + 1, :]
        kS = jnp.sum(kcol * S, axis=0, keepdims=True)
        return egj, beta[j:j + 1, :] * (v[j:j + 1, :] - egj * kS)

    _state_step(s_ref, so_ref, o_scr, None, kT, qT, vnew)
    y_ref[...] = _gated_rms(o_scr[...], nw_ref[...], z_ref[...]).astype(y_ref.dtype)


def _gd_sample(proj, gates, y_all, row0, DB, H, col0, conv_state_t, state, stacked_prev, lp, layer):
    assert row0 % SB == 0 and DB % SB == 0
    rb = row0 // SB
    col = lambda j: pl.BlockSpec((SB, HEAD), lambda h, b: (rb + b, col0 + j * H + h))
    cs = lambda j: pl.BlockSpec((None, CONV_W - 1, SB, HEAD), lambda h, b: (layer, 0, b, j * H + h))
    cw = lambda j: pl.BlockSpec((None, CONV_W, HEAD), lambda h, b: (layer, 0, j * H + h))
    st_spec = pl.BlockSpec((None, SB, None, HEAD, HEAD), lambda h, b: (layer, b, h, 0, 0))
    s_shape, extra_in, extra_specs, aliases = _state_out(state, stacked_prev, 14)
    y, S = pl.pallas_call(
        functools.partial(_gd_sample_body, HEAD ** -0.5, H),
        grid=(H, DB // SB),
        in_specs=[col(0), col(1), col(2), col(3),
                  pl.BlockSpec((SB, LANE), lambda h, b: (rb + b, 0)),
                  cs(0), cs(1), cs(2), cw(0), cw(1), cw(2),
                  pl.BlockSpec((None, 1, HEAD), lambda h, b: (layer, 0, 0)),
                  st_spec,
                  pl.BlockSpec(memory_space=pl.ANY)] + extra_specs,
        out_specs=[pl.BlockSpec((SB, HEAD), lambda h, b: (rb + b, h)), st_spec],
        out_shape=[jax.ShapeDtypeStruct(y_all.shape, y_all.dtype), s_shape],
        scratch_shapes=[pltpu.VMEM((SB, HEAD), f32)],
        input_output_aliases={13: 0, **aliases},
        compiler_params=_params("arbitrary", "arbitrary"), name="gd_sample")(
            proj, proj, proj, proj, gates, conv_state_t, conv_state_t, conv_state_t,
            lp["gd_conv_w"], lp["gd_conv_w"], lp["gd_conv_w"], lp["gd_norm_w"],
            state, y_all, *extra_in)
    return y, S


def kernel(x_prompt, x_sample, state_rg_h, state_rg_conv, state_hg_S, state_gd_S, state_gd_conv, norm_mix_w, norm_mlp_w, norm_final_w, w_in, rg_conv_w, rg_conv_b, rg_wa, rg_ba, rg_wx, rg_bx, rg_a_param, hg_lb_logits, hg_norm_w, gd_conv_w, gd_A_log, gd_dt_bias, gd_norm_w, w_br_rg, w_br_hg, w_br_gd, w_out, w_up, w_down):
    B, T, D = x_prompt.shape
    DB, DT, _ = x_sample.shape
    assert DT == 1
    depth = w_in.shape[0]
    RW = rg_ba.shape[-1]
    H = gd_A_log.shape[-1]
    PT = B * T
    M = PT + DB
    assert RW % HEAD == 0 and hg_norm_w.shape[-1] == HEAD and gd_norm_w.shape[-1] == HEAD
    HW = H * HEAD
    n_main = 2 * RW + 8 * HW
    merge_col0 = n_main + 2 * H
    assert w_in.shape[-1] == merge_col0 + 3 * D and n_main % 1024 == 0 and n_main % LANE == 0
    hg_col0 = 2 * RW // HEAD
    gd_col0 = hg_col0 + 4 * H

    row3 = lambda a: a.reshape(depth, 1, a.shape[-1])
    lane_pad = lambda a: row3(jnp.pad(a.astype(f32), ((0, 0), (0, LANE - a.shape[-1]))))
    lp = dict(rg_conv_w=rg_conv_w, rg_conv_b=row3(rg_conv_b), rg_wa=rg_wa, rg_wx=rg_wx, rg_ba=row3(rg_ba),
              rg_bx=row3(rg_bx), rg_a_param=row3(rg_a_param), gd_conv_w=gd_conv_w, gd_norm_w=row3(gd_norm_w))
    alog_pad, dtb_pad = lane_pad(gd_A_log), lane_pad(gd_dt_bias)
    hg_nw = row3(hg_norm_w)
    lb = row3(_lower_bounds(hg_lb_logits.astype(f32)))
    rg_conv_t = jnp.swapaxes(state_rg_conv, 1, 2)
    gd_conv_t = jnp.swapaxes(state_gd_conv, 1, 2)

    w_in_t = jnp.swapaxes(w_in, 1, 2)
    x = jnp.concatenate([x_prompt.reshape(PT, D), x_sample.reshape(DB, D)], axis=0)
    _, u = _add_norm(x, None, norm_mix_w[0:1], bf16, emit_x=False)

    p_states, s_states = [], []
    y_final = None
    s_hgS = s_gdS = None
    for l in range(depth):
        proj = _gemm_wres(u, w_in_t, l, 0, n_main, 1024, name="in_proj", w_is_nk=True)
        gates = _gd_gate_proj(u, w_in_t, n_main // LANE, alog_pad, dtb_pad, PT, H, l)

        y_rg, p_h = _rg_prompt(proj, B, T, M, lp, l)
        y_rg, s_h = _rg_sample(proj, y_rg, PT, DB, rg_conv_t, state_rg_h, lp, l)
        y_hg, p_hgS = _hg_prompt(proj, B, T, M, H, hg_col0, lb, hg_nw, l)
        y_hg, s_hgS = _hg_sample(proj, y_hg, PT, DB, H, hg_col0, lb, hg_nw, state_hg_S, s_hgS, l)
        y_gd, p_gdS = _gd_prompt(proj, gates, B, T, M, H, gd_col0, lp, l)
        y_gd, s_gdS = _gd_sample(proj, gates, y_gd, PT, DB, H, gd_col0, gd_conv_t, state_gd_S, s_gdS, lp, l)

        gq = gd_col0 * HEAD
        tail = lambda c0, w: jnp.stack([lax.slice(proj, (b * T + T - (CONV_W - 1), c0), (b * T + T, c0 + w))
                                        for b in range(B)], axis=0)
        last = lambda c0, w: lax.slice(proj, (PT, c0), (M, c0 + w))[:, None, :]
        p_states.append((p_h, tail(0, RW), p_hgS, p_gdS, tail(gq, 3 * HW)))
        s_states.append((s_h,
                         jnp.concatenate([state_rg_conv[l][:, 1:], last(0, RW)], axis=1),
                         None, None,
                         jnp.concatenate([state_gd_conv[l][:, 1:], last(gq, 3 * HW)], axis=1)))

        mixed = _mix(u, y_rg, y_hg, y_gd, w_in_t, w_br_rg, w_br_hg, w_br_gd, l, merge_col0)
        x, hmid = _proj_add_norm(mixed, w_out, l, x, norm_mlp_w[l:l + 1])
        hh = _gemm_wres(hmid, w_up, l, 0, w_up.shape[-1], 1024, epi=lambda a: jnp.square(jnp.maximum(a, 0.0)),
                        out_dtype=bf16, name="mlp_up")
        d2 = _gemm_wres(hh, w_down, l, 0, D, 512, name="mlp_down", tm_cands=(208, 128, 64, 16), single_buffer_w=True)
        if l + 1 < depth:
            x, u = _add_norm(x, d2, norm_mix_w[l + 1:l + 2], bf16, emit_x=True)
        else:
            _, y_final = _add_norm(x, d2, norm_final_w.reshape(1, D), f32, emit_x=False)

    def stack(sts, j, like):
        return jnp.stack([s[j] for s in sts], axis=0).astype(like.dtype)

    return (y_final[:PT].reshape(B, T, D), y_final[PT:].reshape(DB, DT, D),
            stack(p_states, 0, state_rg_h), stack(p_states, 1, state_rg_conv), stack(p_states, 2, state_hg_S),
            stack(p_states, 3, state_gd_S), stack(p_states, 4, state_gd_conv),
            stack(s_states, 0, state_rg_h), stack(s_states, 1, state_rg_conv), s_hgS.astype(state_hg_S.dtype),
            s_gdS.astype(state_gd_S.dtype), stack(s_states, 4, state_gd_conv))
```

```python
import functools

import jax
import jax.numpy as jnp
from jax import lax
from jax.experimental import pallas as pl
from jax.experimental.pallas import tpu as pltpu

f32 = jnp.float32
bf16 = jnp.bfloat16

EPS = 1e-6
RG_C = 8.0
HEAD = 128
LANE = 128
SUBLANES = 8
CHUNK = 64
SUB = 16
HEADS_PER_STEP = 8
CONV_W = 4
VMEM_LIMIT = 56 * 1024 * 1024

_NT = (((1,), (1,)), ((), ()))
_TN = (((0,), (0,)), ((), ()))


def _params(*sem):
    return pltpu.CompilerParams(dimension_semantics=sem, vmem_limit_bytes=VMEM_LIMIT)


def _pick(n, cands):
    for c in cands:
        if n % c == 0:
            return c
    raise ValueError(f"no tile for {n} among {cands}")


def _mm(a, b, dims=None):
    a = a.astype(bf16)
    b = b.astype(bf16)
    if dims is None:
        return jnp.dot(a, b, preferred_element_type=f32)
    return lax.dot_general(a, b, dims, preferred_element_type=f32)


def _split3(x):
    hi = x.astype(bf16)
    r = x - hi.astype(f32)
    mid = r.astype(bf16)
    lo = (r - mid.astype(f32)).astype(bf16)
    return hi, mid, lo


def _mm_exact_lhs(a_bf16, x):
    hi, mid, lo = _split3(x)
    return (jnp.dot(a_bf16, hi, preferred_element_type=f32) + jnp.dot(a_bf16, mid, preferred_element_type=f32)
            + jnp.dot(a_bf16, lo, preferred_element_type=f32))


def _mm_hi(a, b):
    ah, am, _ = _split3(a)
    bh, bm, _ = _split3(b)
    d = functools.partial(jnp.dot, preferred_element_type=f32)
    return d(ah, bh) + (d(ah, bm) + d(am, bh))


def _expm1_neg(x, ex):
    return -jnp.tanh(0.5 * x) * (ex + 1.0)


def _softplus(x):
    return jnp.maximum(x, 0.0) + jnp.log1p(jnp.exp(-jnp.abs(x)))


def _silu(x):
    return x * jax.nn.sigmoid(x)


def _gated_rms(o, w, z):
    o = o * lax.rsqrt(jnp.mean(o * o, axis=-1, keepdims=True) + EPS) * w
    return o * _silu(z)


def _l2norm(x):
    return x * lax.rsqrt(jnp.sum(x * x, axis=-1, keepdims=True) + EPS)


def _causal_conv4(xbuf, cw_ref, tc, ls=slice(None)):
    assert CONV_W == 4
    w0, w1, w2, w3 = (cw_ref[j:j + 1, ls] for j in range(CONV_W))
    x0 = xbuf[8:8 + tc, ls]
    xm2 = xbuf[pl.ds(6, tc), ls]
    even = w3 * x0 + w1 * xm2
    odd = w2 * x0 + w0 * xm2
    odd_before = w2 * xbuf[7:8, ls] + w0 * xbuf[5:6, ls]
    row = lax.broadcasted_iota(jnp.int32, x0.shape, 0)
    return even + jnp.where(row == 0, odd_before, pltpu.roll(odd, 1, axis=0))


def _tril_mask(n, strict=False):
    r = lax.broadcasted_iota(jnp.int32, (n, n), 0)
    c = lax.broadcasted_iota(jnp.int32, (n, n), 1)
    return (r > c) if strict else (r >= c)


def _chunk_tril(n):
    r = lax.broadcasted_iota(jnp.int32, (n, n), 0)
    c = lax.broadcasted_iota(jnp.int32, (n, n), 1)
    same = (r // CHUNK) == (c // CHUNK)
    return jnp.where(jnp.logical_and(r >= c, same), 1.0, 0.0).astype(bf16)


def _norm_body(has_delta, emit_x, *refs):
    refs = list(refs)
    x_ref = refs.pop(0)
    d_ref = refs.pop(0) if has_delta else None
    w_ref = refs.pop(0)
    xo_ref = refs.pop(0) if emit_x else None
    n_ref = refs.pop(0)
    x = x_ref[...]
    if has_delta:
        x = x + d_ref[...]
    if emit_x:
        xo_ref[...] = x
    y = x * lax.rsqrt(jnp.mean(x * x, axis=-1, keepdims=True) + EPS)
    n_ref[...] = (y * w_ref[...]).astype(n_ref.dtype)


def _add_norm(x, delta, w_row, out_dtype, emit_x):
    M, D = x.shape
    tm = _pick(M, (416, 320, 256, 128, 64, 16))
    row = pl.BlockSpec((tm, D), lambda m: (m, 0))
    in_specs = [row] + ([row] if delta is not None else []) + [pl.BlockSpec((1, D), lambda m: (0, 0))]
    out_shape = ([jax.ShapeDtypeStruct((M, D), f32)] if emit_x else []) + [jax.ShapeDtypeStruct((M, D), out_dtype)]
    out_specs = ([row] if emit_x else []) + [row]
    args = [x] + ([delta] if delta is not None else []) + [w_row]
    out = pl.pallas_call(
        functools.partial(_norm_body, delta is not None, emit_x),
        grid=(M // tm,), in_specs=in_specs, out_specs=out_specs, out_shape=out_shape,
        compiler_params=_params("arbitrary"), name="add_norm")(*args)
    return out if emit_x else (None, out[0])


def _gemm_wres_body(epi, w_is_nk, a_ref, w_ref, o_ref, wb):
    @pl.when(pl.program_id(1) == 0)
    def _():
        w = w_ref[...]
        wb[...] = (w.T if w_is_nk else w).astype(bf16)
    acc = jnp.dot(a_ref[...], wb[...], preferred_element_type=f32)
    if epi is not None:
        acc = epi(acc)
    o_ref[...] = acc.astype(o_ref.dtype)


BIG_TM = (1040, 640, 512, 320, 256, 128, 64, 16)


def _gemm_wres(a, w, layer, col_block0, n_out, tn, epi=None, out_dtype=f32, name="gemm", w_is_nk=False,
               tm_cands=(640, 512, 320, 256, 128, 64, 16), single_buffer_w=False):
    M, K = a.shape
    tm = _pick(M, tm_cands)
    mode = dict(pipeline_mode=pl.Buffered(1)) if single_buffer_w else {}
    if w_is_nk:
        w_spec = pl.BlockSpec((None, tn, K), lambda n, m: (layer, n + col_block0, 0), **mode)
    else:
        w_spec = pl.BlockSpec((None, K, tn), lambda n, m: (layer, 0, n + col_block0), **mode)
    return pl.pallas_call(
        functools.partial(_gemm_wres_body, epi, w_is_nk),
        grid=(n_out // tn, M // tm),
        in_specs=[pl.BlockSpec((tm, K), lambda n, m: (m, 0)), w_spec],
        out_specs=pl.BlockSpec((tm, tn), lambda n, m: (m, n)),
        out_shape=jax.ShapeDtypeStruct((M, n_out), out_dtype),
        scratch_shapes=[pltpu.VMEM((K, tn), bf16)],
        compiler_params=_params("arbitrary", "arbitrary"), name=name)(a, w)


def _proj_norm_body(a_ref, w_ref, x_ref, nw_ref, xo_ref, n_ref, wb):
    @pl.when(pl.program_id(0) == 0)
    def _():
        wb[...] = w_ref[...].astype(bf16)
    x = x_ref[...] + jnp.dot(a_ref[...], wb[...], preferred_element_type=f32)
    xo_ref[...] = x
    y = x * lax.rsqrt(jnp.mean(x * x, axis=-1, keepdims=True) + EPS)
    n_ref[...] = (y * nw_ref[...]).astype(n_ref.dtype)


def _proj_add_norm(a, w, layer, x, nw_row):
    M, K = a.shape
    D = w.shape[-1]
    tm = _pick(M, (320, 256, 128, 64, 16))
    row = lambda width: pl.BlockSpec((tm, width), lambda m: (m, 0))
    return pl.pallas_call(
        _proj_norm_body,
        grid=(M // tm,),
        in_specs=[row(K), pl.BlockSpec((None, K, D), lambda m: (layer, 0, 0), pipeline_mode=pl.Buffered(1)),
                  row(D), pl.BlockSpec((1, D), lambda m: (0, 0))],
        out_specs=[row(D), row(D)],
        out_shape=[jax.ShapeDtypeStruct((M, D), f32), jax.ShapeDtypeStruct((M, D), bf16)],
        scratch_shapes=[pltpu.VMEM((K, D), bf16)],
        compiler_params=_params("arbitrary"), name="out_proj_norm")(a, w, x, nw_row)


MERGE_SHIFT = 16


def _mix_body(u_ref, yr_ref, yh_ref, yg_ref, wm0, wm1, wm2, wx0, wx1, wx2, wr_ref, wh_ref, wg_ref, o_ref, wmb, wbb):
    tn = o_ref.shape[1]

    @pl.when(pl.program_id(1) == 0)
    def _():
        for b, (wm, wx) in enumerate(((wm0, wx0), (wm1, wx1), (wm2, wx2))):
            wcat = jnp.concatenate([wm[...], wx[...]], axis=0)
            wmb[b] = wcat[MERGE_SHIFT:MERGE_SHIFT + tn].T.astype(bf16)
        for b, wr in enumerate((wr_ref, wh_ref, wg_ref)):
            wbb[b] = wr[...].astype(bf16)

    u = u_ref[...]
    acc = None
    for b, y_ref in enumerate((yr_ref, yh_ref, yg_ref)):
        gate = jax.nn.sigmoid(jnp.dot(u, wmb[b], preferred_element_type=f32))
        p = jnp.dot(y_ref[...], wbb[b], preferred_element_type=f32)
        acc = gate * p if acc is None else acc + gate * p
    o_ref[...] = acc.astype(o_ref.dtype)


def _mix(u, y_rg, y_hg, y_gd, w_in_t, w_br_rg, w_br_hg, w_br_gd, layer, merge_col0):
    M, D = u.shape
    W = y_rg.shape[1]
    tn = 256
    tm = _pick(M, (640, 416, 320, 256, 128, 64, 16))
    nt = D // tn
    assert (merge_col0 - MERGE_SHIFT) % tn == 0 and tn % MERGE_SHIFT == 0
    base = (merge_col0 - MERGE_SHIFT) // tn
    r = tn // MERGE_SHIFT

    def wm_spec(b):
        return pl.BlockSpec((None, tn, D), lambda n, m: (layer, base + b * nt + n, 0))

    def wx_spec(b):
        return pl.BlockSpec((None, MERGE_SHIFT, D), lambda n, m: (layer, (base + b * nt + n + 1) * r, 0))

    row = lambda w: pl.BlockSpec((tm, w), lambda n, m: (m, 0))
    br = pl.BlockSpec((None, W, tn), lambda n, m: (layer, 0, n))
    return pl.pallas_call(
        _mix_body,
        grid=(nt, M // tm),
        in_specs=[row(D), row(W), row(W), row(W), wm_spec(0), wm_spec(1), wm_spec(2),
                  wx_spec(0), wx_spec(1), wx_spec(2), br, br, br],
        out_specs=pl.BlockSpec((tm, tn), lambda n, m: (m, n)),
        out_shape=jax.ShapeDtypeStruct((M, D), bf16),
        scratch_shapes=[pltpu.VMEM((3, D, tn), bf16), pltpu.VMEM((3, W, tn), bf16)],
        compiler_params=_params("arbitrary", "arbitrary"), name="mix")(
            u, y_rg, y_hg, y_gd, w_in_t, w_in_t, w_in_t, w_in_t, w_in_t, w_in_t, w_br_rg, w_br_hg, w_br_gd)


def _lb_body(x_ref, o_ref):
    x = x_ref[...]
    depth = x.shape[0]
    m = jnp.max(x, axis=0, keepdims=True)
    e = jnp.exp(x - m)
    p = e / jnp.sum(e, axis=0, keepdims=True)
    acc = jnp.zeros_like(p[0:1])
    o_ref[0:1, :] = acc
    for l in range(1, depth):
        acc = acc + p[l:l + 1]
        o_ref[l:l + 1, :] = acc


def _lower_bounds(logits):
    return pl.pallas_call(_lb_body, out_shape=jax.ShapeDtypeStruct(logits.shape, f32), name="hg_lower_bounds")(logits)


def _rg_gates(xc, wa, wx, ba, bx, sp):
    xb = xc.astype(bf16)
    r = jax.nn.sigmoid(jnp.dot(xb, wa.astype(bf16), preferred_element_type=f32) + ba)
    i = jax.nn.sigmoid(jnp.dot(xb, wx.astype(bf16), preferred_element_type=f32) + bx)
    log_a = (-RG_C) * r * sp
    a = jnp.exp(log_a)
    m2 = _expm1_neg(2.0 * log_a, a * a)
    mult = jnp.where(m2 > 0.0, m2 * lax.rsqrt(m2), 0.0)
    return a, mult, i


def _rg_prompt_body(x_ref, gate_ref, cw_ref, cb_ref, wa_ref, wx_ref, ba_ref, bx_ref, ap_ref, y_ref, h_ref, xbuf, hprev):
    c = pl.program_id(1)
    tc = x_ref.shape[0]
    nblk = x_ref.shape[1] // HEAD

    @pl.when(c == 0)
    def _():
        xbuf[0:8, :] = jnp.zeros((8, xbuf.shape[1]), f32)
        hprev[...] = jnp.zeros_like(hprev)

    xbuf[8:8 + tc, :] = x_ref[...]
    row = lax.broadcasted_iota(jnp.int32, (tc, HEAD), 0)
    first = jnp.logical_and(row == 0, c == 0)
    for n in range(nblk):
        ls = slice(n * HEAD, (n + 1) * HEAD)
        xc = _causal_conv4(xbuf, cw_ref, tc, ls) + cb_ref[:, ls]
        sp = _softplus(-ap_ref[:, ls])
        a, mult, i = _rg_gates(xc, wa_ref[n], wx_ref[n], ba_ref[:, ls], bx_ref[:, ls], sp)
        mult = jnp.where(first, 1.0, mult)
        b = mult * (i * xc)
        s = 1
        while s < SUBLANES:
            keep = (row % SUBLANES) >= s
            a_sh = jnp.where(keep, pltpu.roll(a, s, axis=0), 1.0)
            b_sh = jnp.where(keep, pltpu.roll(b, s, axis=0), 0.0)
            b = a * b_sh + b
            a = a * a_sh
            s *= 2
        carry = hprev[:, ls]
        groups = []
        for g in range(tc // SUBLANES):
            rows = slice(g * SUBLANES, (g + 1) * SUBLANES)
            hg = b[rows] + a[rows] * carry
            groups.append(hg)
            carry = hg[SUBLANES - 1:SUBLANES, :]
        h = jnp.concatenate(groups, axis=0)
        hprev[:, ls] = carry
        y_ref[:, ls] = (h * jax.nn.gelu(gate_ref[:, ls], approximate=True)).astype(y_ref.dtype)
    xbuf[0:8, :] = xbuf[tc:tc + 8, :]
    h_ref[...] = hprev[...]


def _rg_prompt(proj, B, T, M, lp, layer):
    W = lp["rg_ba"].shape[-1]
    tc = _pick(T, (256, 128, 64))
    nT = T // tc
    nblk = W // HEAD
    vec = pl.BlockSpec((None, 1, W), lambda b, c: (layer, 0, 0))
    blk = pl.BlockSpec((None, nblk, HEAD, HEAD), lambda b, c: (layer, 0, 0, 0))
    y, h = pl.pallas_call(
        _rg_prompt_body,
        grid=(B, nT),
        in_specs=[pl.BlockSpec((tc, W), lambda b, c: (b * nT + c, 0)),
                  pl.BlockSpec((tc, W), lambda b, c: (b * nT + c, 1)),
                  pl.BlockSpec((None, CONV_W, W), lambda b, c: (layer, 0, 0)),
                  vec, blk, blk, vec, vec, vec],
        out_specs=[pl.BlockSpec((tc, W), lambda b, c: (b * nT + c, 0)),
                   pl.BlockSpec((None, 1, W), lambda b, c: (b, 0, 0))],
        out_shape=[jax.ShapeDtypeStruct((M, W), bf16), jax.ShapeDtypeStruct((B, 1, W), f32)],
        scratch_shapes=[pltpu.VMEM((8 + tc, W), f32), pltpu.VMEM((1, W), f32)],
        compiler_params=_params("arbitrary", "arbitrary"), name="rg_prompt")(
            proj, proj, lp["rg_conv_w"], lp["rg_conv_b"], lp["rg_wa"], lp["rg_wx"], lp["rg_ba"], lp["rg_bx"],
            lp["rg_a_param"])
    return y, h[:, 0]


def _rg_sample_body(x_ref, gate_ref, cs_ref, h0_ref, cw_ref, cb_ref, wa_ref, wx_ref, ba_ref, bx_ref, ap_ref,
                    yin_ref, y_ref, h_ref):
    del yin_ref
    nblk = x_ref.shape[1] // HEAD
    for n in range(nblk):
        ls = slice(n * HEAD, (n + 1) * HEAD)
        xc = cb_ref[:, ls] + cw_ref[CONV_W - 1:CONV_W, ls] * x_ref[:, ls]
        for j in range(CONV_W - 1):
            xc = xc + cw_ref[j:j + 1, ls] * cs_ref[j, :, ls]
        sp = _softplus(-ap_ref[:, ls])
        a, mult, i = _rg_gates(xc, wa_ref[n], wx_ref[n], ba_ref[:, ls], bx_ref[:, ls], sp)
        h = a * h0_ref[:, ls] + mult * (i * xc)
        h_ref[:, ls] = h
        y_ref[:, ls] = (h * jax.nn.gelu(gate_ref[:, ls], approximate=True)).astype(y_ref.dtype)


def _rg_sample(proj, y_all, row0, DB, conv_state_t, h0, lp, layer):
    W = lp["rg_ba"].shape[-1]
    nblk = W // HEAD
    assert row0 % DB == 0
    rb = row0 // DB
    vec = pl.BlockSpec((None, 1, W), lambda i: (layer, 0, 0))
    blk = pl.BlockSpec((None, nblk, HEAD, HEAD), lambda i: (layer, 0, 0, 0))
    y, h = pl.pallas_call(
        _rg_sample_body,
        grid=(1,),
        in_specs=[pl.BlockSpec((DB, W), lambda i: (rb, 0)),
                  pl.BlockSpec((DB, W), lambda i: (rb, 1)),
                  pl.BlockSpec((None, CONV_W - 1, DB, W), lambda i: (layer, 0, 0, 0)),
                  pl.BlockSpec((None, DB, W), lambda i: (layer, 0, 0)),
                  pl.BlockSpec((None, CONV_W, W), lambda i: (layer, 0, 0)),
                  vec, blk, blk, vec, vec, vec,
                  pl.BlockSpec(memory_space=pl.ANY)],
        out_specs=[pl.BlockSpec((DB, W), lambda i: (rb, 0)),
                   pl.BlockSpec((DB, W), lambda i: (0, 0))],
        out_shape=[jax.ShapeDtypeStruct(y_all.shape, y_all.dtype), jax.ShapeDtypeStruct((DB, W), f32)],
        input_output_aliases={11: 0},
        compiler_params=_params("arbitrary"), name="rg_sample")(
            proj, proj, conv_state_t, h0, lp["rg_conv_w"], lp["rg_conv_b"], lp["rg_wa"], lp["rg_wx"],
            lp["rg_ba"], lp["rg_bx"], lp["rg_a_param"], y_all)
    return y, h


def _hg_gates(fx, lb):
    f = lb + (1.0 - lb) * jax.nn.sigmoid(fx)
    k = (1.0 - lb) * jax.nn.sigmoid(-fx)
    return f, k


def _hg_intra_diag(G, q, k):
    trow = lax.broadcasted_iota(jnp.int32, (SUB, HEAD), 0)
    lane = lax.broadcasted_iota(jnp.int32, (SUB, CHUNK), 1)
    blocks = []
    for i in range(CHUNK // SUB):
        sl = slice(i * SUB, (i + 1) * SUB)
        g_i, q_i, k_i = G[sl], q[sl], k[sl]
        a_d = jnp.zeros((SUB, CHUNK), f32)
        for s in range(SUB):
            e = jnp.where(trow >= s, jnp.exp(g_i - g_i[s:s + 1, :]), 0.0)
            col = jnp.sum(q_i * k_i[s:s + 1, :] * e, axis=-1, keepdims=True)
            a_d = jnp.where(lane == i * SUB + s, col, a_d)
        blocks.append(a_d)
    return jnp.concatenate(blocks, axis=0)


def _hg_intra_off(G, q, k):
    nsub = CHUNK // SUB
    row = lax.broadcasted_iota(jnp.int32, (CHUNK, HEAD), 0)
    q_parts, k_parts = [], []
    for j in range(nsub - 1):
        g_e = G[(j + 1) * SUB - 1:(j + 1) * SUB, :]
        q_parts.append(jnp.where(row >= (j + 1) * SUB, q * jnp.exp(G - g_e), 0.0))
        in_j = jnp.logical_and(row >= j * SUB, row < (j + 1) * SUB)
        k_parts.append(jnp.where(in_j, k * jnp.exp(g_e - G), 0.0))
    return _mm(jnp.concatenate(q_parts, axis=1), jnp.concatenate(k_parts, axis=1), _NT)


def _hg_prompt_body(scale, HB, q_ref, f_ref, i_ref, g_ref, lb_ref, nw_ref, y_ref, s_ref, S_scr):
    c = pl.program_id(2)
    nchunk = q_ref.shape[0] // CHUNK

    @pl.when(c == 0)
    def _():
        S_scr[...] = jnp.zeros_like(S_scr)

    nw = nw_ref[...]
    tril = _chunk_tril(nchunk * CHUNK)
    work = []
    for hh in range(HB):
        ls = slice(hh * HEAD, (hh + 1) * HEAD)
        f_all, k_all = _hg_gates(f_ref[:, ls], lb_ref[:, ls])
        q_all = q_ref[:, ls] * scale
        G_all = _mm_exact_lhs(tril, jnp.log(f_all))
        for ci in range(nchunk):
            rows = slice(ci * CHUNK, (ci + 1) * CHUNK)
            work.append(dict(hh=hh, ls=ls, rows=rows, G=G_all[rows], q=q_all[rows], k=k_all[rows]))
    for w in work:
        w["A"] = _hg_intra_diag(w["G"], w["q"], w["k"])
    for w in work:
        w["A"] = w["A"] + _hg_intra_off(w["G"], w["q"], w["k"])
    for w in work:
        G = w["G"]
        kT, GT = w["k"].T, G.T
        g_last = GT[:, CHUNK - 1:CHUNK]
        w["dec"] = jnp.exp(g_last)
        w["upd"] = _mm(kT * jnp.exp(g_last - GT), i_ref[w["rows"], w["ls"]])
        w["lhs"] = jnp.concatenate([w["A"], w["q"] * jnp.exp(G)], axis=1)
    S = [S_scr[hh] for hh in range(HB)]
    for ci in range(nchunk):
        for hh in range(HB):
            w = work[hh * nchunk + ci]
            rows, ls = w["rows"], w["ls"]
            o = _mm(w["lhs"], jnp.concatenate([i_ref[rows, ls], S[hh]], axis=0))
            S[hh] = S[hh] * w["dec"] + w["upd"]
            y_ref[rows, ls] = _gated_rms(o, nw, g_ref[rows, ls]).astype(y_ref.dtype)
    for hh in range(HB):
        S_scr[hh] = S[hh]

    @pl.when(c == pl.num_programs(2) - 1)
    def _():
        for hh in range(HB):
            s_ref[hh] = S[hh]


def _hg_prompt(proj, B, T, M, H, col0, lb, norm_w, layer):
    tc = _pick(T, (256, 128, 64))
    nT = T // tc
    HB = HEADS_PER_STEP
    assert H % HB == 0 and col0 % HB == 0
    wb = HB * HEAD
    col = lambda j: pl.BlockSpec((tc, wb), lambda b, h, c: (b * nT + c, (col0 + j * H) // HB + h))
    y, S = pl.pallas_call(
        functools.partial(_hg_prompt_body, HEAD ** -0.5, HB),
        grid=(B, H // HB, nT),
        in_specs=[col(0), col(1), col(2), col(3),
                  pl.BlockSpec((None, 1, wb), lambda b, h, c: (layer, 0, h)),
                  pl.BlockSpec((None, 1, HEAD), lambda b, h, c: (layer, 0, 0))],
        out_specs=[pl.BlockSpec((tc, wb), lambda b, h, c: (b * nT + c, h)),
                   pl.BlockSpec((None, HB, HEAD, HEAD), lambda b, h, c: (b, h, 0, 0))],
        out_shape=[jax.ShapeDtypeStruct((M, H * HEAD), bf16), jax.ShapeDtypeStruct((B, H, HEAD, HEAD), f32)],
        scratch_shapes=[pltpu.VMEM((HB, HEAD, HEAD), f32)],
        compiler_params=_params("arbitrary", "arbitrary", "arbitrary"), name="hg_prompt")(
            proj, proj, proj, proj, lb, norm_w)
    return y, S


SB = 16


def _state_step(s_ref, so_ref, o_scr, d_rows, kT, qT, vnew_fn):
    for j in range(SB):
        S = s_ref[j]
        kcol = kT[:, j:j + 1]
        d, vnew = vnew_fn(j, S, kcol)
        Sn = d * S + kcol * vnew
        so_ref[j] = Sn
        o_scr[j:j + 1, :] = jnp.sum(qT[:, j:j + 1] * Sn, axis=0, keepdims=True)


def _hg_sample_body(scale, q_ref, f_ref, i_ref, g_ref, lb_ref, nw_ref, s_ref, *rest):
    y_ref, so_ref, o_scr = rest[-3:]
    f, k = _hg_gates(f_ref[...], lb_ref[...])
    q = q_ref[...] * scale
    v = i_ref[...]
    fT, kT, qT = f.T, k.T, q.T

    def vnew(j, S, kcol):
        return fT[:, j:j + 1], v[j:j + 1, :]

    _state_step(s_ref, so_ref, o_scr, None, kT, qT, vnew)
    y_ref[...] = _gated_rms(o_scr[...], nw_ref[...], g_ref[...]).astype(y_ref.dtype)


def _state_out(state, stacked_prev, n_in):
    extra_in, extra_specs, aliases = [], [], {}
    if stacked_prev is not None:
        extra_in, extra_specs, aliases = [stacked_prev], [pl.BlockSpec(memory_space=pl.ANY)], {n_in: 1}
    return jax.ShapeDtypeStruct(state.shape, f32), extra_in, extra_specs, aliases


def _hg_sample(proj, y_all, row0, DB, H, col0, lb, norm_w, state, stacked_prev, layer):
    assert row0 % SB == 0 and DB % SB == 0
    rb = row0 // SB
    col = lambda j: pl.BlockSpec((SB, HEAD), lambda h, b: (rb + b, col0 + j * H + h))
    st_spec = pl.BlockSpec((None, SB, None, HEAD, HEAD), lambda h, b: (layer, b, h, 0, 0))
    s_shape, extra_in, extra_specs, aliases = _state_out(state, stacked_prev, 8)
    y, S = pl.pallas_call(
        functools.partial(_hg_sample_body, HEAD ** -0.5),
        grid=(H, DB // SB),
        in_specs=[col(0), col(1), col(2), col(3),
                  pl.BlockSpec((None, 1, HEAD), lambda h, b: (layer, 0, h)),
                  pl.BlockSpec((None, 1, HEAD), lambda h, b: (layer, 0, 0)),
                  st_spec,
                  pl.BlockSpec(memory_space=pl.ANY)] + extra_specs,
        out_specs=[pl.BlockSpec((SB, HEAD), lambda h, b: (rb + b, h)), st_spec],
        out_shape=[jax.ShapeDtypeStruct(y_all.shape, y_all.dtype), s_shape],
        scratch_shapes=[pltpu.VMEM((SB, HEAD), f32)],
        input_output_aliases={7: 0, **aliases},
        compiler_params=_params("arbitrary", "arbitrary"), name="hg_sample")(
            proj, proj, proj, proj, lb, norm_w, state, y_all, *extra_in)
    return y, S


def _pick_lane(x, idx):
    lane = lax.broadcasted_iota(jnp.int32, x.shape, 1)
    col = jnp.sum(jnp.where(lane == idx, x, 0.0), axis=1, keepdims=True)
    return jnp.broadcast_to(col, (x.shape[0], HEAD))


def _gd_gate_body(H, PT, u_ref, w_ref, alog_ref, dtb_ref, o_ref, wb):
    i = pl.program_id(0)
    tm = u_ref.shape[0]

    @pl.when(i == 0)
    def _():
        wb[...] = w_ref[...].T.astype(bf16)

    x = jnp.dot(u_ref[...], wb[...], preferred_element_type=f32)
    lane = lax.broadcasted_iota(jnp.int32, x.shape, 1)
    row = lax.broadcasted_iota(jnp.int32, x.shape, 0) + i * tm
    g = jnp.where(lane < H, -jnp.exp(alog_ref[...]) * _softplus(x + dtb_ref[...]), 0.0)
    G = _mm_exact_lhs(_chunk_tril(tm), g)
    G = jnp.where(row < PT, G, g)
    o_ref[...] = jnp.where(lane < H, G, jax.nn.sigmoid(x))


def _gd_gate_proj(u, w_in_t, row_block, alog_pad, dtb_pad, PT, H, layer):
    M, D = u.shape
    tm = _pick(M, (640, 320, 128, 64))
    assert PT % CHUNK == 0 and tm % CHUNK == 0
    vec = pl.BlockSpec((None, 1, LANE), lambda i: (layer, 0, 0))
    return pl.pallas_call(
        functools.partial(_gd_gate_body, H, PT),
        grid=(M // tm,),
        in_specs=[pl.BlockSpec((tm, D), lambda i: (i, 0)),
                  pl.BlockSpec((None, LANE, D), lambda i: (layer, row_block, 0)), vec, vec],
        out_specs=pl.BlockSpec((tm, LANE), lambda i: (i, 0)),
        out_shape=jax.ShapeDtypeStruct((M, LANE), f32),
        scratch_shapes=[pltpu.VMEM((D, LANE), bf16)],
        compiler_params=_params("arbitrary"), name="gd_gate_proj")(u, w_in_t, alog_pad, dtb_pad)


def _unit_lower_inverses(Ns):
    r = lax.broadcasted_iota(jnp.int32, (CHUNK, CHUNK), 0)
    c = lax.broadcasted_iota(jnp.int32, (CHUNK, CHUNK), 1)
    same = lambda n: (r // n) == (c // n)
    assert CHUNK == 4 * SUB
    dot = functools.partial(jnp.dot, preferred_element_type=f32)
    cast = lambda xs: [x.astype(bf16) for x in xs]
    Rs = [jnp.where(same(SUB), N, 0.0) for N in Ns]
    Rb = cast(Rs)
    Ps = [dot(rb, rb) for rb in Rb]
    p = 2
    while p < SUB:
        Pb = cast(Ps)
        if 2 * p < SUB:
            PMs = [dot(pb, jnp.concatenate([rb, pb], axis=1)) for rb, pb in zip(Rb, Pb)]
            Rs = [R + P + PM[:, :CHUNK] for R, P, PM in zip(Rs, Ps, PMs)]
            Ps = [PM[:, CHUNK:] for PM in PMs]
            Rb = cast(Rs)
        else:
            PRs = [dot(pb, rb) for rb, pb in zip(Rb, Pb)]
            Rs = [R + P + PR for R, P, PR in zip(Rs, Ps, PRs)]
        p *= 2
    eye = jnp.where(r == c, 1.0, 0.0)
    Nb = cast(Ns)
    for n in (2 * SUB, 4 * SUB):
        off = jnp.logical_and(same(n), jnp.logical_not(same(n // 2)))
        Db = cast([eye + R for R in Rs])
        DCs = [dot(db, jnp.where(off, nb, jnp.zeros_like(nb))) for db, nb in zip(Db, Nb)]
        DCDs = [dot(dc, db) for dc, db in zip(cast(DCs), Db)]
        Rs = [R + DCD for R, DCD in zip(Rs, DCDs)]
    return Rs


def _gd_prompt_body(scale, H, HB, q_ref, k_ref, v_ref, z_ref, gt_ref, cwq_ref, cwk_ref, cwv_ref, nw_ref,
                    y_ref, s_ref, xq, xk, xv, S_scr):
    hb = pl.program_id(1)
    c = pl.program_id(2)
    tc = q_ref.shape[0]
    nchunk = tc // CHUNK

    @pl.when(c == 0)
    def _():
        for xb in (xq, xk, xv):
            xb[0:8, :] = jnp.zeros((8, xb.shape[1]), f32)
        S_scr[...] = jnp.zeros_like(S_scr)

    def conv_silu(x_ref, xb, cw_ref):
        xb[8:8 + tc, :] = x_ref[...]
        y = _causal_conv4(xb, cw_ref, tc)
        xb[0:8, :] = xb[tc:tc + 8, :]
        return _silu(y)

    qc = conv_silu(q_ref, xq, cwq_ref)
    kc = conv_silu(k_ref, xk, cwk_ref)
    vc = conv_silu(v_ref, xv, cwv_ref)
    gt = gt_ref[...]
    nw = nw_ref[...]
    tril_b = _tril_mask(CHUNK)
    strict_b = _tril_mask(CHUNK, strict=True)

    work = []
    for hh in range(HB):
        ls = slice(hh * HEAD, (hh + 1) * HEAD)
        h = hb * HB + hh
        q_all = _l2norm(qc[:, ls]) * scale
        k_all = _l2norm(kc[:, ls])
        G_all = _pick_lane(gt, h)
        beta = _pick_lane(gt, H + h)
        eG = jnp.exp(G_all)
        kb_all = k_all * beta
        rhs_all = jnp.concatenate([vc[:, ls] * beta, kb_all * eG], axis=1)
        qe_all = q_all * eG
        for ci in range(nchunk):
            rows = slice(ci * CHUNK, (ci + 1) * CHUNK)
            work.append(dict(hh=hh, ls=ls, rows=rows, G=G_all[rows], q=q_all[rows], k=k_all[rows], kb=kb_all[rows],
                             X=rhs_all[rows], qe=qe_all[rows]))
    for w in work:
        G = w["G"]
        w["decay"] = jnp.where(tril_b, jnp.exp(G[:, :CHUNK] - G.T[:CHUNK, :]), 0.0)
        w["KQ"] = _mm(jnp.concatenate([w["kb"], w["q"]], axis=0), w["k"], _NT)
    Rs = _unit_lower_inverses([jnp.where(strict_b, -(w["KQ"][:CHUNK] * w["decay"]), 0.0) for w in work])
    for w, R in zip(work, Rs):
        w["R"] = R
        w["qk"] = jnp.where(tril_b, w["KQ"][CHUNK:] * w["decay"], 0.0)
    for w in work:
        X = w["X"]
        w["X"] = X + _mm(w["R"], X)
        G = w["G"]
        g_last = G[CHUNK - 1:CHUNK, :]
        w["egl"] = jnp.exp(g_last)
        w["rhs2"] = jnp.concatenate([w["qk"], (w["k"] * jnp.exp(g_last - G)).T], axis=0)
    S = [S_scr[hh] for hh in range(HB)]
    for ci in range(nchunk):
        for hh in range(HB):
            w = work[hh * nchunk + ci]
            rows, ls = w["rows"], w["ls"]
            WS = _mm(jnp.concatenate([w["X"][:, HEAD:], w["qe"]], axis=0), S[hh])
            v_new = w["X"][:, :HEAD] - WS[:CHUNK]
            OS = _mm(w["rhs2"], v_new)
            S[hh] = w["egl"] * S[hh] + OS[CHUNK:]
            y_ref[rows, ls] = _gated_rms(WS[CHUNK:] + OS[:CHUNK], nw, z_ref[rows, ls]).astype(y_ref.dtype)
    for hh in range(HB):
        S_scr[hh] = S[hh]

    @pl.when(c == pl.num_programs(2) - 1)
    def _():
        for hh in range(HB):
            s_ref[hh] = S[hh]


def _gd_prompt(proj, gates, B, T, M, H, col0, lp, layer):
    tc = _pick(T, (256, 128, 64))
    nT = T // tc
    HB = HEADS_PER_STEP
    assert H % HB == 0 and col0 % HB == 0
    wb = HB * HEAD
    col = lambda j: pl.BlockSpec((tc, wb), lambda b, h, c: (b * nT + c, (col0 + j * H) // HB + h))
    cw = lambda j: pl.BlockSpec((None, CONV_W, wb), lambda b, h, c: (layer, 0, j * H // HB + h))
    buf = pltpu.VMEM((8 + tc, wb), f32)
    y, S = pl.pallas_call(
        functools.partial(_gd_prompt_body, HEAD ** -0.5, H, HB),
        grid=(B, H // HB, nT),
        in_specs=[col(0), col(1), col(2), col(3),
                  pl.BlockSpec((tc, LANE), lambda b, h, c: (b * nT + c, 0)),
                  cw(0), cw(1), cw(2),
                  pl.BlockSpec((None, 1, HEAD), lambda b, h, c: (layer, 0, 0))],
        out_specs=[pl.BlockSpec((tc, wb), lambda b, h, c: (b * nT + c, h)),
                   pl.BlockSpec((None, HB, HEAD, HEAD), lambda b, h, c: (b, h, 0, 0))],
        out_shape=[jax.ShapeDtypeStruct((M, H * HEAD), bf16), jax.ShapeDtypeStruct((B, H, HEAD, HEAD), f32)],
        scratch_shapes=[buf, buf, buf, pltpu.VMEM((HB, HEAD, HEAD), f32)],
        compiler_params=_params("arbitrary", "arbitrary", "arbitrary"), name="gd_prompt")(
            proj, proj, proj, proj, gates, lp["gd_conv_w"], lp["gd_conv_w"], lp["gd_conv_w"], lp["gd_norm_w"])
    return y, S


def _gd_sample_body(scale, H, q_ref, k_ref, v_ref, z_ref, gt_ref, csq_ref, csk_ref, csv_ref, cwq_ref, cwk_ref, cwv_ref,
                    nw_ref, s_ref, *rest):
    y_ref, so_ref, o_scr = rest[-3:]
    h = pl.program_id(0)

    def conv_silu(x_ref, cs_ref, cw_ref):
        y = cw_ref[CONV_W - 1:CONV_W, :] * x_ref[...]
        for j in range(CONV_W - 1):
            y = y + cw_ref[j:j + 1, :] * cs_ref[j]
        return _silu(y)

    q = _l2norm(conv_silu(q_ref, csq_ref, cwq_ref)) * scale
    k = _l2norm(conv_silu(k_ref, csk_ref, cwk_ref))
    v = conv_silu(v_ref, csv_ref, cwv_ref)
    gt = gt_ref[...]
    eg = jnp.exp(_pick_lane(gt, h))
    beta = _pick_lane(gt, H + h)
    kT, qT = k.T, q.T

    def vnew(j, S, kcol):
        egj = eg[j:j + 1, :]
        kS = jnp.sum(kcol * S, axis=0, keepdims=True)
        return egj, beta[j:j + 1, :] * (v[j:j + 1, :] - egj * kS)

    _state_step(s_ref, so_ref, o_scr, None, kT, qT, vnew)
    y_ref[...] = _gated_rms(o_scr[...], nw_ref[...], z_ref[...]).astype(y_ref.dtype)


def _gd_sample(proj, gates, y_all, row0, DB, H, col0, conv_state_t, state, stacked_prev, lp, layer):
    assert row0 % SB == 0 and DB % SB == 0
    rb = row0 // SB
    col = lambda j: pl.BlockSpec((SB, HEAD), lambda h, b: (rb + b, col0 + j * H + h))
    cs = lambda j: pl.BlockSpec((None, CONV_W - 1, SB, HEAD), lambda h, b: (layer, 0, b, j * H + h))
    cw = lambda j: pl.BlockSpec((None, CONV_W, HEAD), lambda h, b: (layer, 0, j * H + h))
    st_spec = pl.BlockSpec((None, SB, None, HEAD, HEAD), lambda h, b: (layer, b, h, 0, 0))
    s_shape, extra_in, extra_specs, aliases = _state_out(state, stacked_prev, 14)
    y, S = pl.pallas_call(
        functools.partial(_gd_sample_body, HEAD ** -0.5, H),
        grid=(H, DB // SB),
        in_specs=[col(0), col(1), col(2), col(3),
                  pl.BlockSpec((SB, LANE), lambda h, b: (rb + b, 0)),
                  cs(0), cs(1), cs(2), cw(0), cw(1), cw(2),
                  pl.BlockSpec((None, 1, HEAD), lambda h, b: (layer, 0, 0)),
                  st_spec,
                  pl.BlockSpec(memory_space=pl.ANY)] + extra_specs,
        out_specs=[pl.BlockSpec((SB, HEAD), lambda h, b: (rb + b, h)), st_spec],
        out_shape=[jax.ShapeDtypeStruct(y_all.shape, y_all.dtype), s_shape],
        scratch_shapes=[pltpu.VMEM((SB, HEAD), f32)],
        input_output_aliases={13: 0, **aliases},
        compiler_params=_params("arbitrary", "arbitrary"), name="gd_sample")(
            proj, proj, proj, proj, gates, conv_state_t, conv_state_t, conv_state_t,
            lp["gd_conv_w"], lp["gd_conv_w"], lp["gd_conv_w"], lp["gd_norm_w"],
            state, y_all, *extra_in)
    return y, S


def kernel(x_prompt, x_sample, state_rg_h, state_rg_conv, state_hg_S, state_gd_S, state_gd_conv, norm_mix_w, norm_mlp_w, norm_final_w, w_in, rg_conv_w, rg_conv_b, rg_wa, rg_ba, rg_wx, rg_bx, rg_a_param, hg_lb_logits, hg_norm_w, gd_conv_w, gd_A_log, gd_dt_bias, gd_norm_w, w_br_rg, w_br_hg, w_br_gd, w_out, w_up, w_down):
    B, T, D = x_prompt.shape
    DB, DT, _ = x_sample.shape
    assert DT == 1
    depth = w_in.shape[0]
    RW = rg_ba.shape[-1]
    H = gd_A_log.shape[-1]
    PT = B * T
    M = PT + DB
    assert RW % HEAD == 0 and hg_norm_w.shape[-1] == HEAD and gd_norm_w.shape[-1] == HEAD
    HW = H * HEAD
    n_main = 2 * RW + 8 * HW
    merge_col0 = n_main + 2 * H
    assert w_in.shape[-1] == merge_col0 + 3 * D and n_main % 1024 == 0 and n_main % LANE == 0
    hg_col0 = 2 * RW // HEAD
    gd_col0 = hg_col0 + 4 * H

    row3 = lambda a: a.reshape(depth, 1, a.shape[-1])
    lane_pad = lambda a: row3(jnp.pad(a.astype(f32), ((0, 0), (0, LANE - a.shape[-1]))))
    lp = dict(rg_conv_w=rg_conv_w, rg_conv_b=row3(rg_conv_b), rg_wa=rg_wa, rg_wx=rg_wx, rg_ba=row3(rg_ba),
              rg_bx=row3(rg_bx), rg_a_param=row3(rg_a_param), gd_conv_w=gd_conv_w, gd_norm_w=row3(gd_norm_w))
    alog_pad, dtb_pad = lane_pad(gd_A_log), lane_pad(gd_dt_bias)
    hg_nw = row3(hg_norm_w)
    lb = row3(_lower_bounds(hg_lb_logits.astype(f32)))
    rg_conv_t = jnp.swapaxes(state_rg_conv, 1, 2)
    gd_conv_t = jnp.swapaxes(state_gd_conv, 1, 2)

    w_in_t = jnp.swapaxes(w_in, 1, 2)
    x = jnp.concatenate([x_prompt.reshape(PT, D), x_sample.reshape(DB, D)], axis=0)
    _, u = _add_norm(x, None, norm_mix_w[0:1], bf16, emit_x=False)

    p_states, s_states = [], []
    y_final = None
    s_hgS = s_gdS = None
    for l in range(depth):
        proj = _gemm_wres(u, w_in_t, l, 0, n_main, 1024, name="in_proj", w_is_nk=True, tm_cands=BIG_TM)
        gates = _gd_gate_proj(u, w_in_t, n_main // LANE, alog_pad, dtb_pad, PT, H, l)

        y_rg, p_h = _rg_prompt(proj, B, T, M, lp, l)
        y_rg, s_h = _rg_sample(proj, y_rg, PT, DB, rg_conv_t, state_rg_h, lp, l)
        y_hg, p_hgS = _hg_prompt(proj, B, T, M, H, hg_col0, lb, hg_nw, l)
        y_hg, s_hgS = _hg_sample(proj, y_hg, PT, DB, H, hg_col0, lb, hg_nw, state_hg_S, s_hgS, l)
        y_gd, p_gdS = _gd_prompt(proj, gates, B, T, M, H, gd_col0, lp, l)
        y_gd, s_gdS = _gd_sample(proj, gates, y_gd, PT, DB, H, gd_col0, gd_conv_t, state_gd_S, s_gdS, lp, l)

        gq = gd_col0 * HEAD
        tail = lambda c0, w: jnp.stack([lax.slice(proj, (b * T + T - (CONV_W - 1), c0), (b * T + T, c0 + w))
                                        for b in range(B)], axis=0)
        last = lambda c0, w: lax.slice(proj, (PT, c0), (M, c0 + w))[:, None, :]
        p_states.append((p_h, tail(0, RW), p_hgS, p_gdS, tail(gq, 3 * HW)))
        s_states.append((s_h,
                         jnp.concatenate([state_rg_conv[l][:, 1:], last(0, RW)], axis=1),
                         None, None,
                         jnp.concatenate([state_gd_conv[l][:, 1:], last(gq, 3 * HW)], axis=1)))

        mixed = _mix(u, y_rg, y_hg, y_gd, w_in_t, w_br_rg, w_br_hg, w_br_gd, l, merge_col0)
        x, hmid = _proj_add_norm(mixed, w_out, l, x, norm_mlp_w[l:l + 1])
        hh = _gemm_wres(hmid, w_up, l, 0, w_up.shape[-1], 1024, epi=lambda a: jnp.square(jnp.maximum(a, 0.0)),
                        out_dtype=bf16, name="mlp_up", tm_cands=BIG_TM)
        d2 = _gemm_wres(hh, w_down, l, 0, D, 512, name="mlp_down", single_buffer_w=True)
        if l + 1 < depth:
            x, u = _add_norm(x, d2, norm_mix_w[l + 1:l + 2], bf16, emit_x=True)
        else:
            _, y_final = _add_norm(x, d2, norm_final_w.reshape(1, D), f32, emit_x=False)

    def stack(sts, j, like):
        return jnp.stack([s[j] for s in sts], axis=0).astype(like.dtype)

    return (y_final[:PT].reshape(B, T, D), y_final[PT:].reshape(DB, DT, D),
            stack(p_states, 0, state_rg_h), stack(p_states, 1, state_rg_conv), stack(p_states, 2, state_hg_S),
            stack(p_states, 3, state_gd_S), stack(p_states, 4, state_gd_conv),
            stack(s_states, 0, state_rg_h), stack(s_states, 1, state_rg_conv), s_hgS.astype(state_hg_S.dtype),
            s_gdS.astype(state_gd_S.dtype), stack(s_states, 4, state_gd_conv))
```

```python
import functools

import jax
import jax.numpy as jnp
from jax import lax
from jax.experimental import pallas as pl
from jax.experimental.pallas import tpu as pltpu

f32 = jnp.float32
bf16 = jnp.bfloat16

EPS = 1e-6
RG_C = 8.0
HEAD = 128
LANE = 128
SUBLANES = 8
CHUNK = 64
SUB = 16
HEADS_PER_STEP = 8
CONV_W = 4
VMEM_LIMIT = 56 * 1024 * 1024

_NT = (((1,), (1,)), ((), ()))
_TN = (((0,), (0,)), ((), ()))


def _params(*sem):
    return pltpu.CompilerParams(dimension_semantics=sem, vmem_limit_bytes=VMEM_LIMIT)


def _pick(n, cands):
    for c in cands:
        if n % c == 0:
            return c
    raise ValueError(f"no tile for {n} among {cands}")


def _mm(a, b, dims=None):
    a = a.astype(bf16)
    b = b.astype(bf16)
    if dims is None:
        return jnp.dot(a, b, preferred_element_type=f32)
    return lax.dot_general(a, b, dims, preferred_element_type=f32)


def _split3(x):
    hi = x.astype(bf16)
    r = x - hi.astype(f32)
    mid = r.astype(bf16)
    lo = (r - mid.astype(f32)).astype(bf16)
    return hi, mid, lo


def _mm_exact_lhs(a_bf16, x):
    hi, mid, lo = _split3(x)
    return (jnp.dot(a_bf16, hi, preferred_element_type=f32) + jnp.dot(a_bf16, mid, preferred_element_type=f32)
            + jnp.dot(a_bf16, lo, preferred_element_type=f32))


def _mm_hi(a, b):
    ah, am, _ = _split3(a)
    bh, bm, _ = _split3(b)
    d = functools.partial(jnp.dot, preferred_element_type=f32)
    return d(ah, bh) + (d(ah, bm) + d(am, bh))


def _expm1_neg(x, ex):
    return -jnp.tanh(0.5 * x) * (ex + 1.0)


def _softplus(x):
    return jnp.maximum(x, 0.0) + jnp.log1p(jnp.exp(-jnp.abs(x)))


def _silu(x):
    return x * jax.nn.sigmoid(x)


def _gated_rms(o, w, z):
    o = o * lax.rsqrt(jnp.mean(o * o, axis=-1, keepdims=True) + EPS) * w
    return o * _silu(z)


def _l2norm(x):
    return x * lax.rsqrt(jnp.sum(x * x, axis=-1, keepdims=True) + EPS)


def _causal_conv4(xbuf, cw_ref, tc, ls=slice(None)):
    assert CONV_W == 4
    w0, w1, w2, w3 = (cw_ref[j:j + 1, ls] for j in range(CONV_W))
    x0 = xbuf[8:8 + tc, ls]
    xm2 = xbuf[pl.ds(6, tc), ls]
    even = w3 * x0 + w1 * xm2
    odd = w2 * x0 + w0 * xm2
    odd_before = w2 * xbuf[7:8, ls] + w0 * xbuf[5:6, ls]
    row = lax.broadcasted_iota(jnp.int32, x0.shape, 0)
    return even + jnp.where(row == 0, odd_before, pltpu.roll(odd, 1, axis=0))


def _tril_mask(n, strict=False):
    r = lax.broadcasted_iota(jnp.int32, (n, n), 0)
    c = lax.broadcasted_iota(jnp.int32, (n, n), 1)
    return (r > c) if strict else (r >= c)


def _chunk_tril(n):
    r = lax.broadcasted_iota(jnp.int32, (n, n), 0)
    c = lax.broadcasted_iota(jnp.int32, (n, n), 1)
    same = (r // CHUNK) == (c // CHUNK)
    return jnp.where(jnp.logical_and(r >= c, same), 1.0, 0.0).astype(bf16)


def _norm_body(x_ref, w_ref, n_ref):
    x = x_ref[...]
    y = x * lax.rsqrt(jnp.mean(x * x, axis=-1, keepdims=True) + EPS)
    n_ref[...] = (y * w_ref[...]).astype(n_ref.dtype)


def _rms_norm(x, w_row, out_dtype):
    M, D = x.shape
    tm = _pick(M, (416, 320, 256, 128, 64, 16))
    row = pl.BlockSpec((tm, D), lambda m: (m, 0))
    return pl.pallas_call(
        _norm_body, grid=(M // tm,), in_specs=[row, pl.BlockSpec((1, D), lambda m: (0, 0))], out_specs=row,
        out_shape=jax.ShapeDtypeStruct((M, D), out_dtype),
        compiler_params=_params("arbitrary"), name="rms_norm")(x, w_row)


def _gemm_wres_body(epi, w_is_nk, has_res, a_ref, w_ref, *rest):
    res_ref = rest[0] if has_res else None
    o_ref, wb = rest[-2:]

    @pl.when(pl.program_id(1) == 0)
    def _():
        w = w_ref[...]
        wb[...] = (w.T if w_is_nk else w).astype(bf16)
    acc = jnp.dot(a_ref[...], wb[...], preferred_element_type=f32)
    if epi is not None:
        acc = epi(acc)
    if has_res:
        acc = res_ref[...] + acc
    o_ref[...] = acc.astype(o_ref.dtype)


BIG_TM = (1040, 640, 512, 320, 256, 128, 64, 16)


def _gemm_wres(a, w, layer, col_block0, n_out, tn, epi=None, out_dtype=f32, name="gemm", w_is_nk=False,
               tm_cands=(640, 512, 320, 256, 128, 64, 16), single_buffer_w=False, res=None):
    M, K = a.shape
    tm = _pick(M, tm_cands)
    mode = dict(pipeline_mode=pl.Buffered(1)) if single_buffer_w else {}
    if w_is_nk:
        w_spec = pl.BlockSpec((None, tn, K), lambda n, m: (layer, n + col_block0, 0), **mode)
    else:
        w_spec = pl.BlockSpec((None, K, tn), lambda n, m: (layer, 0, n + col_block0), **mode)
    tile = pl.BlockSpec((tm, tn), lambda n, m: (m, n))
    return pl.pallas_call(
        functools.partial(_gemm_wres_body, epi, w_is_nk, res is not None),
        grid=(n_out // tn, M // tm),
        in_specs=[pl.BlockSpec((tm, K), lambda n, m: (m, 0)), w_spec] + ([tile] if res is not None else []),
        out_specs=tile,
        out_shape=jax.ShapeDtypeStruct((M, n_out), out_dtype),
        scratch_shapes=[pltpu.VMEM((K, tn), bf16)],
        compiler_params=_params("arbitrary", "arbitrary"), name=name)(a, w, *([res] if res is not None else []))


def _proj_norm_body(a_ref, w_ref, x_ref, nw_ref, xo_ref, n_ref, wb):
    @pl.when(pl.program_id(0) == 0)
    def _():
        wb[...] = w_ref[...].astype(bf16)
    x = x_ref[...] + jnp.dot(a_ref[...], wb[...], preferred_element_type=f32)
    xo_ref[...] = x
    y = x * lax.rsqrt(jnp.mean(x * x, axis=-1, keepdims=True) + EPS)
    n_ref[...] = (y * nw_ref[...]).astype(n_ref.dtype)


def _proj_add_norm(a, w, layer, x, nw_row):
    M, K = a.shape
    D = w.shape[-1]
    tm = _pick(M, (320, 256, 128, 64, 16))
    row = lambda width: pl.BlockSpec((tm, width), lambda m: (m, 0))
    return pl.pallas_call(
        _proj_norm_body,
        grid=(M // tm,),
        in_specs=[row(K), pl.BlockSpec((None, K, D), lambda m: (layer, 0, 0), pipeline_mode=pl.Buffered(1)),
                  row(D), pl.BlockSpec((1, D), lambda m: (0, 0))],
        out_specs=[row(D), row(D)],
        out_shape=[jax.ShapeDtypeStruct((M, D), f32), jax.ShapeDtypeStruct((M, D), bf16)],
        scratch_shapes=[pltpu.VMEM((K, D), bf16)],
        compiler_params=_params("arbitrary"), name="out_proj_norm")(a, w, x, nw_row)


MERGE_SHIFT = 16


def _mix_body(u_ref, yr_ref, yh_ref, yg_ref, wm0, wm1, wm2, wx0, wx1, wx2, wr_ref, wh_ref, wg_ref, o_ref, wmb, wbb):
    tn = o_ref.shape[1]

    @pl.when(pl.program_id(1) == 0)
    def _():
        for b, (wm, wx) in enumerate(((wm0, wx0), (wm1, wx1), (wm2, wx2))):
            wcat = jnp.concatenate([wm[...], wx[...]], axis=0)
            wmb[b] = wcat[MERGE_SHIFT:MERGE_SHIFT + tn].T.astype(bf16)
        for b, wr in enumerate((wr_ref, wh_ref, wg_ref)):
            wbb[b] = wr[...].astype(bf16)

    u = u_ref[...]
    acc = None
    for b, y_ref in enumerate((yr_ref, yh_ref, yg_ref)):
        gate = jax.nn.sigmoid(jnp.dot(u, wmb[b], preferred_element_type=f32))
        p = jnp.dot(y_ref[...], wbb[b], preferred_element_type=f32)
        acc = gate * p if acc is None else acc + gate * p
    o_ref[...] = acc.astype(o_ref.dtype)


def _mix(u, y_rg, y_hg, y_gd, w_in_t, w_br_rg, w_br_hg, w_br_gd, layer, merge_col0):
    M, D = u.shape
    W = y_rg.shape[1]
    tn = 256
    tm = _pick(M, (640, 416, 320, 256, 128, 64, 16))
    nt = D // tn
    assert (merge_col0 - MERGE_SHIFT) % tn == 0 and tn % MERGE_SHIFT == 0
    base = (merge_col0 - MERGE_SHIFT) // tn
    r = tn // MERGE_SHIFT

    def wm_spec(b):
        return pl.BlockSpec((None, tn, D), lambda n, m: (layer, base + b * nt + n, 0))

    def wx_spec(b):
        return pl.BlockSpec((None, MERGE_SHIFT, D), lambda n, m: (layer, (base + b * nt + n + 1) * r, 0))

    row = lambda w: pl.BlockSpec((tm, w), lambda n, m: (m, 0))
    br = pl.BlockSpec((None, W, tn), lambda n, m: (layer, 0, n))
    return pl.pallas_call(
        _mix_body,
        grid=(nt, M // tm),
        in_specs=[row(D), row(W), row(W), row(W), wm_spec(0), wm_spec(1), wm_spec(2),
                  wx_spec(0), wx_spec(1), wx_spec(2), br, br, br],
        out_specs=pl.BlockSpec((tm, tn), lambda n, m: (m, n)),
        out_shape=jax.ShapeDtypeStruct((M, D), bf16),
        scratch_shapes=[pltpu.VMEM((3, D, tn), bf16), pltpu.VMEM((3, W, tn), bf16)],
        compiler_params=_params("arbitrary", "arbitrary"), name="mix")(
            u, y_rg, y_hg, y_gd, w_in_t, w_in_t, w_in_t, w_in_t, w_in_t, w_in_t, w_br_rg, w_br_hg, w_br_gd)


def _lb_body(x_ref, o_ref):
    x = x_ref[...]
    depth = x.shape[0]
    m = jnp.max(x, axis=0, keepdims=True)
    e = jnp.exp(x - m)
    p = e / jnp.sum(e, axis=0, keepdims=True)
    acc = jnp.zeros_like(p[0:1])
    o_ref[0:1, :] = acc
    for l in range(1, depth):
        acc = acc + p[l:l + 1]
        o_ref[l:l + 1, :] = acc


def _lower_bounds(logits):
    return pl.pallas_call(_lb_body, out_shape=jax.ShapeDtypeStruct(logits.shape, f32), name="hg_lower_bounds")(logits)


def _rg_gates(xc, wa, wx, ba, bx, sp):
    xb = xc.astype(bf16)
    r = jax.nn.sigmoid(jnp.dot(xb, wa.astype(bf16), preferred_element_type=f32) + ba)
    i = jax.nn.sigmoid(jnp.dot(xb, wx.astype(bf16), preferred_element_type=f32) + bx)
    log_a = (-RG_C) * r * sp
    a = jnp.exp(log_a)
    m2 = _expm1_neg(2.0 * log_a, a * a)
    mult = jnp.where(m2 > 0.0, m2 * lax.rsqrt(m2), 0.0)
    return a, mult, i


def _rg_prompt_body(x_ref, gate_ref, cw_ref, cb_ref, wa_ref, wx_ref, ba_ref, bx_ref, ap_ref, y_ref, h_ref, xbuf, hprev):
    c = pl.program_id(1)
    tc = x_ref.shape[0]
    nblk = x_ref.shape[1] // HEAD

    @pl.when(c == 0)
    def _():
        xbuf[0:8, :] = jnp.zeros((8, xbuf.shape[1]), f32)
        hprev[...] = jnp.zeros_like(hprev)

    xbuf[8:8 + tc, :] = x_ref[...]
    row = lax.broadcasted_iota(jnp.int32, (tc, HEAD), 0)
    first = jnp.logical_and(row == 0, c == 0)
    for n in range(nblk):
        ls = slice(n * HEAD, (n + 1) * HEAD)
        xc = _causal_conv4(xbuf, cw_ref, tc, ls) + cb_ref[:, ls]
        sp = _softplus(-ap_ref[:, ls])
        a, mult, i = _rg_gates(xc, wa_ref[n], wx_ref[n], ba_ref[:, ls], bx_ref[:, ls], sp)
        mult = jnp.where(first, 1.0, mult)
        b = mult * (i * xc)
        s = 1
        while s < SUBLANES:
            keep = (row % SUBLANES) >= s
            a_sh = jnp.where(keep, pltpu.roll(a, s, axis=0), 1.0)
            b_sh = jnp.where(keep, pltpu.roll(b, s, axis=0), 0.0)
            b = a * b_sh + b
            a = a * a_sh
            s *= 2
        carry = hprev[:, ls]
        groups = []
        for g in range(tc // SUBLANES):
            rows = slice(g * SUBLANES, (g + 1) * SUBLANES)
            hg = b[rows] + a[rows] * carry
            groups.append(hg)
            carry = hg[SUBLANES - 1:SUBLANES, :]
        h = jnp.concatenate(groups, axis=0)
        hprev[:, ls] = carry
        y_ref[:, ls] = (h * jax.nn.gelu(gate_ref[:, ls], approximate=True)).astype(y_ref.dtype)
    xbuf[0:8, :] = xbuf[tc:tc + 8, :]
    h_ref[...] = hprev[...]


def _rg_prompt(proj, B, T, M, lp, layer):
    W = lp["rg_ba"].shape[-1]
    tc = _pick(T, (256, 128, 64))
    nT = T // tc
    nblk = W // HEAD
    vec = pl.BlockSpec((None, 1, W), lambda b, c: (layer, 0, 0))
    blk = pl.BlockSpec((None, nblk, HEAD, HEAD), lambda b, c: (layer, 0, 0, 0))
    y, h = pl.pallas_call(
        _rg_prompt_body,
        grid=(B, nT),
        in_specs=[pl.BlockSpec((tc, W), lambda b, c: (b * nT + c, 0)),
                  pl.BlockSpec((tc, W), lambda b, c: (b * nT + c, 1)),
                  pl.BlockSpec((None, CONV_W, W), lambda b, c: (layer, 0, 0)),
                  vec, blk, blk, vec, vec, vec],
        out_specs=[pl.BlockSpec((tc, W), lambda b, c: (b * nT + c, 0)),
                   pl.BlockSpec((None, 1, W), lambda b, c: (b, 0, 0))],
        out_shape=[jax.ShapeDtypeStruct((M, W), bf16), jax.ShapeDtypeStruct((B, 1, W), f32)],
        scratch_shapes=[pltpu.VMEM((8 + tc, W), f32), pltpu.VMEM((1, W), f32)],
        compiler_params=_params("arbitrary", "arbitrary"), name="rg_prompt")(
            proj, proj, lp["rg_conv_w"], lp["rg_conv_b"], lp["rg_wa"], lp["rg_wx"], lp["rg_ba"], lp["rg_bx"],
            lp["rg_a_param"])
    return y, h[:, 0]


def _rg_sample_body(x_ref, gate_ref, cs_ref, h0_ref, cw_ref, cb_ref, wa_ref, wx_ref, ba_ref, bx_ref, ap_ref,
                    yin_ref, y_ref, h_ref):
    del yin_ref
    nblk = x_ref.shape[1] // HEAD
    for n in range(nblk):
        ls = slice(n * HEAD, (n + 1) * HEAD)
        xc = cb_ref[:, ls] + cw_ref[CONV_W - 1:CONV_W, ls] * x_ref[:, ls]
        for j in range(CONV_W - 1):
            xc = xc + cw_ref[j:j + 1, ls] * cs_ref[j, :, ls]
        sp = _softplus(-ap_ref[:, ls])
        a, mult, i = _rg_gates(xc, wa_ref[n], wx_ref[n], ba_ref[:, ls], bx_ref[:, ls], sp)
        h = a * h0_ref[:, ls] + mult * (i * xc)
        h_ref[:, ls] = h
        y_ref[:, ls] = (h * jax.nn.gelu(gate_ref[:, ls], approximate=True)).astype(y_ref.dtype)


def _rg_sample(proj, y_all, row0, DB, conv_state_t, h0, lp, layer):
    W = lp["rg_ba"].shape[-1]
    nblk = W // HEAD
    assert row0 % DB == 0
    rb = row0 // DB
    vec = pl.BlockSpec((None, 1, W), lambda i: (layer, 0, 0))
    blk = pl.BlockSpec((None, nblk, HEAD, HEAD), lambda i: (layer, 0, 0, 0))
    y, h = pl.pallas_call(
        _rg_sample_body,
        grid=(1,),
        in_specs=[pl.BlockSpec((DB, W), lambda i: (rb, 0)),
                  pl.BlockSpec((DB, W), lambda i: (rb, 1)),
                  pl.BlockSpec((None, CONV_W - 1, DB, W), lambda i: (layer, 0, 0, 0)),
                  pl.BlockSpec((None, DB, W), lambda i: (layer, 0, 0)),
                  pl.BlockSpec((None, CONV_W, W), lambda i: (layer, 0, 0)),
                  vec, blk, blk, vec, vec, vec,
                  pl.BlockSpec(memory_space=pl.ANY)],
        out_specs=[pl.BlockSpec((DB, W), lambda i: (rb, 0)),
                   pl.BlockSpec((DB, W), lambda i: (0, 0))],
        out_shape=[jax.ShapeDtypeStruct(y_all.shape, y_all.dtype), jax.ShapeDtypeStruct((DB, W), f32)],
        input_output_aliases={11: 0},
        compiler_params=_params("arbitrary"), name="rg_sample")(
            proj, proj, conv_state_t, h0, lp["rg_conv_w"], lp["rg_conv_b"], lp["rg_wa"], lp["rg_wx"],
            lp["rg_ba"], lp["rg_bx"], lp["rg_a_param"], y_all)
    return y, h


def _hg_gates(fx, lb):
    f = lb + (1.0 - lb) * jax.nn.sigmoid(fx)
    k = (1.0 - lb) * jax.nn.sigmoid(-fx)
    return f, k


def _hg_intra_diag(G, q, k):
    lane = lax.broadcasted_iota(jnp.int32, (SUB, CHUNK), 1)
    blocks = []
    for i in range(CHUNK // SUB):
        sl = slice(i * SUB, (i + 1) * SUB)
        g_i, q_i, k_i = G[sl], q[sl], k[sl]
        a_d = jnp.zeros((SUB, CHUNK), f32)
        for s in range(SUB):
            e = jnp.exp(g_i - g_i[s:s + 1, :])
            col = jnp.sum(q_i * k_i[s:s + 1, :] * e, axis=-1, keepdims=True)
            a_d = jnp.where(lane == i * SUB + s, col, a_d)
        blocks.append(a_d)
    return jnp.where(_tril_mask(CHUNK), jnp.concatenate(blocks, axis=0), 0.0)


def _hg_intra_off(G, q, k):
    nsub = CHUNK // SUB
    row = lax.broadcasted_iota(jnp.int32, (CHUNK, HEAD), 0)
    q_parts, k_parts = [], []
    for j in range(nsub - 1):
        g_e = G[(j + 1) * SUB - 1:(j + 1) * SUB, :]
        q_parts.append(jnp.where(row >= (j + 1) * SUB, q * jnp.exp(G - g_e), 0.0))
        in_j = jnp.logical_and(row >= j * SUB, row < (j + 1) * SUB)
        k_parts.append(jnp.where(in_j, k * jnp.exp(g_e - G), 0.0))
    return _mm(jnp.concatenate(q_parts, axis=1), jnp.concatenate(k_parts, axis=1), _NT)


def _hg_prompt_body(scale, HB, q_ref, f_ref, i_ref, g_ref, lb_ref, nw_ref, y_ref, s_ref, S_scr):
    c = pl.program_id(2)
    nchunk = q_ref.shape[0] // CHUNK

    @pl.when(c == 0)
    def _():
        S_scr[...] = jnp.zeros_like(S_scr)

    nw = nw_ref[...]
    tril = _chunk_tril(nchunk * CHUNK)
    work = []
    for hh in range(HB):
        ls = slice(hh * HEAD, (hh + 1) * HEAD)
        f_all, k_all = _hg_gates(f_ref[:, ls], lb_ref[:, ls])
        q_all = q_ref[:, ls] * scale
        G_all = _mm_exact_lhs(tril, jnp.log(f_all))
        for ci in range(nchunk):
            rows = slice(ci * CHUNK, (ci + 1) * CHUNK)
            work.append(dict(hh=hh, ls=ls, rows=rows, G=G_all[rows], q=q_all[rows], k=k_all[rows]))
    for w in work:
        w["A"] = _hg_intra_diag(w["G"], w["q"], w["k"])
    for w in work:
        w["A"] = w["A"] + _hg_intra_off(w["G"], w["q"], w["k"])
    for w in work:
        G = w["G"]
        kT, GT = w["k"].T, G.T
        g_last = GT[:, CHUNK - 1:CHUNK]
        w["dec"] = jnp.exp(g_last)
        w["upd"] = _mm(kT * jnp.exp(g_last - GT), i_ref[w["rows"], w["ls"]])
        w["lhs"] = jnp.concatenate([w["A"], w["q"] * jnp.exp(G)], axis=1)
    S = [S_scr[hh] for hh in range(HB)]
    for ci in range(nchunk):
        for hh in range(HB):
            w = work[hh * nchunk + ci]
            rows, ls = w["rows"], w["ls"]
            o = _mm(w["lhs"], jnp.concatenate([i_ref[rows, ls], S[hh]], axis=0))
            S[hh] = S[hh] * w["dec"] + w["upd"]
            y_ref[rows, ls] = _gated_rms(o, nw, g_ref[rows, ls]).astype(y_ref.dtype)
    for hh in range(HB):
        S_scr[hh] = S[hh]

    @pl.when(c == pl.num_programs(2) - 1)
    def _():
        for hh in range(HB):
            s_ref[hh] = S[hh]


def _hg_prompt(proj, B, T, M, H, col0, lb, norm_w, layer):
    tc = _pick(T, (256, 128, 64))
    nT = T // tc
    HB = HEADS_PER_STEP
    assert H % HB == 0 and col0 % HB == 0
    wb = HB * HEAD
    col = lambda j: pl.BlockSpec((tc, wb), lambda b, h, c: (b * nT + c, (col0 + j * H) // HB + h))
    y, S = pl.pallas_call(
        functools.partial(_hg_prompt_body, HEAD ** -0.5, HB),
        grid=(B, H // HB, nT),
        in_specs=[col(0), col(1), col(2), col(3),
                  pl.BlockSpec((None, 1, wb), lambda b, h, c: (layer, 0, h)),
                  pl.BlockSpec((None, 1, HEAD), lambda b, h, c: (layer, 0, 0))],
        out_specs=[pl.BlockSpec((tc, wb), lambda b, h, c: (b * nT + c, h)),
                   pl.BlockSpec((None, HB, HEAD, HEAD), lambda b, h, c: (b, h, 0, 0))],
        out_shape=[jax.ShapeDtypeStruct((M, H * HEAD), bf16), jax.ShapeDtypeStruct((B, H, HEAD, HEAD), f32)],
        scratch_shapes=[pltpu.VMEM((HB, HEAD, HEAD), f32)],
        compiler_params=_params("arbitrary", "arbitrary", "arbitrary"), name="hg_prompt")(
            proj, proj, proj, proj, lb, norm_w)
    return y, S


SB = 16


def _state_step(s_ref, so_ref, o_scr, h, kT, q, vnew_fn):
    for j in range(SB):
        S = s_ref[j, h]
        kcol = kT[:, j:j + 1]
        d, vnew = vnew_fn(j, S, kcol)
        so_ref[j, h] = d * S + kcol * vnew
    qb = q.astype(bf16)
    for j in range(SB):
        o_scr[j:j + 1, :] = jnp.dot(qb, so_ref[j, h].astype(bf16), preferred_element_type=f32)[j:j + 1, :]


def _head_lanes(h):
    return pl.ds(pl.multiple_of(h * HEAD, HEAD), HEAD)


def _hg_sample_body(scale, H, q_ref, f_ref, i_ref, g_ref, lb_ref, nw_ref, s_ref, *rest):
    y_ref, so_ref, o_scr = rest[-3:]

    def head(h, carry):
        ls = _head_lanes(h)
        f, k = _hg_gates(f_ref[:, ls], lb_ref[:, ls])
        q = q_ref[:, ls] * scale
        v = i_ref[:, ls]
        fT, kT = f.T, k.T

        def vnew(j, S, kcol):
            return fT[:, j:j + 1], v[j:j + 1, :]

        _state_step(s_ref, so_ref, o_scr, h, kT, q, vnew)
        y_ref[:, ls] = _gated_rms(o_scr[...], nw_ref[...], g_ref[:, ls]).astype(y_ref.dtype)
        return carry

    lax.fori_loop(0, H, head, 0)


def _state_out(state, stacked_prev, n_in):
    extra_in, extra_specs, aliases = [], [], {}
    if stacked_prev is not None:
        extra_in, extra_specs, aliases = [stacked_prev], [pl.BlockSpec(memory_space=pl.ANY)], {n_in: 1}
    return jax.ShapeDtypeStruct(state.shape, f32), extra_in, extra_specs, aliases


def _hg_sample(proj, y_all, row0, DB, H, col0, lb, norm_w, state, stacked_prev, layer):
    assert row0 % SB == 0 and DB % SB == 0 and col0 % H == 0
    rb = row0 // SB
    wh = H * HEAD
    col = lambda j: pl.BlockSpec((SB, wh), lambda b: (rb + b, col0 // H + j))
    st_spec = pl.BlockSpec((None, SB, H, HEAD, HEAD), lambda b: (layer, b, 0, 0, 0))
    s_shape, extra_in, extra_specs, aliases = _state_out(state, stacked_prev, 8)
    y, S = pl.pallas_call(
        functools.partial(_hg_sample_body, HEAD ** -0.5, H),
        grid=(DB // SB,),
        in_specs=[col(0), col(1), col(2), col(3),
                  pl.BlockSpec((None, 1, wh), lambda b: (layer, 0, 0)),
                  pl.BlockSpec((None, 1, HEAD), lambda b: (layer, 0, 0)),
                  st_spec,
                  pl.BlockSpec(memory_space=pl.ANY)] + extra_specs,
        out_specs=[pl.BlockSpec((SB, wh), lambda b: (rb + b, 0)), st_spec],
        out_shape=[jax.ShapeDtypeStruct(y_all.shape, y_all.dtype), s_shape],
        scratch_shapes=[pltpu.VMEM((SB, HEAD), f32)],
        input_output_aliases={7: 0, **aliases},
        compiler_params=_params("arbitrary"), name="hg_sample")(
            proj, proj, proj, proj, lb, norm_w, state, y_all, *extra_in)
    return y, S


def _pick_lane(x, idx):
    lane = lax.broadcasted_iota(jnp.int32, x.shape, 1)
    col = jnp.sum(jnp.where(lane == idx, x, 0.0), axis=1, keepdims=True)
    return jnp.broadcast_to(col, (x.shape[0], HEAD))


def _gd_gate_body(H, PT, u_ref, w_ref, alog_ref, dtb_ref, o_ref, wb):
    i = pl.program_id(0)
    tm = u_ref.shape[0]

    @pl.when(i == 0)
    def _():
        wb[...] = w_ref[...].T.astype(bf16)

    x = jnp.dot(u_ref[...], wb[...], preferred_element_type=f32)
    lane = lax.broadcasted_iota(jnp.int32, x.shape, 1)
    row = lax.broadcasted_iota(jnp.int32, x.shape, 0) + i * tm
    g = jnp.where(lane < H, -jnp.exp(alog_ref[...]) * _softplus(x + dtb_ref[...]), 0.0)
    G = _mm_exact_lhs(_chunk_tril(tm), g)
    G = jnp.where(row < PT, G, g)
    o_ref[...] = jnp.where(lane < H, G, jax.nn.sigmoid(x))


def _gd_gate_proj(u, w_in_t, row_block, alog_pad, dtb_pad, PT, H, layer):
    M, D = u.shape
    tm = _pick(M, (640, 320, 128, 64))
    assert PT % CHUNK == 0 and tm % CHUNK == 0
    vec = pl.BlockSpec((None, 1, LANE), lambda i: (layer, 0, 0))
    return pl.pallas_call(
        functools.partial(_gd_gate_body, H, PT),
        grid=(M // tm,),
        in_specs=[pl.BlockSpec((tm, D), lambda i: (i, 0)),
                  pl.BlockSpec((None, LANE, D), lambda i: (layer, row_block, 0)), vec, vec],
        out_specs=pl.BlockSpec((tm, LANE), lambda i: (i, 0)),
        out_shape=jax.ShapeDtypeStruct((M, LANE), f32),
        scratch_shapes=[pltpu.VMEM((D, LANE), bf16)],
        compiler_params=_params("arbitrary"), name="gd_gate_proj")(u, w_in_t, alog_pad, dtb_pad)


def _unit_lower_inverses(Ns):
    r = lax.broadcasted_iota(jnp.int32, (CHUNK, CHUNK), 0)
    c = lax.broadcasted_iota(jnp.int32, (CHUNK, CHUNK), 1)
    same = lambda n: (r // n) == (c // n)
    assert CHUNK == 4 * SUB
    dot = functools.partial(jnp.dot, preferred_element_type=f32)
    cast = lambda xs: [x.astype(bf16) for x in xs]
    Rs = [jnp.where(same(SUB), N, 0.0) for N in Ns]
    Rb = cast(Rs)
    Ps = [dot(rb, rb) for rb in Rb]
    p = 2
    while p < SUB:
        Pb = cast(Ps)
        if 2 * p < SUB:
            PMs = [dot(pb, jnp.concatenate([rb, pb], axis=1)) for rb, pb in zip(Rb, Pb)]
            Rs = [R + P + PM[:, :CHUNK] for R, P, PM in zip(Rs, Ps, PMs)]
            Ps = [PM[:, CHUNK:] for PM in PMs]
            Rb = cast(Rs)
        else:
            PRs = [dot(pb, rb) for rb, pb in zip(Rb, Pb)]
            Rs = [R + P + PR for R, P, PR in zip(Rs, Ps, PRs)]
        p *= 2
    eye = jnp.where(r == c, 1.0, 0.0)
    Nb = cast(Ns)
    for n in (2 * SUB, 4 * SUB):
        off = jnp.logical_and(same(n), jnp.logical_not(same(n // 2)))
        Db = cast([eye + R for R in Rs])
        DCs = [dot(db, jnp.where(off, nb, jnp.zeros_like(nb))) for db, nb in zip(Db, Nb)]
        DCDs = [dot(dc, db) for dc, db in zip(cast(DCs), Db)]
        Rs = [R + DCD for R, DCD in zip(Rs, DCDs)]
    return Rs


def _gd_prompt_body(scale, H, HB, q_ref, k_ref, v_ref, z_ref, gt_ref, cwq_ref, cwk_ref, cwv_ref, nw_ref,
                    y_ref, s_ref, xq, xk, xv, S_scr):
    hb = pl.program_id(1)
    c = pl.program_id(2)
    tc = q_ref.shape[0]
    nchunk = tc // CHUNK

    @pl.when(c == 0)
    def _():
        for xb in (xq, xk, xv):
            xb[0:8, :] = jnp.zeros((8, xb.shape[1]), f32)
        S_scr[...] = jnp.zeros_like(S_scr)

    def conv_silu(x_ref, xb, cw_ref):
        xb[8:8 + tc, :] = x_ref[...]
        y = _causal_conv4(xb, cw_ref, tc)
        xb[0:8, :] = xb[tc:tc + 8, :]
        return _silu(y)

    qc = conv_silu(q_ref, xq, cwq_ref)
    kc = conv_silu(k_ref, xk, cwk_ref)
    vc = conv_silu(v_ref, xv, cwv_ref)
    gt = gt_ref[...]
    nw = nw_ref[...]
    tril_b = _tril_mask(CHUNK)
    strict_b = _tril_mask(CHUNK, strict=True)

    work = []
    for hh in range(HB):
        ls = slice(hh * HEAD, (hh + 1) * HEAD)
        h = hb * HB + hh
        q_all = _l2norm(qc[:, ls]) * scale
        k_all = _l2norm(kc[:, ls])
        G_all = _pick_lane(gt, h)
        beta = _pick_lane(gt, H + h)
        eG = jnp.exp(G_all)
        kb_all = k_all * beta
        rhs_all = jnp.concatenate([vc[:, ls] * beta, kb_all * eG], axis=1)
        qe_all = q_all * eG
        for ci in range(nchunk):
            rows = slice(ci * CHUNK, (ci + 1) * CHUNK)
            work.append(dict(hh=hh, ls=ls, rows=rows, G=G_all[rows], q=q_all[rows], k=k_all[rows], kb=kb_all[rows],
                             X=rhs_all[rows], qe=qe_all[rows]))
    for w in work:
        G = w["G"]
        w["decay"] = jnp.where(tril_b, jnp.exp(G[:, :CHUNK] - G.T[:CHUNK, :]), 0.0)
        w["KQ"] = _mm(jnp.concatenate([w["kb"], w["q"]], axis=0), w["k"], _NT)
    Rs = _unit_lower_inverses([jnp.where(strict_b, -(w["KQ"][:CHUNK] * w["decay"]), 0.0) for w in work])
    for w, R in zip(work, Rs):
        w["R"] = R
        w["qk"] = jnp.where(tril_b, w["KQ"][CHUNK:] * w["decay"], 0.0)
    for w in work:
        X = w["X"]
        w["X"] = X + _mm(w["R"], X)
        G = w["G"]
        g_last = G[CHUNK - 1:CHUNK, :]
        w["egl"] = jnp.exp(g_last)
        w["rhs2"] = jnp.concatenate([w["qk"], (w["k"] * jnp.exp(g_last - G)).T], axis=0)
    S = [S_scr[hh] for hh in range(HB)]
    for ci in range(nchunk):
        for hh in range(HB):
            w = work[hh * nchunk + ci]
            rows, ls = w["rows"], w["ls"]
            WS = _mm(jnp.concatenate([w["X"][:, HEAD:], w["qe"]], axis=0), S[hh])
            v_new = w["X"][:, :HEAD] - WS[:CHUNK]
            OS = _mm(w["rhs2"], v_new)
            S[hh] = w["egl"] * S[hh] + OS[CHUNK:]
            y_ref[rows, ls] = _gated_rms(WS[CHUNK:] + OS[:CHUNK], nw, z_ref[rows, ls]).astype(y_ref.dtype)
    for hh in range(HB):
        S_scr[hh] = S[hh]

    @pl.when(c == pl.num_programs(2) - 1)
    def _():
        for hh in range(HB):
            s_ref[hh] = S[hh]


def _gd_prompt(proj, gates, B, T, M, H, col0, lp, layer):
    tc = _pick(T, (256, 128, 64))
    nT = T // tc
    HB = HEADS_PER_STEP
    assert H % HB == 0 and col0 % HB == 0
    wb = HB * HEAD
    col = lambda j: pl.BlockSpec((tc, wb), lambda b, h, c: (b * nT + c, (col0 + j * H) // HB + h))
    cw = lambda j: pl.BlockSpec((None, CONV_W, wb), lambda b, h, c: (layer, 0, j * H // HB + h))
    buf = pltpu.VMEM((8 + tc, wb), f32)
    y, S = pl.pallas_call(
        functools.partial(_gd_prompt_body, HEAD ** -0.5, H, HB),
        grid=(B, H // HB, nT),
        in_specs=[col(0), col(1), col(2), col(3),
                  pl.BlockSpec((tc, LANE), lambda b, h, c: (b * nT + c, 0)),
                  cw(0), cw(1), cw(2),
                  pl.BlockSpec((None, 1, HEAD), lambda b, h, c: (layer, 0, 0))],
        out_specs=[pl.BlockSpec((tc, wb), lambda b, h, c: (b * nT + c, h)),
                   pl.BlockSpec((None, HB, HEAD, HEAD), lambda b, h, c: (b, h, 0, 0))],
        out_shape=[jax.ShapeDtypeStruct((M, H * HEAD), bf16), jax.ShapeDtypeStruct((B, H, HEAD, HEAD), f32)],
        scratch_shapes=[buf, buf, buf, pltpu.VMEM((HB, HEAD, HEAD), f32)],
        compiler_params=_params("arbitrary", "arbitrary", "arbitrary"), name="gd_prompt")(
            proj, proj, proj, proj, gates, lp["gd_conv_w"], lp["gd_conv_w"], lp["gd_conv_w"], lp["gd_norm_w"])
    return y, S


def _gd_sample_body(scale, H, q_ref, k_ref, v_ref, z_ref, gt_ref, csq_ref, csk_ref, csv_ref, cwq_ref, cwk_ref, cwv_ref,
                    nw_ref, s_ref, *rest):
    y_ref, so_ref, o_scr = rest[-3:]
    gt = gt_ref[...]

    def head(h, carry):
        ls = _head_lanes(h)

        def conv_silu(x_ref, cs_ref, cw_ref):
            y = cw_ref[CONV_W - 1:CONV_W, ls] * x_ref[:, ls]
            for j in range(CONV_W - 1):
                y = y + cw_ref[j:j + 1, ls] * cs_ref[j, :, ls]
            return _silu(y)

        q = _l2norm(conv_silu(q_ref, csq_ref, cwq_ref)) * scale
        k = _l2norm(conv_silu(k_ref, csk_ref, cwk_ref))
        v = conv_silu(v_ref, csv_ref, cwv_ref)
        eg = jnp.exp(_pick_lane(gt, h))
        beta = _pick_lane(gt, H + h)
        kT = k.T

        def vnew(j, S, kcol):
            egj = eg[j:j + 1, :]
            kS = jnp.sum(kcol * S, axis=0, keepdims=True)
            return egj, beta[j:j + 1, :] * (v[j:j + 1, :] - egj * kS)

        _state_step(s_ref, so_ref, o_scr, h, kT, q, vnew)
        y_ref[:, ls] = _gated_rms(o_scr[...], nw_ref[...], z_ref[:, ls]).astype(y_ref.dtype)
        return carry

    lax.fori_loop(0, H, head, 0)


def _gd_sample(proj, gates, y_all, row0, DB, H, col0, conv_state_t, state, stacked_prev, lp, layer):
    assert row0 % SB == 0 and DB % SB == 0 and col0 % H == 0
    rb = row0 // SB
    wh = H * HEAD
    col = lambda j: pl.BlockSpec((SB, wh), lambda b: (rb + b, col0 // H + j))
    cs = lambda j: pl.BlockSpec((None, CONV_W - 1, SB, wh), lambda b: (layer, 0, b, j))
    cw = lambda j: pl.BlockSpec((None, CONV_W, wh), lambda b: (layer, 0, j))
    st_spec = pl.BlockSpec((None, SB, H, HEAD, HEAD), lambda b: (layer, b, 0, 0, 0))
    s_shape, extra_in, extra_specs, aliases = _state_out(state, stacked_prev, 14)
    y, S = pl.pallas_call(
        functools.partial(_gd_sample_body, HEAD ** -0.5, H),
        grid=(DB // SB,),
        in_specs=[col(0), col(1), col(2), col(3),
                  pl.BlockSpec((SB, LANE), lambda b: (rb + b, 0)),
                  cs(0), cs(1), cs(2), cw(0), cw(1), cw(2),
                  pl.BlockSpec((None, 1, HEAD), lambda b: (layer, 0, 0)),
                  st_spec,
                  pl.BlockSpec(memory_space=pl.ANY)] + extra_specs,
        out_specs=[pl.BlockSpec((SB, wh), lambda b: (rb + b, 0)), st_spec],
        out_shape=[jax.ShapeDtypeStruct(y_all.shape, y_all.dtype), s_shape],
        scratch_shapes=[pltpu.VMEM((SB, HEAD), f32)],
        input_output_aliases={13: 0, **aliases},
        compiler_params=_params("arbitrary"), name="gd_sample")(
            proj, proj, proj, proj, gates, conv_state_t, conv_state_t, conv_state_t,
            lp["gd_conv_w"], lp["gd_conv_w"], lp["gd_conv_w"], lp["gd_norm_w"],
            state, y_all, *extra_in)
    return y, S


def kernel(x_prompt, x_sample, state_rg_h, state_rg_conv, state_hg_S, state_gd_S, state_gd_conv, norm_mix_w, norm_mlp_w, norm_final_w, w_in, rg_conv_w, rg_conv_b, rg_wa, rg_ba, rg_wx, rg_bx, rg_a_param, hg_lb_logits, hg_norm_w, gd_conv_w, gd_A_log, gd_dt_bias, gd_norm_w, w_br_rg, w_br_hg, w_br_gd, w_out, w_up, w_down):
    B, T, D = x_prompt.shape
    DB, DT, _ = x_sample.shape
    assert DT == 1
    depth = w_in.shape[0]
    RW = rg_ba.shape[-1]
    H = gd_A_log.shape[-1]
    PT = B * T
    M = PT + DB
    assert RW % HEAD == 0 and hg_norm_w.shape[-1] == HEAD and gd_norm_w.shape[-1] == HEAD
    HW = H * HEAD
    n_main = 2 * RW + 8 * HW
    merge_col0 = n_main + 2 * H
    assert w_in.shape[-1] == merge_col0 + 3 * D and n_main % 1024 == 0 and n_main % LANE == 0
    hg_col0 = 2 * RW // HEAD
    gd_col0 = hg_col0 + 4 * H

    row3 = lambda a: a.reshape(depth, 1, a.shape[-1])
    lane_pad = lambda a: row3(jnp.pad(a.astype(f32), ((0, 0), (0, LANE - a.shape[-1]))))
    lp = dict(rg_conv_w=rg_conv_w, rg_conv_b=row3(rg_conv_b), rg_wa=rg_wa, rg_wx=rg_wx, rg_ba=row3(rg_ba),
              rg_bx=row3(rg_bx), rg_a_param=row3(rg_a_param), gd_conv_w=gd_conv_w, gd_norm_w=row3(gd_norm_w))
    alog_pad, dtb_pad = lane_pad(gd_A_log), lane_pad(gd_dt_bias)
    hg_nw = row3(hg_norm_w)
    lb = row3(_lower_bounds(hg_lb_logits.astype(f32)))
    rg_conv_t = jnp.swapaxes(state_rg_conv, 1, 2)
    gd_conv_t = jnp.swapaxes(state_gd_conv, 1, 2)

    w_in_t = jnp.swapaxes(w_in, 1, 2)
    x = jnp.concatenate([x_prompt.reshape(PT, D), x_sample.reshape(DB, D)], axis=0)
    u = _rms_norm(x, norm_mix_w[0:1], bf16)

    p_states, s_states = [], []
    y_final = None
    s_hgS = s_gdS = None
    for l in range(depth):
        proj = _gemm_wres(u, w_in_t, l, 0, n_main, 1024, name="in_proj", w_is_nk=True, tm_cands=BIG_TM)
        gates = _gd_gate_proj(u, w_in_t, n_main // LANE, alog_pad, dtb_pad, PT, H, l)

        y_rg, p_h = _rg_prompt(proj, B, T, M, lp, l)
        y_rg, s_h = _rg_sample(proj, y_rg, PT, DB, rg_conv_t, state_rg_h, lp, l)
        y_hg, p_hgS = _hg_prompt(proj, B, T, M, H, hg_col0, lb, hg_nw, l)
        y_hg, s_hgS = _hg_sample(proj, y_hg, PT, DB, H, hg_col0, lb, hg_nw, state_hg_S, s_hgS, l)
        y_gd, p_gdS = _gd_prompt(proj, gates, B, T, M, H, gd_col0, lp, l)
        y_gd, s_gdS = _gd_sample(proj, gates, y_gd, PT, DB, H, gd_col0, gd_conv_t, state_gd_S, s_gdS, lp, l)

        gq = gd_col0 * HEAD
        tail = lambda c0, w: jnp.stack([lax.slice(proj, (b * T + T - (CONV_W - 1), c0), (b * T + T, c0 + w))
                                        for b in range(B)], axis=0)
        last = lambda c0, w: lax.slice(proj, (PT, c0), (M, c0 + w))[:, None, :]
        p_states.append((p_h, tail(0, RW), p_hgS, p_gdS, tail(gq, 3 * HW)))
        s_states.append((s_h,
                         jnp.concatenate([state_rg_conv[l][:, 1:], last(0, RW)], axis=1),
                         None, None,
                         jnp.concatenate([state_gd_conv[l][:, 1:], last(gq, 3 * HW)], axis=1)))

        mixed = _mix(u, y_rg, y_hg, y_gd, w_in_t, w_br_rg, w_br_hg, w_br_gd, l, merge_col0)
        x, hmid = _proj_add_norm(mixed, w_out, l, x, norm_mlp_w[l:l + 1])
        hh = _gemm_wres(hmid, w_up, l, 0, w_up.shape[-1], 1024, epi=lambda a: jnp.square(jnp.maximum(a, 0.0)),
                        out_dtype=bf16, name="mlp_up", tm_cands=BIG_TM)
        x = _gemm_wres(hh, w_down, l, 0, D, 512, name="mlp_down", single_buffer_w=True, res=x)
        if l + 1 < depth:
            u = _rms_norm(x, norm_mix_w[l + 1:l + 2], bf16)
        else:
            y_final = _rms_norm(x, norm_final_w.reshape(1, D), f32)

    def stack(sts, j, like):
        return jnp.stack([s[j] for s in sts], axis=0).astype(like.dtype)

    return (y_final[:PT].reshape(B, T, D), y_final[PT:].reshape(DB, DT, D),
            stack(p_states, 0, state_rg_h), stack(p_states, 1, state_rg_conv), stack(p_states, 2, state_hg_S),
            stack(p_states, 3, state_gd_S), stack(p_states, 4, state_gd_conv),
            stack(s_states, 0, state_rg_h), stack(s_states, 1, state_rg_conv), s_hgS.astype(state_hg_S.dtype),
            s_gdS.astype(state_gd_S.dtype), stack(s_states, 4, state_gd_conv))
```

```python
import functools

import jax
import jax.numpy as jnp
from jax import lax
from jax.experimental import pallas as pl
from jax.experimental.pallas import tpu as pltpu

f32 = jnp.float32
bf16 = jnp.bfloat16

EPS = 1e-6
RG_C = 8.0
HEAD = 128
LANE = 128
SUBLANES = 8
CHUNK = 64
SUB = 8
INV_BLOCK = 16
HEADS_PER_STEP = 8
CONV_W = 4
VMEM_LIMIT = 56 * 1024 * 1024

_NT = (((1,), (1,)), ((), ()))
_TN = (((0,), (0,)), ((), ()))


def _params(*sem):
    return pltpu.CompilerParams(dimension_semantics=sem, vmem_limit_bytes=VMEM_LIMIT)


def _pick(n, cands):
    for c in cands:
        if n % c == 0:
            return c
    raise ValueError(f"no tile for {n} among {cands}")


def _mm(a, b, dims=None):
    a = a.astype(bf16)
    b = b.astype(bf16)
    if dims is None:
        return jnp.dot(a, b, preferred_element_type=f32)
    return lax.dot_general(a, b, dims, preferred_element_type=f32)


def _split3(x):
    hi = x.astype(bf16)
    r = x - hi.astype(f32)
    mid = r.astype(bf16)
    lo = (r - mid.astype(f32)).astype(bf16)
    return hi, mid, lo


def _mm_exact_lhs(a_bf16, x):
    hi, mid, lo = _split3(x)
    return (jnp.dot(a_bf16, hi, preferred_element_type=f32) + jnp.dot(a_bf16, mid, preferred_element_type=f32)
            + jnp.dot(a_bf16, lo, preferred_element_type=f32))


def _mm_hi(a, b):
    ah, am, _ = _split3(a)
    bh, bm, _ = _split3(b)
    d = functools.partial(jnp.dot, preferred_element_type=f32)
    return d(ah, bh) + (d(ah, bm) + d(am, bh))


def _expm1_neg(x, ex):
    return -jnp.tanh(0.5 * x) * (ex + 1.0)


def _softplus(x):
    return jnp.maximum(x, 0.0) + jnp.log1p(jnp.exp(-jnp.abs(x)))


def _silu(x):
    return x * jax.nn.sigmoid(x)


def _gated_rms(o, w, z):
    o = o * lax.rsqrt(jnp.mean(o * o, axis=-1, keepdims=True) + EPS) * w
    return o * _silu(z)


def _l2norm(x):
    return x * lax.rsqrt(jnp.sum(x * x, axis=-1, keepdims=True) + EPS)


def _causal_conv4(xbuf, cw_ref, tc, ls=slice(None)):
    assert CONV_W == 4
    w0, w1, w2, w3 = (cw_ref[j:j + 1, ls] for j in range(CONV_W))
    x0 = xbuf[8:8 + tc, ls]
    row = lax.broadcasted_iota(jnp.int32, x0.shape, 0)
    r2 = pltpu.roll(x0, 2, axis=0)
    xm2 = jnp.concatenate([xbuf[pl.ds(8 - 2, SUBLANES), ls], r2[SUBLANES:]], axis=0)
    even = w3 * x0 + w1 * xm2
    odd = w2 * x0 + w0 * xm2
    odd_before = w2 * xbuf[7:8, ls] + w0 * xbuf[5:6, ls]
    return even + jnp.where(row == 0, odd_before, pltpu.roll(odd, 1, axis=0))


def _tril_mask(n, strict=False):
    r = lax.broadcasted_iota(jnp.int32, (n, n), 0)
    c = lax.broadcasted_iota(jnp.int32, (n, n), 1)
    return (r > c) if strict else (r >= c)


def _chunk_tril(n):
    r = lax.broadcasted_iota(jnp.int32, (n, n), 0)
    c = lax.broadcasted_iota(jnp.int32, (n, n), 1)
    same = (r // CHUNK) == (c // CHUNK)
    return jnp.where(jnp.logical_and(r >= c, same), 1.0, 0.0).astype(bf16)


def _norm_body(x_ref, w_ref, n_ref):
    x = x_ref[...]
    y = x * lax.rsqrt(jnp.mean(x * x, axis=-1, keepdims=True) + EPS)
    n_ref[...] = (y * w_ref[...]).astype(n_ref.dtype)


def _rms_norm(x, w_row, out_dtype):
    M, D = x.shape
    tm = _pick(M, (416, 320, 256, 128, 64, 16))
    row = pl.BlockSpec((tm, D), lambda m: (m, 0))
    return pl.pallas_call(
        _norm_body, grid=(M // tm,), in_specs=[row, pl.BlockSpec((1, D), lambda m: (0, 0))], out_specs=row,
        out_shape=jax.ShapeDtypeStruct((M, D), out_dtype),
        compiler_params=_params("arbitrary"), name="rms_norm")(x, w_row)


def _gemm_wres_body(epi, w_is_nk, has_res, a_ref, w_ref, *rest):
    res_ref = rest[0] if has_res else None
    o_ref, wb = rest[-2:]

    @pl.when(pl.program_id(1) == 0)
    def _():
        w = w_ref[...]
        wb[...] = (w.T if w_is_nk else w).astype(bf16)
    acc = jnp.dot(a_ref[...], wb[...], preferred_element_type=f32)
    if epi is not None:
        acc = epi(acc)
    if has_res:
        acc = res_ref[...] + acc
    o_ref[...] = acc.astype(o_ref.dtype)


BIG_TM = (1040, 640, 512, 320, 256, 128, 64, 16)


def _gemm_wres(a, w, layer, col_block0, n_out, tn, epi=None, out_dtype=f32, name="gemm", w_is_nk=False,
               tm_cands=(640, 512, 320, 256, 128, 64, 16), single_buffer_w=False, res=None):
    M, K = a.shape
    tm = _pick(M, tm_cands)
    mode = dict(pipeline_mode=pl.Buffered(1)) if single_buffer_w else {}
    if w_is_nk:
        w_spec = pl.BlockSpec((None, tn, K), lambda n, m: (layer, n + col_block0, 0), **mode)
    else:
        w_spec = pl.BlockSpec((None, K, tn), lambda n, m: (layer, 0, n + col_block0), **mode)
    tile = pl.BlockSpec((tm, tn), lambda n, m: (m, n))
    return pl.pallas_call(
        functools.partial(_gemm_wres_body, epi, w_is_nk, res is not None),
        grid=(n_out // tn, M // tm),
        in_specs=[pl.BlockSpec((tm, K), lambda n, m: (m, 0)), w_spec] + ([tile] if res is not None else []),
        out_specs=tile,
        out_shape=jax.ShapeDtypeStruct((M, n_out), out_dtype),
        scratch_shapes=[pltpu.VMEM((K, tn), bf16)],
        compiler_params=_params("arbitrary", "arbitrary"), name=name)(a, w, *([res] if res is not None else []))


def _proj_norm_body(a_ref, w_ref, x_ref, nw_ref, xo_ref, n_ref, wb):
    @pl.when(pl.program_id(0) == 0)
    def _():
        wb[...] = w_ref[...].astype(bf16)
    x = x_ref[...] + jnp.dot(a_ref[...], wb[...], preferred_element_type=f32)
    xo_ref[...] = x
    y = x * lax.rsqrt(jnp.mean(x * x, axis=-1, keepdims=True) + EPS)
    n_ref[...] = (y * nw_ref[...]).astype(n_ref.dtype)


def _proj_add_norm(a, w, layer, x, nw_row):
    M, K = a.shape
    D = w.shape[-1]
    tm = _pick(M, (320, 256, 128, 64, 16))
    row = lambda width: pl.BlockSpec((tm, width), lambda m: (m, 0))
    return pl.pallas_call(
        _proj_norm_body,
        grid=(M // tm,),
        in_specs=[row(K), pl.BlockSpec((None, K, D), lambda m: (layer, 0, 0), pipeline_mode=pl.Buffered(1)),
                  row(D), pl.BlockSpec((1, D), lambda m: (0, 0))],
        out_specs=[row(D), row(D)],
        out_shape=[jax.ShapeDtypeStruct((M, D), f32), jax.ShapeDtypeStruct((M, D), bf16)],
        scratch_shapes=[pltpu.VMEM((K, D), bf16)],
        compiler_params=_params("arbitrary"), name="out_proj_norm")(a, w, x, nw_row)


MERGE_SHIFT = 16


def _mix_body(u_ref, yr_ref, yh_ref, yg_ref, wm0, wm1, wm2, wx0, wx1, wx2, wr_ref, wh_ref, wg_ref, o_ref, wmb, wbb):
    tn = o_ref.shape[1]

    @pl.when(pl.program_id(1) == 0)
    def _():
        for b, (wm, wx) in enumerate(((wm0, wx0), (wm1, wx1), (wm2, wx2))):
            wcat = jnp.concatenate([wm[...], wx[...]], axis=0)
            wmb[b] = wcat[MERGE_SHIFT:MERGE_SHIFT + tn].T.astype(bf16)
        for b, wr in enumerate((wr_ref, wh_ref, wg_ref)):
            wbb[b] = wr[...].astype(bf16)

    u = u_ref[...]
    acc = None
    for b, y_ref in enumerate((yr_ref, yh_ref, yg_ref)):
        gate = jax.nn.sigmoid(jnp.dot(u, wmb[b], preferred_element_type=f32))
        p = jnp.dot(y_ref[...], wbb[b], preferred_element_type=f32)
        acc = gate * p if acc is None else acc + gate * p
    o_ref[...] = acc.astype(o_ref.dtype)


def _mix(u, y_rg, y_hg, y_gd, w_in_t, w_br_rg, w_br_hg, w_br_gd, layer, merge_col0):
    M, D = u.shape
    W = y_rg.shape[1]
    tn = 512
    tm = _pick(M, (640, 416, 320, 256, 128, 64, 16))
    nt = D // tn
    assert (merge_col0 - MERGE_SHIFT) % tn == 0 and tn % MERGE_SHIFT == 0
    base = (merge_col0 - MERGE_SHIFT) // tn
    r = tn // MERGE_SHIFT
    once = dict(pipeline_mode=pl.Buffered(1))

    def wm_spec(b):
        return pl.BlockSpec((None, tn, D), lambda n, m: (layer, base + b * nt + n, 0), **once)

    def wx_spec(b):
        return pl.BlockSpec((None, MERGE_SHIFT, D), lambda n, m: (layer, (base + b * nt + n + 1) * r, 0), **once)

    row = lambda w: pl.BlockSpec((tm, w), lambda n, m: (m, 0))
    br = pl.BlockSpec((None, W, tn), lambda n, m: (layer, 0, n), **once)
    return pl.pallas_call(
        _mix_body,
        grid=(nt, M // tm),
        in_specs=[row(D), row(W), row(W), row(W), wm_spec(0), wm_spec(1), wm_spec(2),
                  wx_spec(0), wx_spec(1), wx_spec(2), br, br, br],
        out_specs=pl.BlockSpec((tm, tn), lambda n, m: (m, n)),
        out_shape=jax.ShapeDtypeStruct((M, D), bf16),
        scratch_shapes=[pltpu.VMEM((3, D, tn), bf16), pltpu.VMEM((3, W, tn), bf16)],
        compiler_params=_params("arbitrary", "arbitrary"), name="mix")(
            u, y_rg, y_hg, y_gd, w_in_t, w_in_t, w_in_t, w_in_t, w_in_t, w_in_t, w_br_rg, w_br_hg, w_br_gd)


def _lb_body(x_ref, o_ref):
    x = x_ref[...]
    depth = x.shape[0]
    m = jnp.max(x, axis=0, keepdims=True)
    e = jnp.exp(x - m)
    p = e / jnp.sum(e, axis=0, keepdims=True)
    acc = jnp.zeros_like(p[0:1])
    o_ref[0:1, :] = acc
    for l in range(1, depth):
        acc = acc + p[l:l + 1]
        o_ref[l:l + 1, :] = acc


def _lower_bounds(logits):
    return pl.pallas_call(_lb_body, out_shape=jax.ShapeDtypeStruct(logits.shape, f32), name="hg_lower_bounds")(logits)


def _rg_gates(xc, wa, wx, ba, bx, sp):
    xb = xc.astype(bf16)
    r = jax.nn.sigmoid(jnp.dot(xb, wa.astype(bf16), preferred_element_type=f32) + ba)
    i = jax.nn.sigmoid(jnp.dot(xb, wx.astype(bf16), preferred_element_type=f32) + bx)
    log_a = (-RG_C) * r * sp
    a = jnp.exp(log_a)
    m2 = _expm1_neg(2.0 * log_a, a * a)
    mult = jnp.where(m2 > 0.0, m2 * lax.rsqrt(m2), 0.0)
    return a, mult, i


def _rg_prompt_body(x_ref, gate_ref, cw_ref, cb_ref, wa_ref, wx_ref, ba_ref, bx_ref, ap_ref, y_ref, h_ref, xbuf, hprev):
    c = pl.program_id(1)
    tc = x_ref.shape[0]
    nblk = x_ref.shape[1] // HEAD

    @pl.when(c == 0)
    def _():
        xbuf[0:8, :] = jnp.zeros((8, xbuf.shape[1]), f32)
        hprev[...] = jnp.zeros_like(hprev)

    xbuf[8:8 + tc, :] = x_ref[...]
    row = lax.broadcasted_iota(jnp.int32, (tc, HEAD), 0)
    first = jnp.logical_and(row == 0, c == 0)
    for n in range(nblk):
        ls = slice(n * HEAD, (n + 1) * HEAD)
        xc = _causal_conv4(xbuf, cw_ref, tc, ls) + cb_ref[:, ls]
        sp = _softplus(-ap_ref[:, ls])
        a, mult, i = _rg_gates(xc, wa_ref[n], wx_ref[n], ba_ref[:, ls], bx_ref[:, ls], sp)
        mult = jnp.where(first, 1.0, mult)
        b = mult * (i * xc)
        s = 1
        while s < SUBLANES:
            keep = (row % SUBLANES) >= s
            a_sh = jnp.where(keep, pltpu.roll(a, s, axis=0), 1.0)
            b_sh = jnp.where(keep, pltpu.roll(b, s, axis=0), 0.0)
            b = a * b_sh + b
            a = a * a_sh
            s *= 2
        carry = hprev[:, ls]
        groups = []
        for g in range(tc // SUBLANES):
            rows = slice(g * SUBLANES, (g + 1) * SUBLANES)
            hg = b[rows] + a[rows] * carry
            groups.append(hg)
            carry = hg[SUBLANES - 1:SUBLANES, :]
        h = jnp.concatenate(groups, axis=0)
        hprev[:, ls] = carry
        y_ref[:, ls] = (h * jax.nn.gelu(gate_ref[:, ls], approximate=True)).astype(y_ref.dtype)
    xbuf[0:8, :] = xbuf[tc:tc + 8, :]
    h_ref[...] = hprev[...]


def _rg_prompt(proj, B, T, M, lp, layer):
    W = lp["rg_ba"].shape[-1]
    tc = _pick(T, (256, 128, 64))
    nT = T // tc
    nblk = W // HEAD
    vec = pl.BlockSpec((None, 1, W), lambda b, c: (layer, 0, 0))
    blk = pl.BlockSpec((None, nblk, HEAD, HEAD), lambda b, c: (layer, 0, 0, 0))
    y, h = pl.pallas_call(
        _rg_prompt_body,
        grid=(B, nT),
        in_specs=[pl.BlockSpec((tc, W), lambda b, c: (b * nT + c, 0)),
                  pl.BlockSpec((tc, W), lambda b, c: (b * nT + c, 1)),
                  pl.BlockSpec((None, CONV_W, W), lambda b, c: (layer, 0, 0)),
                  vec, blk, blk, vec, vec, vec],
        out_specs=[pl.BlockSpec((tc, W), lambda b, c: (b * nT + c, 0)),
                   pl.BlockSpec((None, 1, W), lambda b, c: (b, 0, 0))],
        out_shape=[jax.ShapeDtypeStruct((M, W), bf16), jax.ShapeDtypeStruct((B, 1, W), f32)],
        scratch_shapes=[pltpu.VMEM((8 + tc, W), f32), pltpu.VMEM((1, W), f32)],
        compiler_params=_params("arbitrary", "arbitrary"), name="rg_prompt")(
            proj, proj, lp["rg_conv_w"], lp["rg_conv_b"], lp["rg_wa"], lp["rg_wx"], lp["rg_ba"], lp["rg_bx"],
            lp["rg_a_param"])
    return y, h[:, 0]


def _rg_sample_body(x_ref, gate_ref, cs_ref, h0_ref, cw_ref, cb_ref, wa_ref, wx_ref, ba_ref, bx_ref, ap_ref,
                    yin_ref, y_ref, h_ref):
    del yin_ref
    nblk = x_ref.shape[1] // HEAD
    for n in range(nblk):
        ls = slice(n * HEAD, (n + 1) * HEAD)
        xc = cb_ref[:, ls] + cw_ref[CONV_W - 1:CONV_W, ls] * x_ref[:, ls]
        for j in range(CONV_W - 1):
            xc = xc + cw_ref[j:j + 1, ls] * cs_ref[j, :, ls]
        sp = _softplus(-ap_ref[:, ls])
        a, mult, i = _rg_gates(xc, wa_ref[n], wx_ref[n], ba_ref[:, ls], bx_ref[:, ls], sp)
        h = a * h0_ref[:, ls] + mult * (i * xc)
        h_ref[:, ls] = h
        y_ref[:, ls] = (h * jax.nn.gelu(gate_ref[:, ls], approximate=True)).astype(y_ref.dtype)


def _rg_sample(proj, y_all, row0, DB, conv_state_t, h0, lp, layer):
    W = lp["rg_ba"].shape[-1]
    nblk = W // HEAD
    assert row0 % DB == 0
    rb = row0 // DB
    vec = pl.BlockSpec((None, 1, W), lambda i: (layer, 0, 0))
    blk = pl.BlockSpec((None, nblk, HEAD, HEAD), lambda i: (layer, 0, 0, 0))
    y, h = pl.pallas_call(
        _rg_sample_body,
        grid=(1,),
        in_specs=[pl.BlockSpec((DB, W), lambda i: (rb, 0)),
                  pl.BlockSpec((DB, W), lambda i: (rb, 1)),
                  pl.BlockSpec((None, CONV_W - 1, DB, W), lambda i: (layer, 0, 0, 0)),
                  pl.BlockSpec((None, DB, W), lambda i: (layer, 0, 0)),
                  pl.BlockSpec((None, CONV_W, W), lambda i: (layer, 0, 0)),
                  vec, blk, blk, vec, vec, vec,
                  pl.BlockSpec(memory_space=pl.ANY)],
        out_specs=[pl.BlockSpec((DB, W), lambda i: (rb, 0)),
                   pl.BlockSpec((DB, W), lambda i: (0, 0))],
        out_shape=[jax.ShapeDtypeStruct(y_all.shape, y_all.dtype), jax.ShapeDtypeStruct((DB, W), f32)],
        input_output_aliases={11: 0},
        compiler_params=_params("arbitrary"), name="rg_sample")(
            proj, proj, conv_state_t, h0, lp["rg_conv_w"], lp["rg_conv_b"], lp["rg_wa"], lp["rg_wx"],
            lp["rg_ba"], lp["rg_bx"], lp["rg_a_param"], y_all)
    return y, h


def _hg_gates(fx, lb):
    f = lb + (1.0 - lb) * jax.nn.sigmoid(fx)
    k = (1.0 - lb) * jax.nn.sigmoid(-fx)
    return f, k


def _hg_intra_diag(G, q, k):
    lane = lax.broadcasted_iota(jnp.int32, (SUB, CHUNK), 1)
    blocks = []
    for i in range(CHUNK // SUB):
        sl = slice(i * SUB, (i + 1) * SUB)
        g_i, q_i, k_i = G[sl], q[sl], k[sl]
        a_d = jnp.zeros((SUB, CHUNK), f32)
        for s in range(SUB):
            e = jnp.exp(g_i - g_i[s:s + 1, :])
            col = jnp.sum(q_i * k_i[s:s + 1, :] * e, axis=-1, keepdims=True)
            a_d = jnp.where(lane == i * SUB + s, col, a_d)
        blocks.append(a_d)
    return jnp.where(_tril_mask(CHUNK), jnp.concatenate(blocks, axis=0), 0.0)


def _hg_intra_off(G, q, k):
    nsub = CHUNK // SUB
    row = lax.broadcasted_iota(jnp.int32, (CHUNK, HEAD), 0)
    q_parts, k_parts = [], []
    for j in range(nsub - 1):
        g_e = G[(j + 1) * SUB - 1:(j + 1) * SUB, :]
        q_parts.append(jnp.where(row >= (j + 1) * SUB, q * jnp.exp(G - g_e), 0.0))
        in_j = jnp.logical_and(row >= j * SUB, row < (j + 1) * SUB)
        k_parts.append(jnp.where(in_j, k * jnp.exp(g_e - G), 0.0))
    return _mm(jnp.concatenate(q_parts, axis=1), jnp.concatenate(k_parts, axis=1), _NT)


def _hg_prompt_body(scale, HB, q_ref, f_ref, i_ref, g_ref, lb_ref, nw_ref, y_ref, s_ref, S_scr):
    c = pl.program_id(2)
    nchunk = q_ref.shape[0] // CHUNK

    @pl.when(c == 0)
    def _():
        S_scr[...] = jnp.zeros_like(S_scr)

    nw = nw_ref[...]
    tril = _chunk_tril(nchunk * CHUNK)
    work = []
    for hh in range(HB):
        ls = slice(hh * HEAD, (hh + 1) * HEAD)
        f_all, k_all = _hg_gates(f_ref[:, ls], lb_ref[:, ls])
        q_all = q_ref[:, ls] * scale
        G_all = _mm_exact_lhs(tril, jnp.log(f_all))
        for ci in range(nchunk):
            rows = slice(ci * CHUNK, (ci + 1) * CHUNK)
            work.append(dict(hh=hh, ls=ls, rows=rows, G=G_all[rows], q=q_all[rows], k=k_all[rows]))
    for w in work:
        w["A"] = _hg_intra_diag(w["G"], w["q"], w["k"])
    for w in work:
        w["A"] = w["A"] + _hg_intra_off(w["G"], w["q"], w["k"])
    for w in work:
        G = w["G"]
        kT, GT = w["k"].T, G.T
        g_last = GT[:, CHUNK - 1:CHUNK]
        w["dec"] = jnp.exp(g_last)
        w["upd"] = _mm(kT * jnp.exp(g_last - GT), i_ref[w["rows"], w["ls"]])
        w["lhs"] = jnp.concatenate([w["A"], w["q"] * jnp.exp(G)], axis=1)
    S = [S_scr[hh] for hh in range(HB)]
    for ci in range(nchunk):
        for hh in range(HB):
            w = work[hh * nchunk + ci]
            rows, ls = w["rows"], w["ls"]
            o = _mm(w["lhs"], jnp.concatenate([i_ref[rows, ls], S[hh]], axis=0))
            S[hh] = S[hh] * w["dec"] + w["upd"]
            y_ref[rows, ls] = _gated_rms(o, nw, g_ref[rows, ls]).astype(y_ref.dtype)
    for hh in range(HB):
        S_scr[hh] = S[hh]

    @pl.when(c == pl.num_programs(2) - 1)
    def _():
        for hh in range(HB):
            s_ref[hh] = S[hh]


def _hg_prompt(proj, B, T, M, H, col0, lb, norm_w, layer):
    tc = _pick(T, (256, 128, 64))
    nT = T // tc
    HB = HEADS_PER_STEP
    assert H % HB == 0 and col0 % HB == 0
    wb = HB * HEAD
    col = lambda j: pl.BlockSpec((tc, wb), lambda b, h, c: (b * nT + c, (col0 + j * H) // HB + h))
    y, S = pl.pallas_call(
        functools.partial(_hg_prompt_body, HEAD ** -0.5, HB),
        grid=(B, H // HB, nT),
        in_specs=[col(0), col(1), col(2), col(3),
                  pl.BlockSpec((None, 1, wb), lambda b, h, c: (layer, 0, h)),
                  pl.BlockSpec((None, 1, HEAD), lambda b, h, c: (layer, 0, 0))],
        out_specs=[pl.BlockSpec((tc, wb), lambda b, h, c: (b * nT + c, h)),
                   pl.BlockSpec((None, HB, HEAD, HEAD), lambda b, h, c: (b, h, 0, 0))],
        out_shape=[jax.ShapeDtypeStruct((M, H * HEAD), bf16), jax.ShapeDtypeStruct((B, H, HEAD, HEAD), f32)],
        scratch_shapes=[pltpu.VMEM((HB, HEAD, HEAD), f32)],
        compiler_params=_params("arbitrary", "arbitrary", "arbitrary"), name="hg_prompt")(
            proj, proj, proj, proj, lb, norm_w)
    return y, S


SB = 16


def _state_step(s_ref, so_ref, o_scr, h, kT, q, vnew_fn):
    for j in range(SB):
        S = s_ref[j, h]
        kcol = kT[:, j:j + 1]
        d, vnew = vnew_fn(j, S, kcol)
        so_ref[j, h] = d * S + kcol * vnew
    qb = q.astype(bf16)
    for j in range(SB):
        o_scr[j:j + 1, :] = jnp.dot(qb, so_ref[j, h].astype(bf16), preferred_element_type=f32)[j:j + 1, :]


def _head_lanes(h):
    return pl.ds(pl.multiple_of(h * HEAD, HEAD), HEAD)


def _hg_sample_body(scale, H, q_ref, f_ref, i_ref, g_ref, lb_ref, nw_ref, s_ref, *rest):
    y_ref, so_ref, o_scr = rest[-3:]

    def head(h, carry):
        ls = _head_lanes(h)
        f, k = _hg_gates(f_ref[:, ls], lb_ref[:, ls])
        q = q_ref[:, ls] * scale
        v = i_ref[:, ls]
        fT, kT = f.T, k.T

        def vnew(j, S, kcol):
            return fT[:, j:j + 1], v[j:j + 1, :]

        _state_step(s_ref, so_ref, o_scr, h, kT, q, vnew)
        y_ref[:, ls] = _gated_rms(o_scr[...], nw_ref[...], g_ref[:, ls]).astype(y_ref.dtype)
        return carry

    lax.fori_loop(0, H, head, 0)


def _state_out(state, stacked_prev, n_in):
    extra_in, extra_specs, aliases = [], [], {}
    if stacked_prev is not None:
        extra_in, extra_specs, aliases = [stacked_prev], [pl.BlockSpec(memory_space=pl.ANY)], {n_in: 1}
    return jax.ShapeDtypeStruct(state.shape, f32), extra_in, extra_specs, aliases


def _hg_sample(proj, y_all, row0, DB, H, col0, lb, norm_w, state, stacked_prev, layer):
    assert row0 % SB == 0 and DB % SB == 0 and col0 % H == 0
    rb = row0 // SB
    wh = H * HEAD
    col = lambda j: pl.BlockSpec((SB, wh), lambda b: (rb + b, col0 // H + j))
    st_spec = pl.BlockSpec((None, SB, H, HEAD, HEAD), lambda b: (layer, b, 0, 0, 0))
    s_shape, extra_in, extra_specs, aliases = _state_out(state, stacked_prev, 8)
    y, S = pl.pallas_call(
        functools.partial(_hg_sample_body, HEAD ** -0.5, H),
        grid=(DB // SB,),
        in_specs=[col(0), col(1), col(2), col(3),
                  pl.BlockSpec((None, 1, wh), lambda b: (layer, 0, 0)),
                  pl.BlockSpec((None, 1, HEAD), lambda b: (layer, 0, 0)),
                  st_spec,
                  pl.BlockSpec(memory_space=pl.ANY)] + extra_specs,
        out_specs=[pl.BlockSpec((SB, wh), lambda b: (rb + b, 0)), st_spec],
        out_shape=[jax.ShapeDtypeStruct(y_all.shape, y_all.dtype), s_shape],
        scratch_shapes=[pltpu.VMEM((SB, HEAD), f32)],
        input_output_aliases={7: 0, **aliases},
        compiler_params=_params("arbitrary"), name="hg_sample")(
            proj, proj, proj, proj, lb, norm_w, state, y_all, *extra_in)
    return y, S


def _pick_lane(x, idx):
    lane = lax.broadcasted_iota(jnp.int32, x.shape, 1)
    col = jnp.sum(jnp.where(lane == idx, x, 0.0), axis=1, keepdims=True)
    return jnp.broadcast_to(col, (x.shape[0], HEAD))


def _gd_gate_body(H, PT, u_ref, w_ref, alog_ref, dtb_ref, o_ref, wb):
    i = pl.program_id(0)
    tm = u_ref.shape[0]

    @pl.when(i == 0)
    def _():
        wb[...] = w_ref[...].T.astype(bf16)

    x = jnp.dot(u_ref[...], wb[...], preferred_element_type=f32)
    lane = lax.broadcasted_iota(jnp.int32, x.shape, 1)
    row = lax.broadcasted_iota(jnp.int32, x.shape, 0) + i * tm
    g = jnp.where(lane < H, -jnp.exp(alog_ref[...]) * _softplus(x + dtb_ref[...]), 0.0)
    G = _mm_exact_lhs(_chunk_tril(tm), g)
    G = jnp.where(row < PT, G, g)
    o_ref[...] = jnp.where(lane < H, G, jax.nn.sigmoid(x))


def _gd_gate_proj(u, w_in_t, row_block, alog_pad, dtb_pad, PT, H, layer):
    M, D = u.shape
    tm = _pick(M, (640, 320, 128, 64))
    assert PT % CHUNK == 0 and tm % CHUNK == 0
    vec = pl.BlockSpec((None, 1, LANE), lambda i: (layer, 0, 0))
    return pl.pallas_call(
        functools.partial(_gd_gate_body, H, PT),
        grid=(M // tm,),
        in_specs=[pl.BlockSpec((tm, D), lambda i: (i, 0)),
                  pl.BlockSpec((None, LANE, D), lambda i: (layer, row_block, 0)), vec, vec],
        out_specs=pl.BlockSpec((tm, LANE), lambda i: (i, 0)),
        out_shape=jax.ShapeDtypeStruct((M, LANE), f32),
        scratch_shapes=[pltpu.VMEM((D, LANE), bf16)],
        compiler_params=_params("arbitrary"), name="gd_gate_proj")(u, w_in_t, alog_pad, dtb_pad)


def _unit_lower_inverses(Ns):
    r = lax.broadcasted_iota(jnp.int32, (CHUNK, CHUNK), 0)
    c = lax.broadcasted_iota(jnp.int32, (CHUNK, CHUNK), 1)
    same = lambda n: (r // n) == (c // n)
    assert CHUNK == 4 * INV_BLOCK
    dot = functools.partial(jnp.dot, preferred_element_type=f32)
    cast = lambda xs: [x.astype(bf16) for x in xs]
    Rs = [jnp.where(same(INV_BLOCK), N, 0.0) for N in Ns]
    Rb = cast(Rs)
    Ps = [dot(rb, rb) for rb in Rb]
    p = 2
    while p < INV_BLOCK:
        Pb = cast(Ps)
        if 2 * p < INV_BLOCK:
            PMs = [dot(pb, jnp.concatenate([rb, pb], axis=1)) for rb, pb in zip(Rb, Pb)]
            Rs = [R + P + PM[:, :CHUNK] for R, P, PM in zip(Rs, Ps, PMs)]
            Ps = [PM[:, CHUNK:] for PM in PMs]
            Rb = cast(Rs)
        else:
            PRs = [dot(pb, rb) for rb, pb in zip(Rb, Pb)]
            Rs = [R + P + PR for R, P, PR in zip(Rs, Ps, PRs)]
        p *= 2
    eye = jnp.where(r == c, 1.0, 0.0)
    Nb = cast(Ns)
    for n in (2 * INV_BLOCK, 4 * INV_BLOCK):
        off = jnp.logical_and(same(n), jnp.logical_not(same(n // 2)))
        Db = cast([eye + R for R in Rs])
        DCs = [dot(db, jnp.where(off, nb, jnp.zeros_like(nb))) for db, nb in zip(Db, Nb)]
        DCDs = [dot(dc, db) for dc, db in zip(cast(DCs), Db)]
        Rs = [R + DCD for R, DCD in zip(Rs, DCDs)]
    return Rs


def _gd_prompt_body(scale, H, HB, q_ref, k_ref, v_ref, z_ref, gt_ref, cwq_ref, cwk_ref, cwv_ref, nw_ref,
                    y_ref, s_ref, xq, xk, xv, S_scr):
    hb = pl.program_id(1)
    c = pl.program_id(2)
    tc = q_ref.shape[0]
    nchunk = tc // CHUNK

    @pl.when(c == 0)
    def _():
        for xb in (xq, xk, xv):
            xb[0:8, :] = jnp.zeros((8, xb.shape[1]), f32)
        S_scr[...] = jnp.zeros_like(S_scr)

    def conv_silu(x_ref, xb, cw_ref):
        xb[8:8 + tc, :] = x_ref[...]
        y = _causal_conv4(xb, cw_ref, tc)
        xb[0:8, :] = xb[tc:tc + 8, :]
        return _silu(y)

    qc = conv_silu(q_ref, xq, cwq_ref)
    kc = conv_silu(k_ref, xk, cwk_ref)
    vc = conv_silu(v_ref, xv, cwv_ref)
    gt = gt_ref[...]
    nw = nw_ref[...]
    tril_b = _tril_mask(CHUNK)
    strict_b = _tril_mask(CHUNK, strict=True)

    work = []
    for hh in range(HB):
        ls = slice(hh * HEAD, (hh + 1) * HEAD)
        h = hb * HB + hh
        q_all = _l2norm(qc[:, ls]) * scale
        k_all = _l2norm(kc[:, ls])
        G_all = _pick_lane(gt, h)
        beta = _pick_lane(gt, H + h)
        eG = jnp.exp(G_all)
        kb_all = k_all * beta
        rhs_all = jnp.concatenate([vc[:, ls] * beta, kb_all * eG], axis=1)
        qe_all = q_all * eG
        for ci in range(nchunk):
            rows = slice(ci * CHUNK, (ci + 1) * CHUNK)
            work.append(dict(hh=hh, ls=ls, rows=rows, G=G_all[rows], q=q_all[rows], k=k_all[rows], kb=kb_all[rows],
                             X=rhs_all[rows], qe=qe_all[rows]))
    for w in work:
        G = w["G"]
        w["decay"] = jnp.where(tril_b, jnp.exp(G[:, :CHUNK] - G.T[:CHUNK, :]), 0.0)
        w["KQ"] = _mm(jnp.concatenate([w["kb"], w["q"]], axis=0), w["k"], _NT)
    Rs = _unit_lower_inverses([jnp.where(strict_b, -(w["KQ"][:CHUNK] * w["decay"]), 0.0) for w in work])
    for w, R in zip(work, Rs):
        w["R"] = R
        w["qk"] = jnp.where(tril_b, w["KQ"][CHUNK:] * w["decay"], 0.0)
    for w in work:
        X = w["X"]
        w["X"] = X + _mm(w["R"], X)
        G = w["G"]
        g_last = G[CHUNK - 1:CHUNK, :]
        w["egl"] = jnp.exp(g_last)
        w["rhs2"] = jnp.concatenate([w["qk"], (w["k"] * jnp.exp(g_last - G)).T], axis=0)
    S = [S_scr[hh] for hh in range(HB)]
    for ci in range(nchunk):
        for hh in range(HB):
            w = work[hh * nchunk + ci]
            rows, ls = w["rows"], w["ls"]
            WS = _mm(jnp.concatenate([w["X"][:, HEAD:], w["qe"]], axis=0), S[hh])
            v_new = w["X"][:, :HEAD] - WS[:CHUNK]
            OS = _mm(w["rhs2"], v_new)
            S[hh] = w["egl"] * S[hh] + OS[CHUNK:]
            y_ref[rows, ls] = _gated_rms(WS[CHUNK:] + OS[:CHUNK], nw, z_ref[rows, ls]).astype(y_ref.dtype)
    for hh in range(HB):
        S_scr[hh] = S[hh]

    @pl.when(c == pl.num_programs(2) - 1)
    def _():
        for hh in range(HB):
            s_ref[hh] = S[hh]


def _gd_prompt(proj, gates, B, T, M, H, col0, lp, layer):
    tc = _pick(T, (256, 128, 64))
    nT = T // tc
    HB = HEADS_PER_STEP
    assert H % HB == 0 and col0 % HB == 0
    wb = HB * HEAD
    col = lambda j: pl.BlockSpec((tc, wb), lambda b, h, c: (b * nT + c, (col0 + j * H) // HB + h))
    cw = lambda j: pl.BlockSpec((None, CONV_W, wb), lambda b, h, c: (layer, 0, j * H // HB + h))
    buf = pltpu.VMEM((8 + tc, wb), f32)
    y, S = pl.pallas_call(
        functools.partial(_gd_prompt_body, HEAD ** -0.5, H, HB),
        grid=(B, H // HB, nT),
        in_specs=[col(0), col(1), col(2), col(3),
                  pl.BlockSpec((tc, LANE), lambda b, h, c: (b * nT + c, 0)),
                  cw(0), cw(1), cw(2),
                  pl.BlockSpec((None, 1, HEAD), lambda b, h, c: (layer, 0, 0))],
        out_specs=[pl.BlockSpec((tc, wb), lambda b, h, c: (b * nT + c, h)),
                   pl.BlockSpec((None, HB, HEAD, HEAD), lambda b, h, c: (b, h, 0, 0))],
        out_shape=[jax.ShapeDtypeStruct((M, H * HEAD), bf16), jax.ShapeDtypeStruct((B, H, HEAD, HEAD), f32)],
        scratch_shapes=[buf, buf, buf, pltpu.VMEM((HB, HEAD, HEAD), f32)],
        compiler_params=_params("arbitrary", "arbitrary", "arbitrary"), name="gd_prompt")(
            proj, proj, proj, proj, gates, lp["gd_conv_w"], lp["gd_conv_w"], lp["gd_conv_w"], lp["gd_norm_w"])
    return y, S


def _gd_sample_body(scale, H, q_ref, k_ref, v_ref, z_ref, gt_ref, csq_ref, csk_ref, csv_ref, cwq_ref, cwk_ref, cwv_ref,
                    nw_ref, s_ref, *rest):
    y_ref, so_ref, o_scr = rest[-3:]
    gt = gt_ref[...]

    def head(h, carry):
        ls = _head_lanes(h)

        def conv_silu(x_ref, cs_ref, cw_ref):
            y = cw_ref[CONV_W - 1:CONV_W, ls] * x_ref[:, ls]
            for j in range(CONV_W - 1):
                y = y + cw_ref[j:j + 1, ls] * cs_ref[j, :, ls]
            return _silu(y)

        q = _l2norm(conv_silu(q_ref, csq_ref, cwq_ref)) * scale
        k = _l2norm(conv_silu(k_ref, csk_ref, cwk_ref))
        v = conv_silu(v_ref, csv_ref, cwv_ref)
        eg = jnp.exp(_pick_lane(gt, h))
        beta = _pick_lane(gt, H + h)
        kT = k.T

        def vnew(j, S, kcol):
            egj = eg[j:j + 1, :]
            kS = jnp.sum(kcol * S, axis=0, keepdims=True)
            return egj, beta[j:j + 1, :] * (v[j:j + 1, :] - egj * kS)

        _state_step(s_ref, so_ref, o_scr, h, kT, q, vnew)
        y_ref[:, ls] = _gated_rms(o_scr[...], nw_ref[...], z_ref[:, ls]).astype(y_ref.dtype)
        return carry

    lax.fori_loop(0, H, head, 0)


def _gd_sample(proj, gates, y_all, row0, DB, H, col0, conv_state_t, state, stacked_prev, lp, layer):
    assert row0 % SB == 0 and DB % SB == 0 and col0 % H == 0
    rb = row0 // SB
    wh = H * HEAD
    col = lambda j: pl.BlockSpec((SB, wh), lambda b: (rb + b, col0 // H + j))
    cs = lambda j: pl.BlockSpec((None, CONV_W - 1, SB, wh), lambda b: (layer, 0, b, j))
    cw = lambda j: pl.BlockSpec((None, CONV_W, wh), lambda b: (layer, 0, j))
    st_spec = pl.BlockSpec((None, SB, H, HEAD, HEAD), lambda b: (layer, b, 0, 0, 0))
    s_shape, extra_in, extra_specs, aliases = _state_out(state, stacked_prev, 14)
    y, S = pl.pallas_call(
        functools.partial(_gd_sample_body, HEAD ** -0.5, H),
        grid=(DB // SB,),
        in_specs=[col(0), col(1), col(2), col(3),
                  pl.BlockSpec((SB, LANE), lambda b: (rb + b, 0)),
                  cs(0), cs(1), cs(2), cw(0), cw(1), cw(2),
                  pl.BlockSpec((None, 1, HEAD), lambda b: (layer, 0, 0)),
                  st_spec,
                  pl.BlockSpec(memory_space=pl.ANY)] + extra_specs,
        out_specs=[pl.BlockSpec((SB, wh), lambda b: (rb + b, 0)), st_spec],
        out_shape=[jax.ShapeDtypeStruct(y_all.shape, y_all.dtype), s_shape],
        scratch_shapes=[pltpu.VMEM((SB, HEAD), f32)],
        input_output_aliases={13: 0, **aliases},
        compiler_params=_params("arbitrary"), name="gd_sample")(
            proj, proj, proj, proj, gates, conv_state_t, conv_state_t, conv_state_t,
            lp["gd_conv_w"], lp["gd_conv_w"], lp["gd_conv_w"], lp["gd_norm_w"],
            state, y_all, *extra_in)
    return y, S


def kernel(x_prompt, x_sample, state_rg_h, state_rg_conv, state_hg_S, state_gd_S, state_gd_conv, norm_mix_w, norm_mlp_w, norm_final_w, w_in, rg_conv_w, rg_conv_b, rg_wa, rg_ba, rg_wx, rg_bx, rg_a_param, hg_lb_logits, hg_norm_w, gd_conv_w, gd_A_log, gd_dt_bias, gd_norm_w, w_br_rg, w_br_hg, w_br_gd, w_out, w_up, w_down):
    B, T, D = x_prompt.shape
    DB, DT, _ = x_sample.shape
    assert DT == 1
    depth = w_in.shape[0]
    RW = rg_ba.shape[-1]
    H = gd_A_log.shape[-1]
    PT = B * T
    M = PT + DB
    assert RW % HEAD == 0 and hg_norm_w.shape[-1] == HEAD and gd_norm_w.shape[-1] == HEAD
    HW = H * HEAD
    n_main = 2 * RW + 8 * HW
    merge_col0 = n_main + 2 * H
    assert w_in.shape[-1] == merge_col0 + 3 * D and n_main % 1024 == 0 and n_main % LANE == 0
    hg_col0 = 2 * RW // HEAD
    gd_col0 = hg_col0 + 4 * H

    row3 = lambda a: a.reshape(depth, 1, a.shape[-1])
    lane_pad = lambda a: row3(jnp.pad(a.astype(f32), ((0, 0), (0, LANE - a.shape[-1]))))
    lp = dict(rg_conv_w=rg_conv_w, rg_conv_b=row3(rg_conv_b), rg_wa=rg_wa, rg_wx=rg_wx, rg_ba=row3(rg_ba),
              rg_bx=row3(rg_bx), rg_a_param=row3(rg_a_param), gd_conv_w=gd_conv_w, gd_norm_w=row3(gd_norm_w))
    alog_pad, dtb_pad = lane_pad(gd_A_log), lane_pad(gd_dt_bias)
    hg_nw = row3(hg_norm_w)
    lb = row3(_lower_bounds(hg_lb_logits.astype(f32)))
    rg_conv_t = jnp.swapaxes(state_rg_conv, 1, 2)
    gd_conv_t = jnp.swapaxes(state_gd_conv, 1, 2)

    w_in_t = jnp.swapaxes(w_in, 1, 2)
    x = jnp.concatenate([x_prompt.reshape(PT, D), x_sample.reshape(DB, D)], axis=0)
    u = _rms_norm(x, norm_mix_w[0:1], bf16)

    p_states, s_states = [], []
    y_final = None
    s_hgS = s_gdS = None
    for l in range(depth):
        proj = _gemm_wres(u, w_in_t, l, 0, n_main, 1024, name="in_proj", w_is_nk=True, tm_cands=BIG_TM)
        gates = _gd_gate_proj(u, w_in_t, n_main // LANE, alog_pad, dtb_pad, PT, H, l)

        y_rg, p_h = _rg_prompt(proj, B, T, M, lp, l)
        y_rg, s_h = _rg_sample(proj, y_rg, PT, DB, rg_conv_t, state_rg_h, lp, l)
        y_hg, p_hgS = _hg_prompt(proj, B, T, M, H, hg_col0, lb, hg_nw, l)
        y_hg, s_hgS = _hg_sample(proj, y_hg, PT, DB, H, hg_col0, lb, hg_nw, state_hg_S, s_hgS, l)
        y_gd, p_gdS = _gd_prompt(proj, gates, B, T, M, H, gd_col0, lp, l)
        y_gd, s_gdS = _gd_sample(proj, gates, y_gd, PT, DB, H, gd_col0, gd_conv_t, state_gd_S, s_gdS, lp, l)

        gq = gd_col0 * HEAD
        tail = lambda c0, w: jnp.stack([lax.slice(proj, (b * T + T - (CONV_W - 1), c0), (b * T + T, c0 + w))
                                        for b in range(B)], axis=0)
        last = lambda c0, w: lax.slice(proj, (PT, c0), (M, c0 + w))[:, None, :]
        p_states.append((p_h, tail(0, RW), p_hgS, p_gdS, tail(gq, 3 * HW)))
        s_states.append((s_h,
                         jnp.concatenate([state_rg_conv[l][:, 1:], last(0, RW)], axis=1),
                         None, None,
                         jnp.concatenate([state_gd_conv[l][:, 1:], last(gq, 3 * HW)], axis=1)))

        mixed = _mix(u, y_rg, y_hg, y_gd, w_in_t, w_br_rg, w_br_hg, w_br_gd, l, merge_col0)
        x, hmid = _proj_add_norm(mixed, w_out, l, x, norm_mlp_w[l:l + 1])
        hh = _gemm_wres(hmid, w_up, l, 0, w_up.shape[-1], 1024, epi=lambda a: jnp.square(jnp.maximum(a, 0.0)),
                        out_dtype=bf16, name="mlp_up", tm_cands=BIG_TM)
        x = _gemm_wres(hh, w_down, l, 0, D, 512, name="mlp_down", single_buffer_w=True, res=x)
        if l + 1 < depth:
            u = _rms_norm(x, norm_mix_w[l + 1:l + 2], bf16)
        else:
            y_final = _rms_norm(x, norm_final_w.reshape(1, D), f32)

    def stack(sts, j, like):
        return jnp.stack([s[j] for s in sts], axis=0).astype(like.dtype)

    return (y_final[:PT].reshape(B, T, D), y_final[PT:].reshape(DB, DT, D),
            stack(p_states, 0, state_rg_h), stack(p_states, 1, state_rg_conv), stack(p_states, 2, state_hg_S),
            stack(p_states, 3, state_gd_S), stack(p_states, 4, state_gd_conv),
            stack(s_states, 0, state_rg_h), stack(s_states, 1, state_rg_conv), s_hgS.astype(state_hg_S.dtype),
            s_gdS.astype(state_gd_S.dtype), stack(s_states, 4, state_gd_conv))
```

```python
import functools

import jax
import jax.numpy as jnp
from jax import lax
from jax.experimental import pallas as pl
from jax.experimental.pallas import tpu as pltpu

f32 = jnp.float32
bf16 = jnp.bfloat16

EPS = 1e-6
RG_C = 8.0
HEAD = 128
LANE = 128
SUBLANES = 8
CHUNK = 64
SUB = 8
INV_BLOCK = 16
HEADS_PER_STEP = 8
CONV_W = 4
VMEM_LIMIT = 56 * 1024 * 1024

_NT = (((1,), (1,)), ((), ()))
_TN = (((0,), (0,)), ((), ()))


def _params(*sem):
    return pltpu.CompilerParams(dimension_semantics=sem, vmem_limit_bytes=VMEM_LIMIT)


def _pick(n, cands):
    for c in cands:
        if n % c == 0:
            return c
    raise ValueError(f"no tile for {n} among {cands}")


def _mm(a, b, dims=None):
    a = a.astype(bf16)
    b = b.astype(bf16)
    if dims is None:
        return jnp.dot(a, b, preferred_element_type=f32)
    return lax.dot_general(a, b, dims, preferred_element_type=f32)


def _split3(x):
    hi = x.astype(bf16)
    r = x - hi.astype(f32)
    mid = r.astype(bf16)
    lo = (r - mid.astype(f32)).astype(bf16)
    return hi, mid, lo


def _mm_exact_lhs(a_bf16, x):
    hi, mid, lo = _split3(x)
    return (jnp.dot(a_bf16, hi, preferred_element_type=f32) + jnp.dot(a_bf16, mid, preferred_element_type=f32)
            + jnp.dot(a_bf16, lo, preferred_element_type=f32))


def _mm_hi(a, b):
    ah, am, _ = _split3(a)
    bh, bm, _ = _split3(b)
    d = functools.partial(jnp.dot, preferred_element_type=f32)
    return d(ah, bh) + (d(ah, bm) + d(am, bh))


def _expm1_neg(x, ex):
    return -jnp.tanh(0.5 * x) * (ex + 1.0)


def _softplus(x):
    return jnp.maximum(x, 0.0) + jnp.log1p(jnp.exp(-jnp.abs(x)))


def _silu(x):
    return x * jax.nn.sigmoid(x)


def _gated_rms(o, w, z):
    o = o * lax.rsqrt(jnp.mean(o * o, axis=-1, keepdims=True) + EPS) * w
    return o * _silu(z)


def _l2norm(x):
    return x * lax.rsqrt(jnp.sum(x * x, axis=-1, keepdims=True) + EPS)


def _causal_conv4(xbuf, cw_ref, tc, ls=slice(None)):
    assert CONV_W == 4
    w0, w1, w2, w3 = (cw_ref[j:j + 1, ls] for j in range(CONV_W))
    x0 = xbuf[8:8 + tc, ls]
    row = lax.broadcasted_iota(jnp.int32, x0.shape, 0)
    r2 = pltpu.roll(x0, 2, axis=0)
    xm2 = jnp.concatenate([xbuf[pl.ds(8 - 2, SUBLANES), ls], r2[SUBLANES:]], axis=0)
    even = w3 * x0 + w1 * xm2
    odd = w2 * x0 + w0 * xm2
    odd_before = w2 * xbuf[7:8, ls] + w0 * xbuf[5:6, ls]
    return even + jnp.where(row == 0, odd_before, pltpu.roll(odd, 1, axis=0))


def _tril_mask(n, strict=False):
    r = lax.broadcasted_iota(jnp.int32, (n, n), 0)
    c = lax.broadcasted_iota(jnp.int32, (n, n), 1)
    return (r > c) if strict else (r >= c)


def _chunk_tril(n):
    r = lax.broadcasted_iota(jnp.int32, (n, n), 0)
    c = lax.broadcasted_iota(jnp.int32, (n, n), 1)
    same = (r // CHUNK) == (c // CHUNK)
    return jnp.where(jnp.logical_and(r >= c, same), 1.0, 0.0).astype(bf16)


def _norm_body(x_ref, w_ref, n_ref):
    x = x_ref[...]
    y = x * lax.rsqrt(jnp.mean(x * x, axis=-1, keepdims=True) + EPS)
    n_ref[...] = (y * w_ref[...]).astype(n_ref.dtype)


def _rms_norm(x, w_row, out_dtype):
    M, D = x.shape
    tm = _pick(M, (416, 320, 256, 128, 64, 16))
    row = pl.BlockSpec((tm, D), lambda m: (m, 0))
    return pl.pallas_call(
        _norm_body, grid=(M // tm,), in_specs=[row, pl.BlockSpec((1, D), lambda m: (0, 0))], out_specs=row,
        out_shape=jax.ShapeDtypeStruct((M, D), out_dtype),
        compiler_params=_params("arbitrary"), name="rms_norm")(x, w_row)


def _join_norm_body(n_first, xp_ref, xs_ref, w_ref, x_ref, n_ref):
    x = jnp.where(pl.program_id(0) < n_first, xp_ref[...], xs_ref[...])
    x_ref[...] = x
    y = x * lax.rsqrt(jnp.mean(x * x, axis=-1, keepdims=True) + EPS)
    n_ref[...] = (y * w_ref[...]).astype(n_ref.dtype)


def _join_norm(xp, xs, w_row):
    PT, D = xp.shape
    DB = xs.shape[0]
    tm = _pick(DB, (128, 64, 16))
    assert PT % tm == 0
    n_first = PT // tm
    row = pl.BlockSpec((tm, D), lambda i: (i, 0))
    return pl.pallas_call(
        functools.partial(_join_norm_body, n_first),
        grid=((PT + DB) // tm,),
        in_specs=[pl.BlockSpec((tm, D), lambda i: (jnp.minimum(i, n_first - 1), 0)),
                  pl.BlockSpec((tm, D), lambda i: (jnp.maximum(i - n_first, 0), 0)),
                  pl.BlockSpec((1, D), lambda i: (0, 0))],
        out_specs=[row, row],
        out_shape=[jax.ShapeDtypeStruct((PT + DB, D), f32), jax.ShapeDtypeStruct((PT + DB, D), bf16)],
        compiler_params=_params("arbitrary"), name="join_norm")(xp, xs, w_row)


def _norm_split_body(n_first, x_ref, w_ref, yp_ref, ys_ref):
    i = pl.program_id(0)
    x = x_ref[...]
    y = x * lax.rsqrt(jnp.mean(x * x, axis=-1, keepdims=True) + EPS) * w_ref[...]

    @pl.when(i < n_first)
    def _():
        yp_ref[...] = y

    @pl.when(i >= n_first)
    def _():
        ys_ref[...] = y


def _norm_split(x, w_row, PT):
    M, D = x.shape
    tm = _pick(M - PT, (128, 64, 16))
    assert PT % tm == 0
    n_first = PT // tm
    return pl.pallas_call(
        functools.partial(_norm_split_body, n_first),
        grid=(M // tm,),
        in_specs=[pl.BlockSpec((tm, D), lambda i: (i, 0)), pl.BlockSpec((1, D), lambda i: (0, 0))],
        out_specs=[pl.BlockSpec((tm, D), lambda i: (jnp.minimum(i, n_first - 1), 0)),
                   pl.BlockSpec((tm, D), lambda i: (jnp.maximum(i - n_first, 0), 0))],
        out_shape=[jax.ShapeDtypeStruct((PT, D), f32), jax.ShapeDtypeStruct((M - PT, D), f32)],
        compiler_params=_params("arbitrary"), name="norm_split")(x, w_row)


def _gemm_wres_body(epi, w_is_nk, has_res, a_ref, w_ref, *rest):
    res_ref = rest[0] if has_res else None
    o_ref, wb = rest[-2:]

    @pl.when(pl.program_id(1) == 0)
    def _():
        w = w_ref[...]
        wb[...] = (w.T if w_is_nk else w).astype(bf16)
    acc = jnp.dot(a_ref[...], wb[...], preferred_element_type=f32)
    if epi is not None:
        acc = epi(acc)
    if has_res:
        acc = res_ref[...] + acc
    o_ref[...] = acc.astype(o_ref.dtype)


BIG_TM = (1040, 640, 512, 320, 256, 128, 64, 16)


def _gemm_wres(a, w, layer, col_block0, n_out, tn, epi=None, out_dtype=f32, name="gemm", w_is_nk=False,
               tm_cands=(640, 512, 320, 256, 128, 64, 16), single_buffer_w=False, res=None):
    M, K = a.shape
    tm = _pick(M, tm_cands)
    mode = dict(pipeline_mode=pl.Buffered(1)) if single_buffer_w else {}
    if w_is_nk:
        w_spec = pl.BlockSpec((None, tn, K), lambda n, m: (layer, n + col_block0, 0), **mode)
    else:
        w_spec = pl.BlockSpec((None, K, tn), lambda n, m: (layer, 0, n + col_block0), **mode)
    tile = pl.BlockSpec((tm, tn), lambda n, m: (m, n))
    return pl.pallas_call(
        functools.partial(_gemm_wres_body, epi, w_is_nk, res is not None),
        grid=(n_out // tn, M // tm),
        in_specs=[pl.BlockSpec((tm, K), lambda n, m: (m, 0)), w_spec] + ([tile] if res is not None else []),
        out_specs=tile,
        out_shape=jax.ShapeDtypeStruct((M, n_out), out_dtype),
        scratch_shapes=[pltpu.VMEM((K, tn), bf16)],
        compiler_params=_params("arbitrary", "arbitrary"), name=name)(a, w, *([res] if res is not None else []))


def _proj_norm_body(a_ref, w_ref, x_ref, nw_ref, xo_ref, n_ref, wb):
    @pl.when(pl.program_id(0) == 0)
    def _():
        wb[...] = w_ref[...].astype(bf16)
    x = x_ref[...] + jnp.dot(a_ref[...], wb[...], preferred_element_type=f32)
    xo_ref[...] = x
    y = x * lax.rsqrt(jnp.mean(x * x, axis=-1, keepdims=True) + EPS)
    n_ref[...] = (y * nw_ref[...]).astype(n_ref.dtype)


def _proj_add_norm(a, w, layer, x, nw_row):
    M, K = a.shape
    D = w.shape[-1]
    tm = _pick(M, (320, 256, 128, 64, 16))
    row = lambda width: pl.BlockSpec((tm, width), lambda m: (m, 0))
    return pl.pallas_call(
        _proj_norm_body,
        grid=(M // tm,),
        in_specs=[row(K), pl.BlockSpec((None, K, D), lambda m: (layer, 0, 0), pipeline_mode=pl.Buffered(1)),
                  row(D), pl.BlockSpec((1, D), lambda m: (0, 0))],
        out_specs=[row(D), row(D)],
        out_shape=[jax.ShapeDtypeStruct((M, D), f32), jax.ShapeDtypeStruct((M, D), bf16)],
        scratch_shapes=[pltpu.VMEM((K, D), bf16)],
        compiler_params=_params("arbitrary"), name="out_proj_norm")(a, w, x, nw_row)


MERGE_SHIFT = 16


def _mix_body(u_ref, yr_ref, yh_ref, yg_ref, wm0, wm1, wm2, wx0, wx1, wx2, wr_ref, wh_ref, wg_ref, o_ref, wmb, wbb):
    tn = o_ref.shape[1]

    @pl.when(pl.program_id(1) == 0)
    def _():
        for b, (wm, wx) in enumerate(((wm0, wx0), (wm1, wx1), (wm2, wx2))):
            wcat = jnp.concatenate([wm[...], wx[...]], axis=0)
            wmb[b] = wcat[MERGE_SHIFT:MERGE_SHIFT + tn].T.astype(bf16)
        for b, wr in enumerate((wr_ref, wh_ref, wg_ref)):
            wbb[b] = wr[...].astype(bf16)

    u = u_ref[...]
    acc = None
    for b, y_ref in enumerate((yr_ref, yh_ref, yg_ref)):
        gate = jax.nn.sigmoid(jnp.dot(u, wmb[b], preferred_element_type=f32))
        p = jnp.dot(y_ref[...], wbb[b], preferred_element_type=f32)
        acc = gate * p if acc is None else acc + gate * p
    o_ref[...] = acc.astype(o_ref.dtype)


def _mix(u, y_rg, y_hg, y_gd, w_in_t, w_br_rg, w_br_hg, w_br_gd, layer, merge_col0):
    M, D = u.shape
    W = y_rg.shape[1]
    tn = 512
    tm = _pick(M, (640, 416, 320, 256, 128, 64, 16))
    nt = D // tn
    assert (merge_col0 - MERGE_SHIFT) % tn == 0 and tn % MERGE_SHIFT == 0
    base = (merge_col0 - MERGE_SHIFT) // tn
    r = tn // MERGE_SHIFT
    once = dict(pipeline_mode=pl.Buffered(1))

    def wm_spec(b):
        return pl.BlockSpec((None, tn, D), lambda n, m: (layer, base + b * nt + n, 0), **once)

    def wx_spec(b):
        return pl.BlockSpec((None, MERGE_SHIFT, D), lambda n, m: (layer, (base + b * nt + n + 1) * r, 0), **once)

    row = lambda w: pl.BlockSpec((tm, w), lambda n, m: (m, 0))
    br = pl.BlockSpec((None, W, tn), lambda n, m: (layer, 0, n), **once)
    return pl.pallas_call(
        _mix_body,
        grid=(nt, M // tm),
        in_specs=[row(D), row(W), row(W), row(W), wm_spec(0), wm_spec(1), wm_spec(2),
                  wx_spec(0), wx_spec(1), wx_spec(2), br, br, br],
        out_specs=pl.BlockSpec((tm, tn), lambda n, m: (m, n)),
        out_shape=jax.ShapeDtypeStruct((M, D), bf16),
        scratch_shapes=[pltpu.VMEM((3, D, tn), bf16), pltpu.VMEM((3, W, tn), bf16)],
        compiler_params=_params("arbitrary", "arbitrary"), name="mix")(
            u, y_rg, y_hg, y_gd, w_in_t, w_in_t, w_in_t, w_in_t, w_in_t, w_in_t, w_br_rg, w_br_hg, w_br_gd)


def _lb_body(x_ref, o_ref):
    x = x_ref[...]
    depth = x.shape[0]
    m = jnp.max(x, axis=0, keepdims=True)
    e = jnp.exp(x - m)
    p = e / jnp.sum(e, axis=0, keepdims=True)
    acc = jnp.zeros_like(p[0:1])
    o_ref[0:1, :] = acc
    for l in range(1, depth):
        acc = acc + p[l:l + 1]
        o_ref[l:l + 1, :] = acc


def _lower_bounds(logits):
    return pl.pallas_call(_lb_body, out_shape=jax.ShapeDtypeStruct(logits.shape, f32), name="hg_lower_bounds")(logits)


def _rg_gates(xc, wa, wx, ba, bx, sp):
    xb = xc.astype(bf16)
    r = jax.nn.sigmoid(jnp.dot(xb, wa.astype(bf16), preferred_element_type=f32) + ba)
    i = jax.nn.sigmoid(jnp.dot(xb, wx.astype(bf16), preferred_element_type=f32) + bx)
    log_a = (-RG_C) * r * sp
    a = jnp.exp(log_a)
    m2 = _expm1_neg(2.0 * log_a, a * a)
    mult = jnp.where(m2 > 0.0, m2 * lax.rsqrt(m2), 0.0)
    return a, mult, i


def _rg_prompt_body(x_ref, gate_ref, cw_ref, cb_ref, wa_ref, wx_ref, ba_ref, bx_ref, ap_ref, y_ref, h_ref, xbuf, hprev):
    c = pl.program_id(1)
    tc = x_ref.shape[0]
    nblk = x_ref.shape[1] // HEAD

    @pl.when(c == 0)
    def _():
        xbuf[0:8, :] = jnp.zeros((8, xbuf.shape[1]), f32)
        hprev[...] = jnp.zeros_like(hprev)

    xbuf[8:8 + tc, :] = x_ref[...]
    row = lax.broadcasted_iota(jnp.int32, (tc, HEAD), 0)
    first = jnp.logical_and(row == 0, c == 0)
    for n in range(nblk):
        ls = slice(n * HEAD, (n + 1) * HEAD)
        xc = _causal_conv4(xbuf, cw_ref, tc, ls) + cb_ref[:, ls]
        sp = _softplus(-ap_ref[:, ls])
        a, mult, i = _rg_gates(xc, wa_ref[n], wx_ref[n], ba_ref[:, ls], bx_ref[:, ls], sp)
        mult = jnp.where(first, 1.0, mult)
        b = mult * (i * xc)
        s = 1
        while s < SUBLANES:
            keep = (row % SUBLANES) >= s
            a_sh = jnp.where(keep, pltpu.roll(a, s, axis=0), 1.0)
            b_sh = jnp.where(keep, pltpu.roll(b, s, axis=0), 0.0)
            b = a * b_sh + b
            a = a * a_sh
            s *= 2
        carry = hprev[:, ls]
        groups = []
        for g in range(tc // SUBLANES):
            rows = slice(g * SUBLANES, (g + 1) * SUBLANES)
            hg = b[rows] + a[rows] * carry
            groups.append(hg)
            carry = hg[SUBLANES - 1:SUBLANES, :]
        h = jnp.concatenate(groups, axis=0)
        hprev[:, ls] = carry
        y_ref[:, ls] = (h * jax.nn.gelu(gate_ref[:, ls], approximate=True)).astype(y_ref.dtype)
    xbuf[0:8, :] = xbuf[tc:tc + 8, :]
    h_ref[...] = hprev[...]


def _rg_prompt(proj, B, T, M, lp, layer):
    W = lp["rg_ba"].shape[-1]
    tc = _pick(T, (256, 128, 64))
    nT = T // tc
    nblk = W // HEAD
    vec = pl.BlockSpec((None, 1, W), lambda b, c: (layer, 0, 0))
    blk = pl.BlockSpec((None, nblk, HEAD, HEAD), lambda b, c: (layer, 0, 0, 0))
    y, h = pl.pallas_call(
        _rg_prompt_body,
        grid=(B, nT),
        in_specs=[pl.BlockSpec((tc, W), lambda b, c: (b * nT + c, 0)),
                  pl.BlockSpec((tc, W), lambda b, c: (b * nT + c, 1)),
                  pl.BlockSpec((None, CONV_W, W), lambda b, c: (layer, 0, 0)),
                  vec, blk, blk, vec, vec, vec],
        out_specs=[pl.BlockSpec((tc, W), lambda b, c: (b * nT + c, 0)),
                   pl.BlockSpec((None, 1, W), lambda b, c: (b, 0, 0))],
        out_shape=[jax.ShapeDtypeStruct((M, W), bf16), jax.ShapeDtypeStruct((B, 1, W), f32)],
        scratch_shapes=[pltpu.VMEM((8 + tc, W), f32), pltpu.VMEM((1, W), f32)],
        compiler_params=_params("arbitrary", "arbitrary"), name="rg_prompt")(
            proj, proj, lp["rg_conv_w"], lp["rg_conv_b"], lp["rg_wa"], lp["rg_wx"], lp["rg_ba"], lp["rg_bx"],
            lp["rg_a_param"])
    return y, h[:, 0]


def _rg_sample_body(x_ref, gate_ref, cs_ref, h0_ref, cw_ref, cb_ref, wa_ref, wx_ref, ba_ref, bx_ref, ap_ref,
                    yin_ref, y_ref, h_ref):
    del yin_ref
    nblk = x_ref.shape[1] // HEAD
    for n in range(nblk):
        ls = slice(n * HEAD, (n + 1) * HEAD)
        xc = cb_ref[:, ls] + cw_ref[CONV_W - 1:CONV_W, ls] * x_ref[:, ls]
        for j in range(CONV_W - 1):
            xc = xc + cw_ref[j:j + 1, ls] * cs_ref[j, :, ls]
        sp = _softplus(-ap_ref[:, ls])
        a, mult, i = _rg_gates(xc, wa_ref[n], wx_ref[n], ba_ref[:, ls], bx_ref[:, ls], sp)
        h = a * h0_ref[:, ls] + mult * (i * xc)
        h_ref[:, ls] = h
        y_ref[:, ls] = (h * jax.nn.gelu(gate_ref[:, ls], approximate=True)).astype(y_ref.dtype)


def _rg_sample(proj, y_all, row0, DB, conv_state_t, h0, lp, layer):
    W = lp["rg_ba"].shape[-1]
    nblk = W // HEAD
    assert row0 % DB == 0
    rb = row0 // DB
    vec = pl.BlockSpec((None, 1, W), lambda i: (layer, 0, 0))
    blk = pl.BlockSpec((None, nblk, HEAD, HEAD), lambda i: (layer, 0, 0, 0))
    y, h = pl.pallas_call(
        _rg_sample_body,
        grid=(1,),
        in_specs=[pl.BlockSpec((DB, W), lambda i: (rb, 0)),
                  pl.BlockSpec((DB, W), lambda i: (rb, 1)),
                  pl.BlockSpec((None, CONV_W - 1, DB, W), lambda i: (layer, 0, 0, 0)),
                  pl.BlockSpec((None, DB, W), lambda i: (layer, 0, 0)),
                  pl.BlockSpec((None, CONV_W, W), lambda i: (layer, 0, 0)),
                  vec, blk, blk, vec, vec, vec,
                  pl.BlockSpec(memory_space=pl.ANY)],
        out_specs=[pl.BlockSpec((DB, W), lambda i: (rb, 0)),
                   pl.BlockSpec((DB, W), lambda i: (0, 0))],
        out_shape=[jax.ShapeDtypeStruct(y_all.shape, y_all.dtype), jax.ShapeDtypeStruct((DB, W), f32)],
        input_output_aliases={11: 0},
        compiler_params=_params("arbitrary"), name="rg_sample")(
            proj, proj, conv_state_t, h0, lp["rg_conv_w"], lp["rg_conv_b"], lp["rg_wa"], lp["rg_wx"],
            lp["rg_ba"], lp["rg_bx"], lp["rg_a_param"], y_all)
    return y, h


def _hg_gates(fx, lb):
    f = lb + (1.0 - lb) * jax.nn.sigmoid(fx)
    k = (1.0 - lb) * jax.nn.sigmoid(-fx)
    return f, k


def _hg_intra_diag(G, q, k):
    lane = lax.broadcasted_iota(jnp.int32, (SUB, CHUNK), 1)
    blocks = []
    for i in range(CHUNK // SUB):
        sl = slice(i * SUB, (i + 1) * SUB)
        g_i, q_i, k_i = G[sl], q[sl], k[sl]
        a_d = jnp.zeros((SUB, CHUNK), f32)
        for s in range(SUB):
            e = jnp.exp(g_i - g_i[s:s + 1, :])
            col = jnp.sum(q_i * k_i[s:s + 1, :] * e, axis=-1, keepdims=True)
            a_d = jnp.where(lane == i * SUB + s, col, a_d)
        blocks.append(a_d)
    return jnp.where(_tril_mask(CHUNK), jnp.concatenate(blocks, axis=0), 0.0)


def _hg_intra_off(G, q, k):
    nsub = CHUNK // SUB
    row = lax.broadcasted_iota(jnp.int32, (CHUNK, HEAD), 0)
    q_parts, k_parts = [], []
    for j in range(nsub - 1):
        g_e = G[(j + 1) * SUB - 1:(j + 1) * SUB, :]
        q_parts.append(jnp.where(row >= (j + 1) * SUB, q * jnp.exp(G - g_e), 0.0))
        in_j = jnp.logical_and(row >= j * SUB, row < (j + 1) * SUB)
        k_parts.append(jnp.where(in_j, k * jnp.exp(g_e - G), 0.0))
    return _mm(jnp.concatenate(q_parts, axis=1), jnp.concatenate(k_parts, axis=1), _NT)


def _hg_prompt_body(scale, HB, q_ref, f_ref, i_ref, g_ref, lb_ref, nw_ref, y_ref, s_ref, S_scr):
    c = pl.program_id(2)
    nchunk = q_ref.shape[0] // CHUNK

    @pl.when(c == 0)
    def _():
        S_scr[...] = jnp.zeros_like(S_scr)

    nw = nw_ref[...]
    tril = _chunk_tril(nchunk * CHUNK)
    work = []
    for hh in range(HB):
        ls = slice(hh * HEAD, (hh + 1) * HEAD)
        f_all, k_all = _hg_gates(f_ref[:, ls], lb_ref[:, ls])
        q_all = q_ref[:, ls] * scale
        G_all = _mm_exact_lhs(tril, jnp.log(f_all))
        for ci in range(nchunk):
            rows = slice(ci * CHUNK, (ci + 1) * CHUNK)
            work.append(dict(hh=hh, ls=ls, rows=rows, G=G_all[rows], q=q_all[rows], k=k_all[rows]))
    for w in work:
        w["A"] = _hg_intra_diag(w["G"], w["q"], w["k"])
    for w in work:
        w["A"] = w["A"] + _hg_intra_off(w["G"], w["q"], w["k"])
    for w in work:
        G = w["G"]
        kT, GT = w["k"].T, G.T
        g_last = GT[:, CHUNK - 1:CHUNK]
        w["dec"] = jnp.exp(g_last)
        w["upd"] = _mm(kT * jnp.exp(g_last - GT), i_ref[w["rows"], w["ls"]])
        w["lhs"] = jnp.concatenate([w["A"], w["q"] * jnp.exp(G)], axis=1)
    S = [S_scr[hh] for hh in range(HB)]
    for ci in range(nchunk):
        for hh in range(HB):
            w = work[hh * nchunk + ci]
            rows, ls = w["rows"], w["ls"]
            o = _mm(w["lhs"], jnp.concatenate([i_ref[rows, ls], S[hh]], axis=0))
            S[hh] = S[hh] * w["dec"] + w["upd"]
            y_ref[rows, ls] = _gated_rms(o, nw, g_ref[rows, ls]).astype(y_ref.dtype)
    for hh in range(HB):
        S_scr[hh] = S[hh]

    @pl.when(c == pl.num_programs(2) - 1)
    def _():
        for hh in range(HB):
            s_ref[hh] = S[hh]


def _hg_prompt(proj, B, T, M, H, col0, lb, norm_w, layer):
    tc = _pick(T, (256, 128, 64))
    nT = T // tc
    HB = HEADS_PER_STEP
    assert H % HB == 0 and col0 % HB == 0
    wb = HB * HEAD
    col = lambda j: pl.BlockSpec((tc, wb), lambda b, h, c: (b * nT + c, (col0 + j * H) // HB + h))
    y, S = pl.pallas_call(
        functools.partial(_hg_prompt_body, HEAD ** -0.5, HB),
        grid=(B, H // HB, nT),
        in_specs=[col(0), col(1), col(2), col(3),
                  pl.BlockSpec((None, 1, wb), lambda b, h, c: (layer, 0, h)),
                  pl.BlockSpec((None, 1, HEAD), lambda b, h, c: (layer, 0, 0))],
        out_specs=[pl.BlockSpec((tc, wb), lambda b, h, c: (b * nT + c, h)),
                   pl.BlockSpec((None, HB, HEAD, HEAD), lambda b, h, c: (b, h, 0, 0))],
        out_shape=[jax.ShapeDtypeStruct((M, H * HEAD), bf16), jax.ShapeDtypeStruct((B, H, HEAD, HEAD), f32)],
        scratch_shapes=[pltpu.VMEM((HB, HEAD, HEAD), f32)],
        compiler_params=_params("arbitrary", "arbitrary", "arbitrary"), name="hg_prompt")(
            proj, proj, proj, proj, lb, norm_w)
    return y, S


SB = 16


def _state_step(s_ref, so_ref, o_scr, h, kT, q, vnew_fn):
    for j in range(SB):
        S = s_ref[j, h]
        kcol = kT[:, j:j + 1]
        d, vnew = vnew_fn(j, S, kcol)
        so_ref[j, h] = d * S + kcol * vnew
    qb = q.astype(bf16)
    for j in range(SB):
        o_scr[h, j:j + 1, :] = jnp.dot(qb, so_ref[j, h].astype(bf16), preferred_element_type=f32)[j:j + 1, :]


def _hg_sample_body(scale, H, q_ref, f_ref, i_ref, g_ref, lb_ref, nw_ref, s_ref, *rest):
    y_ref, so_ref, o_scr = rest[-3:]

    def head(h, carry):
        ls = pl.ds(pl.multiple_of(h * HEAD, HEAD), HEAD)
        f, k = _hg_gates(f_ref[:, ls], lb_ref[:, ls])
        q = q_ref[:, ls] * scale
        v = i_ref[:, ls]
        fT, kT = f.T, k.T

        def vnew(j, S, kcol):
            return fT[:, j:j + 1], v[j:j + 1, :]

        _state_step(s_ref, so_ref, o_scr, h, kT, q, vnew)
        y_ref[:, ls] = _gated_rms(o_scr[h], nw_ref[...], g_ref[:, ls]).astype(y_ref.dtype)
        return carry

    lax.fori_loop(0, H, head, 0)


def _state_out(state, stacked_prev, n_in):
    extra_in, extra_specs, aliases = [], [], {}
    if stacked_prev is not None:
        extra_in, extra_specs, aliases = [stacked_prev], [pl.BlockSpec(memory_space=pl.ANY)], {n_in: 1}
    return jax.ShapeDtypeStruct(state.shape, f32), extra_in, extra_specs, aliases


def _hg_sample(proj, y_all, row0, DB, H, col0, lb, norm_w, state, stacked_prev, layer):
    assert row0 % SB == 0 and DB % SB == 0 and col0 % H == 0
    rb = row0 // SB
    wh = H * HEAD
    col = lambda j: pl.BlockSpec((SB, wh), lambda b: (rb + b, col0 // H + j))
    st_spec = pl.BlockSpec((None, SB, H, HEAD, HEAD), lambda b: (layer, b, 0, 0, 0))
    s_shape, extra_in, extra_specs, aliases = _state_out(state, stacked_prev, 8)
    y, S = pl.pallas_call(
        functools.partial(_hg_sample_body, HEAD ** -0.5, H),
        grid=(DB // SB,),
        in_specs=[col(0), col(1), col(2), col(3),
                  pl.BlockSpec((None, 1, wh), lambda b: (layer, 0, 0)),
                  pl.BlockSpec((None, 1, HEAD), lambda b: (layer, 0, 0)),
                  st_spec,
                  pl.BlockSpec(memory_space=pl.ANY)] + extra_specs,
        out_specs=[pl.BlockSpec((SB, wh), lambda b: (rb + b, 0)), st_spec],
        out_shape=[jax.ShapeDtypeStruct(y_all.shape, y_all.dtype), s_shape],
        scratch_shapes=[pltpu.VMEM((H, SB, HEAD), f32)],
        input_output_aliases={7: 0, **aliases},
        compiler_params=_params("arbitrary"), name="hg_sample")(
            proj, proj, proj, proj, lb, norm_w, state, y_all, *extra_in)
    return y, S


def _pick_lane(x, idx):
    lane = lax.broadcasted_iota(jnp.int32, x.shape, 1)
    col = jnp.sum(jnp.where(lane == idx, x, 0.0), axis=1, keepdims=True)
    return jnp.broadcast_to(col, (x.shape[0], HEAD))


def _gd_gate_body(H, PT, u_ref, w_ref, alog_ref, dtb_ref, o_ref, wb):
    i = pl.program_id(0)
    tm = u_ref.shape[0]

    @pl.when(i == 0)
    def _():
        wb[...] = w_ref[...].T.astype(bf16)

    x = jnp.dot(u_ref[...], wb[...], preferred_element_type=f32)
    lane = lax.broadcasted_iota(jnp.int32, x.shape, 1)
    row = lax.broadcasted_iota(jnp.int32, x.shape, 0) + i * tm
    g = jnp.where(lane < H, -jnp.exp(alog_ref[...]) * _softplus(x + dtb_ref[...]), 0.0)
    blk = min(2 * CHUNK, tm)
    tril = _chunk_tril(blk)
    G = jnp.concatenate([_mm_exact_lhs(tril, g[r:r + blk]) for r in range(0, tm, blk)], axis=0)
    G = jnp.where(row < PT, G, g)
    o_ref[...] = jnp.where(lane < H, G, jax.nn.sigmoid(x))


def _gd_gate_proj(u, w_in_t, row_block, alog_pad, dtb_pad, PT, H, layer):
    M, D = u.shape
    tm = _pick(M, (640, 320, 128, 64))
    assert PT % CHUNK == 0 and tm % CHUNK == 0
    vec = pl.BlockSpec((None, 1, LANE), lambda i: (layer, 0, 0))
    return pl.pallas_call(
        functools.partial(_gd_gate_body, H, PT),
        grid=(M // tm,),
        in_specs=[pl.BlockSpec((tm, D), lambda i: (i, 0)),
                  pl.BlockSpec((None, LANE, D), lambda i: (layer, row_block, 0)), vec, vec],
        out_specs=pl.BlockSpec((tm, LANE), lambda i: (i, 0)),
        out_shape=jax.ShapeDtypeStruct((M, LANE), f32),
        scratch_shapes=[pltpu.VMEM((D, LANE), bf16)],
        compiler_params=_params("arbitrary"), name="gd_gate_proj")(u, w_in_t, alog_pad, dtb_pad)


def _unit_lower_inverses(Ns):
    r = lax.broadcasted_iota(jnp.int32, (CHUNK, CHUNK), 0)
    c = lax.broadcasted_iota(jnp.int32, (CHUNK, CHUNK), 1)
    same = lambda n: (r // n) == (c // n)
    assert CHUNK == 4 * INV_BLOCK
    dot = functools.partial(jnp.dot, preferred_element_type=f32)
    cast = lambda xs: [x.astype(bf16) for x in xs]
    Rs = [jnp.where(same(INV_BLOCK), N, 0.0) for N in Ns]
    Rb = cast(Rs)
    Ps = [dot(rb, rb) for rb in Rb]
    p = 2
    while p < INV_BLOCK:
        Pb = cast(Ps)
        if 2 * p < INV_BLOCK:
            PMs = [dot(pb, jnp.concatenate([rb, pb], axis=1)) for rb, pb in zip(Rb, Pb)]
            Rs = [R + P + PM[:, :CHUNK] for R, P, PM in zip(Rs, Ps, PMs)]
            Ps = [PM[:, CHUNK:] for PM in PMs]
            Rb = cast(Rs)
        else:
            PRs = [dot(pb, rb) for rb, pb in zip(Rb, Pb)]
            Rs = [R + P + PR for R, P, PR in zip(Rs, Ps, PRs)]
        p *= 2
    eye = jnp.where(r == c, 1.0, 0.0)
    Nb = cast(Ns)
    for n in (2 * INV_BLOCK, 4 * INV_BLOCK):
        off = jnp.logical_and(same(n), jnp.logical_not(same(n // 2)))
        Db = cast([eye + R for R in Rs])
        DCs = [dot(db, jnp.where(off, nb, jnp.zeros_like(nb))) for db, nb in zip(Db, Nb)]
        DCDs = [dot(dc, db) for dc, db in zip(cast(DCs), Db)]
        Rs = [R + DCD for R, DCD in zip(Rs, DCDs)]
    return Rs


def _gd_prompt_body(scale, H, HB, q_ref, k_ref, v_ref, z_ref, gt_ref, cwq_ref, cwk_ref, cwv_ref, nw_ref,
                    y_ref, s_ref, xq, xk, xv, S_scr):
    hb = pl.program_id(1)
    c = pl.program_id(2)
    tc = q_ref.shape[0]
    nchunk = tc // CHUNK

    @pl.when(c == 0)
    def _():
        for xb in (xq, xk, xv):
            xb[0:8, :] = jnp.zeros((8, xb.shape[1]), f32)
        S_scr[...] = jnp.zeros_like(S_scr)

    def conv_silu(x_ref, xb, cw_ref):
        xb[8:8 + tc, :] = x_ref[...]
        y = _causal_conv4(xb, cw_ref, tc)
        xb[0:8, :] = xb[tc:tc + 8, :]
        return _silu(y)

    qc = conv_silu(q_ref, xq, cwq_ref)
    kc = conv_silu(k_ref, xk, cwk_ref)
    vc = conv_silu(v_ref, xv, cwv_ref)
    gt = gt_ref[...]
    nw = nw_ref[...]
    tril_b = _tril_mask(CHUNK)
    strict_b = _tril_mask(CHUNK, strict=True)

    work = []
    for hh in range(HB):
        ls = slice(hh * HEAD, (hh + 1) * HEAD)
        h = hb * HB + hh
        q_all = _l2norm(qc[:, ls]) * scale
        k_all = _l2norm(kc[:, ls])
        G_all = _pick_lane(gt, h)
        beta = _pick_lane(gt, H + h)
        eG = jnp.exp(G_all)
        kb_all = k_all * beta
        rhs_all = jnp.concatenate([vc[:, ls] * beta, kb_all * eG], axis=1)
        qe_all = q_all * eG
        for ci in range(nchunk):
            rows = slice(ci * CHUNK, (ci + 1) * CHUNK)
            work.append(dict(hh=hh, ls=ls, rows=rows, G=G_all[rows], q=q_all[rows], k=k_all[rows], kb=kb_all[rows],
                             X=rhs_all[rows], qe=qe_all[rows]))
    for w in work:
        G = w["G"]
        w["decay"] = jnp.where(tril_b, jnp.exp(G[:, :CHUNK] - G.T[:CHUNK, :]), 0.0)
        w["KQ"] = _mm(jnp.concatenate([w["kb"], w["q"]], axis=0), w["k"], _NT)
    Rs = _unit_lower_inverses([jnp.where(strict_b, -(w["KQ"][:CHUNK] * w["decay"]), 0.0) for w in work])
    for w, R in zip(work, Rs):
        w["R"] = R
        w["qk"] = jnp.where(tril_b, w["KQ"][CHUNK:] * w["decay"], 0.0)
    for w in work:
        X = w["X"]
        w["X"] = X + _mm(w["R"], X)
        G = w["G"]
        g_last = G[CHUNK - 1:CHUNK, :]
        w["egl"] = jnp.exp(g_last)
        w["rhs2"] = jnp.concatenate([w["qk"], (w["k"] * jnp.exp(g_last - G)).T], axis=0)
    S = [S_scr[hh] for hh in range(HB)]
    for ci in range(nchunk):
        for hh in range(HB):
            w = work[hh * nchunk + ci]
            rows, ls = w["rows"], w["ls"]
            WS = _mm(jnp.concatenate([w["X"][:, HEAD:], w["qe"]], axis=0), S[hh])
            v_new = w["X"][:, :HEAD] - WS[:CHUNK]
            OS = _mm(w["rhs2"], v_new)
            S[hh] = w["egl"] * S[hh] + OS[CHUNK:]
            y_ref[rows, ls] = _gated_rms(WS[CHUNK:] + OS[:CHUNK], nw, z_ref[rows, ls]).astype(y_ref.dtype)
    for hh in range(HB):
        S_scr[hh] = S[hh]

    @pl.when(c == pl.num_programs(2) - 1)
    def _():
        for hh in range(HB):
            s_ref[hh] = S[hh]


def _gd_prompt(proj, gates, B, T, M, H, col0, lp, layer):
    tc = _pick(T, (256, 128, 64))
    nT = T // tc
    HB = HEADS_PER_STEP
    assert H % HB == 0 and col0 % HB == 0
    wb = HB * HEAD
    col = lambda j: pl.BlockSpec((tc, wb), lambda b, h, c: (b * nT + c, (col0 + j * H) // HB + h))
    cw = lambda j: pl.BlockSpec((None, CONV_W, wb), lambda b, h, c: (layer, 0, j * H // HB + h))
    buf = pltpu.VMEM((8 + tc, wb), f32)
    y, S = pl.pallas_call(
        functools.partial(_gd_prompt_body, HEAD ** -0.5, H, HB),
        grid=(B, H // HB, nT),
        in_specs=[col(0), col(1), col(2), col(3),
                  pl.BlockSpec((tc, LANE), lambda b, h, c: (b * nT + c, 0)),
                  cw(0), cw(1), cw(2),
                  pl.BlockSpec((None, 1, HEAD), lambda b, h, c: (layer, 0, 0))],
        out_specs=[pl.BlockSpec((tc, wb), lambda b, h, c: (b * nT + c, h)),
                   pl.BlockSpec((None, HB, HEAD, HEAD), lambda b, h, c: (b, h, 0, 0))],
        out_shape=[jax.ShapeDtypeStruct((M, H * HEAD), bf16), jax.ShapeDtypeStruct((B, H, HEAD, HEAD), f32)],
        scratch_shapes=[buf, buf, buf, pltpu.VMEM((HB, HEAD, HEAD), f32)],
        compiler_params=_params("arbitrary", "arbitrary", "arbitrary"), name="gd_prompt")(
            proj, proj, proj, proj, gates, lp["gd_conv_w"], lp["gd_conv_w"], lp["gd_conv_w"], lp["gd_norm_w"])
    return y, S


def _gd_sample_body(scale, H, q_ref, k_ref, v_ref, z_ref, gt_ref, csq_ref, csk_ref, csv_ref, cwq_ref, cwk_ref, cwv_ref,
                    nw_ref, s_ref, *rest):
    y_ref, so_ref, o_scr = rest[-3:]
    gt = gt_ref[...]

    def head(h):
        ls = slice(h * HEAD, (h + 1) * HEAD)

        def conv_silu(x_ref, cs_ref, cw_ref):
            y = cw_ref[CONV_W - 1:CONV_W, ls] * x_ref[:, ls]
            for j in range(CONV_W - 1):
                y = y + cw_ref[j:j + 1, ls] * cs_ref[j, :, ls]
            return _silu(y)

        q = _l2norm(conv_silu(q_ref, csq_ref, cwq_ref)) * scale
        k = _l2norm(conv_silu(k_ref, csk_ref, cwk_ref))
        v = conv_silu(v_ref, csv_ref, cwv_ref)
        eg = jnp.exp(_pick_lane(gt, h))
        beta = _pick_lane(gt, H + h)
        kT = k.T

        def vnew(j, S, kcol):
            egj = eg[j:j + 1, :]
            kS = jnp.sum(kcol * S, axis=0, keepdims=True)
            return egj, beta[j:j + 1, :] * (v[j:j + 1, :] - egj * kS)

        _state_step(s_ref, so_ref, o_scr, h, kT, q, vnew)
        y_ref[:, ls] = _gated_rms(o_scr[h], nw_ref[...], z_ref[:, ls]).astype(y_ref.dtype)
    for h in range(H):
        head(h)


def _gd_sample(proj, gates, y_all, row0, DB, H, col0, conv_state_t, state, stacked_prev, lp, layer):
    assert row0 % SB == 0 and DB % SB == 0 and col0 % H == 0
    rb = row0 // SB
    wh = H * HEAD
    col = lambda j: pl.BlockSpec((SB, wh), lambda b: (rb + b, col0 // H + j))
    cs = lambda j: pl.BlockSpec((None, CONV_W - 1, SB, wh), lambda b: (layer, 0, b, j))
    cw = lambda j: pl.BlockSpec((None, CONV_W, wh), lambda b: (layer, 0, j))
    st_spec = pl.BlockSpec((None, SB, H, HEAD, HEAD), lambda b: (layer, b, 0, 0, 0))
    s_shape, extra_in, extra_specs, aliases = _state_out(state, stacked_prev, 14)
    y, S = pl.pallas_call(
        functools.partial(_gd_sample_body, HEAD ** -0.5, H),
        grid=(DB // SB,),
        in_specs=[col(0), col(1), col(2), col(3),
                  pl.BlockSpec((SB, LANE), lambda b: (rb + b, 0)),
                  cs(0), cs(1), cs(2), cw(0), cw(1), cw(2),
                  pl.BlockSpec((None, 1, HEAD), lambda b: (layer, 0, 0)),
                  st_spec,
                  pl.BlockSpec(memory_space=pl.ANY)] + extra_specs,
        out_specs=[pl.BlockSpec((SB, wh), lambda b: (rb + b, 0)), st_spec],
        out_shape=[jax.ShapeDtypeStruct(y_all.shape, y_all.dtype), s_shape],
        scratch_shapes=[pltpu.VMEM((H, SB, HEAD), f32)],
        input_output_aliases={13: 0, **aliases},
        compiler_params=_params("arbitrary"), name="gd_sample")(
            proj, proj, proj, proj, gates, conv_state_t, conv_state_t, conv_state_t,
            lp["gd_conv_w"], lp["gd_conv_w"], lp["gd_conv_w"], lp["gd_norm_w"],
            state, y_all, *extra_in)
    return y, S


def kernel(x_prompt, x_sample, state_rg_h, state_rg_conv, state_hg_S, state_gd_S, state_gd_conv, norm_mix_w, norm_mlp_w, norm_final_w, w_in, rg_conv_w, rg_conv_b, rg_wa, rg_ba, rg_wx, rg_bx, rg_a_param, hg_lb_logits, hg_norm_w, gd_conv_w, gd_A_log, gd_dt_bias, gd_norm_w, w_br_rg, w_br_hg, w_br_gd, w_out, w_up, w_down):
    B, T, D = x_prompt.shape
    DB, DT, _ = x_sample.shape
    assert DT == 1
    depth = w_in.shape[0]
    RW = rg_ba.shape[-1]
    H = gd_A_log.shape[-1]
    PT = B * T
    M = PT + DB
    assert RW % HEAD == 0 and hg_norm_w.shape[-1] == HEAD and gd_norm_w.shape[-1] == HEAD
    HW = H * HEAD
    n_main = 2 * RW + 8 * HW
    merge_col0 = n_main + 2 * H
    assert w_in.shape[-1] == merge_col0 + 3 * D and n_main % 1024 == 0 and n_main % LANE == 0
    hg_col0 = 2 * RW // HEAD
    gd_col0 = hg_col0 + 4 * H

    row3 = lambda a: a.reshape(depth, 1, a.shape[-1])
    lane_pad = lambda a: row3(jnp.pad(a.astype(f32), ((0, 0), (0, LANE - a.shape[-1]))))
    lp = dict(rg_conv_w=rg_conv_w, rg_conv_b=row3(rg_conv_b), rg_wa=rg_wa, rg_wx=rg_wx, rg_ba=row3(rg_ba),
              rg_bx=row3(rg_bx), rg_a_param=row3(rg_a_param), gd_conv_w=gd_conv_w, gd_norm_w=row3(gd_norm_w))
    alog_pad, dtb_pad = lane_pad(gd_A_log), lane_pad(gd_dt_bias)
    hg_nw = row3(hg_norm_w)
    lb = row3(_lower_bounds(hg_lb_logits.astype(f32)))
    rg_conv_t = jnp.swapaxes(state_rg_conv, 1, 2)
    gd_conv_t = jnp.swapaxes(state_gd_conv, 1, 2)

    w_in_t = jnp.swapaxes(w_in, 1, 2)
    x, u = _join_norm(x_prompt.reshape(PT, D), x_sample.reshape(DB, D), norm_mix_w[0:1])

    p_states, s_states = [], []
    y_prompt = y_sample = None
    s_hgS = s_gdS = None
    for l in range(depth):
        proj = _gemm_wres(u, w_in_t, l, 0, n_main, 1024, name="in_proj", w_is_nk=True, tm_cands=BIG_TM)
        gates = _gd_gate_proj(u, w_in_t, n_main // LANE, alog_pad, dtb_pad, PT, H, l)

        y_rg, p_h = _rg_prompt(proj, B, T, M, lp, l)
        y_rg, s_h = _rg_sample(proj, y_rg, PT, DB, rg_conv_t, state_rg_h, lp, l)
        y_hg, p_hgS = _hg_prompt(proj, B, T, M, H, hg_col0, lb, hg_nw, l)
        y_hg, s_hgS = _hg_sample(proj, y_hg, PT, DB, H, hg_col0, lb, hg_nw, state_hg_S, s_hgS, l)
        y_gd, p_gdS = _gd_prompt(proj, gates, B, T, M, H, gd_col0, lp, l)
        y_gd, s_gdS = _gd_sample(proj, gates, y_gd, PT, DB, H, gd_col0, gd_conv_t, state_gd_S, s_gdS, lp, l)

        gq = gd_col0 * HEAD
        tail = lambda c0, w: jnp.stack([lax.slice(proj, (b * T + T - (CONV_W - 1), c0), (b * T + T, c0 + w))
                                        for b in range(B)], axis=0)
        last = lambda c0, w: lax.slice(proj, (PT, c0), (M, c0 + w))[:, None, :]
        p_states.append((p_h, tail(0, RW), p_hgS, p_gdS, tail(gq, 3 * HW)))
        s_states.append((s_h,
                         jnp.concatenate([state_rg_conv[l][:, 1:], last(0, RW)], axis=1),
                         None, None,
                         jnp.concatenate([state_gd_conv[l][:, 1:], last(gq, 3 * HW)], axis=1)))

        mixed = _mix(u, y_rg, y_hg, y_gd, w_in_t, w_br_rg, w_br_hg, w_br_gd, l, merge_col0)
        x, hmid = _proj_add_norm(mixed, w_out, l, x, norm_mlp_w[l:l + 1])
        hh = _gemm_wres(hmid, w_up, l, 0, w_up.shape[-1], 1024, epi=lambda a: jnp.square(jnp.maximum(a, 0.0)),
                        out_dtype=bf16, name="mlp_up", tm_cands=BIG_TM)
        x = _gemm_wres(hh, w_down, l, 0, D, 512, name="mlp_down", single_buffer_w=True, res=x)
        if l + 1 < depth:
            u = _rms_norm(x, norm_mix_w[l + 1:l + 2], bf16)
        else:
            y_prompt, y_sample = _norm_split(x, norm_final_w.reshape(1, D), PT)

    def stack(sts, j, like):
        return jnp.stack([s[j] for s in sts], axis=0).astype(like.dtype)

    return (y_prompt.reshape(B, T, D), y_sample.reshape(DB, DT, D),
            stack(p_states, 0, state_rg_h), stack(p_states, 1, state_rg_conv), stack(p_states, 2, state_hg_S),
            stack(p_states, 3, state_gd_S), stack(p_states, 4, state_gd_conv),
            stack(s_states, 0, state_rg_h), stack(s_states, 1, state_rg_conv), s_hgS.astype(state_hg_S.dtype),
            s_gdS.astype(state_gd_S.dtype), stack(s_states, 4, state_gd_conv))
```

```python
import functools

import jax
import jax.numpy as jnp
from jax import lax
from jax.experimental import pallas as pl
from jax.experimental.pallas import tpu as pltpu

f32 = jnp.float32
bf16 = jnp.bfloat16

EPS = 1e-6
RG_C = 8.0
HEAD = 128
LANE = 128
SUBLANES = 8
HIST = SUBLANES
CHUNK = 64
SUB = 8
INV_BLOCK = 16
HEADS_PER_STEP = 8
CONV_W = 4
VMEM_LIMIT = 56 * 1024 * 1024

_NT = (((1,), (1,)), ((), ()))


def _params(*sem):
    return pltpu.CompilerParams(dimension_semantics=sem, vmem_limit_bytes=VMEM_LIMIT)


def _pick(n, cands):
    for c in cands:
        if n % c == 0:
            return c
    raise ValueError(f"no tile for {n} among {cands}")


def _mm(a, b, dims=None):
    a = a.astype(bf16)
    b = b.astype(bf16)
    if dims is None:
        return jnp.dot(a, b, preferred_element_type=f32)
    return lax.dot_general(a, b, dims, preferred_element_type=f32)


def _split3(x):
    hi = x.astype(bf16)
    r = x - hi.astype(f32)
    mid = r.astype(bf16)
    lo = (r - mid.astype(f32)).astype(bf16)
    return hi, mid, lo


def _mm_exact_lhs(a_bf16, x):
    hi, mid, lo = _split3(x)
    return (jnp.dot(a_bf16, hi, preferred_element_type=f32) + jnp.dot(a_bf16, mid, preferred_element_type=f32)
            + jnp.dot(a_bf16, lo, preferred_element_type=f32))


def _expm1_neg(x, ex):
    return -jnp.tanh(0.5 * x) * (ex + 1.0)


def _softplus(x):
    return jnp.maximum(x, 0.0) + jnp.log1p(jnp.exp(-jnp.abs(x)))


def _silu(x):
    return x * jax.nn.sigmoid(x)


def _gated_rms(o, w, z):
    o = o * lax.rsqrt(jnp.mean(o * o, axis=-1, keepdims=True) + EPS) * w
    return o * _silu(z)


def _l2norm(x):
    return x * lax.rsqrt(jnp.sum(x * x, axis=-1, keepdims=True) + EPS)


def _causal_conv4(xbuf, cw_ref, tc, ls=slice(None)):
    assert CONV_W == 4
    w0, w1, w2, w3 = (cw_ref[j:j + 1, ls] for j in range(CONV_W))
    x0 = xbuf[HIST:HIST + tc, ls]
    row = lax.broadcasted_iota(jnp.int32, x0.shape, 0)
    r2 = pltpu.roll(x0, 2, axis=0)
    xm2 = jnp.concatenate([xbuf[pl.ds(HIST - 2, SUBLANES), ls], r2[SUBLANES:]], axis=0)
    even = w3 * x0 + w1 * xm2
    odd = w2 * x0 + w0 * xm2
    odd_before = w2 * xbuf[HIST - 1:HIST, ls] + w0 * xbuf[HIST - 3:HIST - 2, ls]
    return even + jnp.where(row == 0, odd_before, pltpu.roll(odd, 1, axis=0))


def _tril_mask(n, strict=False):
    r = lax.broadcasted_iota(jnp.int32, (n, n), 0)
    c = lax.broadcasted_iota(jnp.int32, (n, n), 1)
    return (r > c) if strict else (r >= c)


def _chunk_tril(n):
    r = lax.broadcasted_iota(jnp.int32, (n, n), 0)
    c = lax.broadcasted_iota(jnp.int32, (n, n), 1)
    same = (r // CHUNK) == (c // CHUNK)
    return jnp.where(jnp.logical_and(r >= c, same), 1.0, 0.0).astype(bf16)


def _norm_body(x_ref, w_ref, n_ref):
    x = x_ref[...]
    y = x * lax.rsqrt(jnp.mean(x * x, axis=-1, keepdims=True) + EPS)
    n_ref[...] = (y * w_ref[...]).astype(n_ref.dtype)


def _rms_norm(x, w_row, out_dtype):
    M, D = x.shape
    tm = _pick(M, (416, 320, 256, 128, 64, 16))
    row = pl.BlockSpec((tm, D), lambda m: (m, 0))
    return pl.pallas_call(
        _norm_body, grid=(M // tm,), in_specs=[row, pl.BlockSpec((1, D), lambda m: (0, 0))], out_specs=row,
        out_shape=jax.ShapeDtypeStruct((M, D), out_dtype),
        compiler_params=_params("arbitrary"), name="rms_norm")(x, w_row)


def _join_norm_body(n_first, xp_ref, xs_ref, w_ref, x_ref, n_ref):
    x = jnp.where(pl.program_id(0) < n_first, xp_ref[...], xs_ref[...])
    x_ref[...] = x
    y = x * lax.rsqrt(jnp.mean(x * x, axis=-1, keepdims=True) + EPS)
    n_ref[...] = (y * w_ref[...]).astype(n_ref.dtype)


def _join_norm(xp, xs, w_row):
    PT, D = xp.shape
    DB = xs.shape[0]
    tm = _pick(DB, (128, 64, 16))
    assert PT % tm == 0
    n_first = PT // tm
    row = pl.BlockSpec((tm, D), lambda i: (i, 0))
    return pl.pallas_call(
        functools.partial(_join_norm_body, n_first),
        grid=((PT + DB) // tm,),
        in_specs=[pl.BlockSpec((tm, D), lambda i: (jnp.minimum(i, n_first - 1), 0)),
                  pl.BlockSpec((tm, D), lambda i: (jnp.maximum(i - n_first, 0), 0)),
                  pl.BlockSpec((1, D), lambda i: (0, 0))],
        out_specs=[row, row],
        out_shape=[jax.ShapeDtypeStruct((PT + DB, D), f32), jax.ShapeDtypeStruct((PT + DB, D), bf16)],
        compiler_params=_params("arbitrary"), name="join_norm")(xp, xs, w_row)


def _norm_split_body(n_first, x_ref, w_ref, yp_ref, ys_ref):
    i = pl.program_id(0)
    x = x_ref[...]
    y = x * lax.rsqrt(jnp.mean(x * x, axis=-1, keepdims=True) + EPS) * w_ref[...]

    @pl.when(i < n_first)
    def _():
        yp_ref[...] = y

    @pl.when(i >= n_first)
    def _():
        ys_ref[...] = y


def _norm_split(x, w_row, PT):
    M, D = x.shape
    tm = _pick(M - PT, (128, 64, 16))
    assert PT % tm == 0
    n_first = PT // tm
    return pl.pallas_call(
        functools.partial(_norm_split_body, n_first),
        grid=(M // tm,),
        in_specs=[pl.BlockSpec((tm, D), lambda i: (i, 0)), pl.BlockSpec((1, D), lambda i: (0, 0))],
        out_specs=[pl.BlockSpec((tm, D), lambda i: (jnp.minimum(i, n_first - 1), 0)),
                   pl.BlockSpec((tm, D), lambda i: (jnp.maximum(i - n_first, 0), 0))],
        out_shape=[jax.ShapeDtypeStruct((PT, D), f32), jax.ShapeDtypeStruct((M - PT, D), f32)],
        compiler_params=_params("arbitrary"), name="norm_split")(x, w_row)


def _gemm_wres_body(epi, w_is_nk, has_res, a_ref, w_ref, *rest):
    res_ref = rest[0] if has_res else None
    o_ref, wb = rest[-2:]

    @pl.when(pl.program_id(1) == 0)
    def _():
        w = w_ref[...]
        wb[...] = (w.T if w_is_nk else w).astype(bf16)
    acc = jnp.dot(a_ref[...], wb[...], preferred_element_type=f32)
    if epi is not None:
        acc = epi(acc)
    if has_res:
        acc = res_ref[...] + acc
    o_ref[...] = acc.astype(o_ref.dtype)


BIG_TM = (1040, 640, 512, 320, 256, 128, 64, 16)


def _gemm_wres(a, w, layer, col_block0, n_out, tn, epi=None, out_dtype=f32, name="gemm", w_is_nk=False,
               tm_cands=(640, 512, 320, 256, 128, 64, 16), single_buffer_w=False, res=None):
    M, K = a.shape
    tm = _pick(M, tm_cands)
    mode = dict(pipeline_mode=pl.Buffered(1)) if single_buffer_w else {}
    if w_is_nk:
        w_spec = pl.BlockSpec((None, tn, K), lambda n, m: (layer, n + col_block0, 0), **mode)
    else:
        w_spec = pl.BlockSpec((None, K, tn), lambda n, m: (layer, 0, n + col_block0), **mode)
    tile = pl.BlockSpec((tm, tn), lambda n, m: (m, n))
    return pl.pallas_call(
        functools.partial(_gemm_wres_body, epi, w_is_nk, res is not None),
        grid=(n_out // tn, M // tm),
        in_specs=[pl.BlockSpec((tm, K), lambda n, m: (m, 0)), w_spec] + ([tile] if res is not None else []),
        out_specs=tile,
        out_shape=jax.ShapeDtypeStruct((M, n_out), out_dtype),
        scratch_shapes=[pltpu.VMEM((K, tn), bf16)],
        compiler_params=_params("arbitrary", "arbitrary"), name=name)(a, w, *([res] if res is not None else []))


def _proj_norm_body(a_ref, w_ref, x_ref, nw_ref, xo_ref, n_ref, wb):
    @pl.when(pl.program_id(0) == 0)
    def _():
        wb[...] = w_ref[...].astype(bf16)
    x = x_ref[...] + jnp.dot(a_ref[...], wb[...], preferred_element_type=f32)
    xo_ref[...] = x
    y = x * lax.rsqrt(jnp.mean(x * x, axis=-1, keepdims=True) + EPS)
    n_ref[...] = (y * nw_ref[...]).astype(n_ref.dtype)


def _proj_add_norm(a, w, layer, x, nw_row):
    M, K = a.shape
    D = w.shape[-1]
    tm = _pick(M, (416, 320, 256, 128, 64, 16))
    row = lambda width: pl.BlockSpec((tm, width), lambda m: (m, 0))
    return pl.pallas_call(
        _proj_norm_body,
        grid=(M // tm,),
        in_specs=[row(K), pl.BlockSpec((None, K, D), lambda m: (layer, 0, 0), pipeline_mode=pl.Buffered(1)),
                  row(D), pl.BlockSpec((1, D), lambda m: (0, 0))],
        out_specs=[row(D), row(D)],
        out_shape=[jax.ShapeDtypeStruct((M, D), f32), jax.ShapeDtypeStruct((M, D), bf16)],
        scratch_shapes=[pltpu.VMEM((K, D), bf16)],
        compiler_params=_params("arbitrary"), name="out_proj_norm")(a, w, x, nw_row)


MERGE_SHIFT = 16


def _mix_body(u_ref, yr_ref, yh_ref, yg_ref, wm0, wm1, wm2, wx0, wx1, wx2, wr_ref, wh_ref, wg_ref, o_ref, wmb, wbb):
    tn = o_ref.shape[1]

    @pl.when(pl.program_id(1) == 0)
    def _():
        for b, (wm, wx) in enumerate(((wm0, wx0), (wm1, wx1), (wm2, wx2))):
            wcat = jnp.concatenate([wm[...], wx[...]], axis=0)
            wmb[b] = wcat[MERGE_SHIFT:MERGE_SHIFT + tn].T.astype(bf16)
        for b, wr in enumerate((wr_ref, wh_ref, wg_ref)):
            wbb[b] = wr[...].astype(bf16)

    u = u_ref[...]
    acc = None
    for b, y_ref in enumerate((yr_ref, yh_ref, yg_ref)):
        gate = jax.nn.sigmoid(jnp.dot(u, wmb[b], preferred_element_type=f32))
        p = jnp.dot(y_ref[...], wbb[b], preferred_element_type=f32)
        acc = gate * p if acc is None else acc + gate * p
    o_ref[...] = acc.astype(o_ref.dtype)


def _mix(u, y_rg, y_hg, y_gd, w_in_t, w_br_rg, w_br_hg, w_br_gd, layer, merge_col0):
    M, D = u.shape
    W = y_rg.shape[1]
    tn = 512
    tm = _pick(M, (640, 416, 320, 256, 128, 64, 16))
    nt = D // tn
    assert (merge_col0 - MERGE_SHIFT) % tn == 0 and tn % MERGE_SHIFT == 0
    base = (merge_col0 - MERGE_SHIFT) // tn
    r = tn // MERGE_SHIFT
    once = dict(pipeline_mode=pl.Buffered(1))

    def wm_spec(b):
        return pl.BlockSpec((None, tn, D), lambda n, m: (layer, base + b * nt + n, 0), **once)

    def wx_spec(b):
        return pl.BlockSpec((None, MERGE_SHIFT, D), lambda n, m: (layer, (base + b * nt + n + 1) * r, 0), **once)

    row = lambda w: pl.BlockSpec((tm, w), lambda n, m: (m, 0))
    br = pl.BlockSpec((None, W, tn), lambda n, m: (layer, 0, n), **once)
    return pl.pallas_call(
        _mix_body,
        grid=(nt, M // tm),
        in_specs=[row(D), row(W), row(W), row(W), wm_spec(0), wm_spec(1), wm_spec(2),
                  wx_spec(0), wx_spec(1), wx_spec(2), br, br, br],
        out_specs=pl.BlockSpec((tm, tn), lambda n, m: (m, n)),
        out_shape=jax.ShapeDtypeStruct((M, D), bf16),
        scratch_shapes=[pltpu.VMEM((3, D, tn), bf16), pltpu.VMEM((3, W, tn), bf16)],
        compiler_params=_params("arbitrary", "arbitrary"), name="mix")(
            u, y_rg, y_hg, y_gd, w_in_t, w_in_t, w_in_t, w_in_t, w_in_t, w_in_t, w_br_rg, w_br_hg, w_br_gd)


def _lb_body(x_ref, o_ref):
    x = x_ref[...]
    depth = x.shape[0]
    m = jnp.max(x, axis=0, keepdims=True)
    e = jnp.exp(x - m)
    p = e / jnp.sum(e, axis=0, keepdims=True)
    acc = jnp.zeros_like(p[0:1])
    o_ref[0:1, :] = acc
    for l in range(1, depth):
        acc = acc + p[l:l + 1]
        o_ref[l:l + 1, :] = acc


def _lower_bounds(logits):
    return pl.pallas_call(_lb_body, out_shape=jax.ShapeDtypeStruct(logits.shape, f32), name="hg_lower_bounds")(logits)


def _rg_gates(xc, wa, wx, ba, bx, sp):
    xb = xc.astype(bf16)
    r = jax.nn.sigmoid(jnp.dot(xb, wa.astype(bf16), preferred_element_type=f32) + ba)
    i = jax.nn.sigmoid(jnp.dot(xb, wx.astype(bf16), preferred_element_type=f32) + bx)
    log_a = (-RG_C) * r * sp
    a = jnp.exp(log_a)
    m2 = _expm1_neg(2.0 * log_a, a * a)
    mult = jnp.where(m2 > 0.0, m2 * lax.rsqrt(m2), 0.0)
    return a, mult, i


def _rg_prompt_body(x_ref, gate_ref, cw_ref, cb_ref, wa_ref, wx_ref, ba_ref, bx_ref, ap_ref, y_ref, h_ref, xbuf, hprev):
    c = pl.program_id(1)
    tc = x_ref.shape[0]
    nblk = x_ref.shape[1] // HEAD

    @pl.when(c == 0)
    def _():
        xbuf[0:HIST, :] = jnp.zeros((HIST, xbuf.shape[1]), f32)
        hprev[...] = jnp.zeros_like(hprev)

    xbuf[HIST:HIST + tc, :] = x_ref[...]
    row = lax.broadcasted_iota(jnp.int32, (tc, HEAD), 0)
    first = jnp.logical_and(row == 0, c == 0)
    for n in range(nblk):
        ls = slice(n * HEAD, (n + 1) * HEAD)
        xc = _causal_conv4(xbuf, cw_ref, tc, ls) + cb_ref[:, ls]
        sp = _softplus(-ap_ref[:, ls])
        a, mult, i = _rg_gates(xc, wa_ref[n], wx_ref[n], ba_ref[:, ls], bx_ref[:, ls], sp)
        mult = jnp.where(first, 1.0, mult)
        b = mult * (i * xc)
        s = 1
        while s < SUBLANES:
            keep = (row % SUBLANES) >= s
            a_sh = jnp.where(keep, pltpu.roll(a, s, axis=0), 1.0)
            b_sh = jnp.where(keep, pltpu.roll(b, s, axis=0), 0.0)
            b = a * b_sh + b
            a = a * a_sh
            s *= 2
        carry = hprev[:, ls]
        groups = []
        for g in range(tc // SUBLANES):
            rows = slice(g * SUBLANES, (g + 1) * SUBLANES)
            hg = b[rows] + a[rows] * carry
            groups.append(hg)
            carry = hg[SUBLANES - 1:SUBLANES, :]
        h = jnp.concatenate(groups, axis=0)
        hprev[:, ls] = carry
        y_ref[:, ls] = (h * jax.nn.gelu(gate_ref[:, ls], approximate=True)).astype(y_ref.dtype)
    xbuf[0:HIST, :] = xbuf[tc:tc + HIST, :]
    h_ref[...] = hprev[...]


def _rg_prompt(proj, B, T, M, lp, layer):
    W = lp["rg_ba"].shape[-1]
    tc = _pick(T, (256, 128, 64))
    nT = T // tc
    nblk = W // HEAD
    vec = pl.BlockSpec((None, 1, W), lambda b, c: (layer, 0, 0))
    blk = pl.BlockSpec((None, nblk, HEAD, HEAD), lambda b, c: (layer, 0, 0, 0))
    y, h = pl.pallas_call(
        _rg_prompt_body,
        grid=(B, nT),
        in_specs=[pl.BlockSpec((tc, W), lambda b, c: (b * nT + c, 0)),
                  pl.BlockSpec((tc, W), lambda b, c: (b * nT + c, 1)),
                  pl.BlockSpec((None, CONV_W, W), lambda b, c: (layer, 0, 0)),
                  vec, blk, blk, vec, vec, vec],
        out_specs=[pl.BlockSpec((tc, W), lambda b, c: (b * nT + c, 0)),
                   pl.BlockSpec((None, 1, W), lambda b, c: (b, 0, 0))],
        out_shape=[jax.ShapeDtypeStruct((M, W), bf16), jax.ShapeDtypeStruct((B, 1, W), f32)],
        scratch_shapes=[pltpu.VMEM((HIST + tc, W), f32), pltpu.VMEM((1, W), f32)],
        compiler_params=_params("arbitrary", "arbitrary"), name="rg_prompt")(
            proj, proj, lp["rg_conv_w"], lp["rg_conv_b"], lp["rg_wa"], lp["rg_wx"], lp["rg_ba"], lp["rg_bx"],
            lp["rg_a_param"])
    return y, h[:, 0]


def _rg_sample_body(x_ref, gate_ref, cs_ref, h0_ref, cw_ref, cb_ref, wa_ref, wx_ref, ba_ref, bx_ref, ap_ref,
                    yin_ref, y_ref, h_ref):
    del yin_ref
    nblk = x_ref.shape[1] // HEAD
    for n in range(nblk):
        ls = slice(n * HEAD, (n + 1) * HEAD)
        xc = cb_ref[:, ls] + cw_ref[CONV_W - 1:CONV_W, ls] * x_ref[:, ls]
        for j in range(CONV_W - 1):
            xc = xc + cw_ref[j:j + 1, ls] * cs_ref[j, :, ls]
        sp = _softplus(-ap_ref[:, ls])
        a, mult, i = _rg_gates(xc, wa_ref[n], wx_ref[n], ba_ref[:, ls], bx_ref[:, ls], sp)
        h = a * h0_ref[:, ls] + mult * (i * xc)
        h_ref[:, ls] = h
        y_ref[:, ls] = (h * jax.nn.gelu(gate_ref[:, ls], approximate=True)).astype(y_ref.dtype)


def _rg_sample(proj, y_all, row0, DB, conv_state_t, h0, lp, layer):
    W = lp["rg_ba"].shape[-1]
    nblk = W // HEAD
    assert row0 % DB == 0
    rb = row0 // DB
    vec = pl.BlockSpec((None, 1, W), lambda i: (layer, 0, 0))
    blk = pl.BlockSpec((None, nblk, HEAD, HEAD), lambda i: (layer, 0, 0, 0))
    y, h = pl.pallas_call(
        _rg_sample_body,
        grid=(1,),
        in_specs=[pl.BlockSpec((DB, W), lambda i: (rb, 0)),
                  pl.BlockSpec((DB, W), lambda i: (rb, 1)),
                  pl.BlockSpec((None, CONV_W - 1, DB, W), lambda i: (layer, 0, 0, 0)),
                  pl.BlockSpec((None, DB, W), lambda i: (layer, 0, 0)),
                  pl.BlockSpec((None, CONV_W, W), lambda i: (layer, 0, 0)),
                  vec, blk, blk, vec, vec, vec,
                  pl.BlockSpec(memory_space=pl.ANY)],
        out_specs=[pl.BlockSpec((DB, W), lambda i: (rb, 0)),
                   pl.BlockSpec((DB, W), lambda i: (0, 0))],
        out_shape=[jax.ShapeDtypeStruct(y_all.shape, y_all.dtype), jax.ShapeDtypeStruct((DB, W), f32)],
        input_output_aliases={11: 0},
        compiler_params=_params("arbitrary"), name="rg_sample")(
            proj, proj, conv_state_t, h0, lp["rg_conv_w"], lp["rg_conv_b"], lp["rg_wa"], lp["rg_wx"],
            lp["rg_ba"], lp["rg_bx"], lp["rg_a_param"], y_all)
    return y, h


def _hg_gates(fx, lb):
    f = lb + (1.0 - lb) * jax.nn.sigmoid(fx)
    k = (1.0 - lb) * jax.nn.sigmoid(-fx)
    return f, k


def _hg_intra_diag(G, q, k):
    lane = lax.broadcasted_iota(jnp.int32, (SUB, CHUNK), 1)
    blocks = []
    for i in range(CHUNK // SUB):
        sl = slice(i * SUB, (i + 1) * SUB)
        g_i, q_i, k_i = G[sl], q[sl], k[sl]
        a_d = jnp.zeros((SUB, CHUNK), f32)
        for s in range(SUB):
            e = jnp.exp(g_i - g_i[s:s + 1, :])
            col = jnp.sum(q_i * k_i[s:s + 1, :] * e, axis=-1, keepdims=True)
            a_d = jnp.where(lane == i * SUB + s, col, a_d)
        blocks.append(a_d)
    return jnp.where(_tril_mask(CHUNK), jnp.concatenate(blocks, axis=0), 0.0)


def _hg_intra_off(G, q, k):
    nsub = CHUNK // SUB
    row = lax.broadcasted_iota(jnp.int32, (CHUNK, HEAD), 0)
    q_parts, k_parts = [], []
    for j in range(nsub - 1):
        g_e = G[(j + 1) * SUB - 1:(j + 1) * SUB, :]
        q_parts.append(jnp.where(row >= (j + 1) * SUB, q * jnp.exp(G - g_e), 0.0))
        in_j = jnp.logical_and(row >= j * SUB, row < (j + 1) * SUB)
        k_parts.append(jnp.where(in_j, k * jnp.exp(g_e - G), 0.0))
    return _mm(jnp.concatenate(q_parts, axis=1), jnp.concatenate(k_parts, axis=1), _NT)


def _hg_prompt_body(scale, HB, q_ref, f_ref, i_ref, g_ref, lb_ref, nw_ref, y_ref, s_ref, S_scr):
    c = pl.program_id(2)
    nchunk = q_ref.shape[0] // CHUNK

    @pl.when(c == 0)
    def _():
        S_scr[...] = jnp.zeros_like(S_scr)

    nw = nw_ref[...]
    tril = _chunk_tril(nchunk * CHUNK)
    work = []
    for hh in range(HB):
        ls = slice(hh * HEAD, (hh + 1) * HEAD)
        f_all, k_all = _hg_gates(f_ref[:, ls], lb_ref[:, ls])
        q_all = q_ref[:, ls] * scale
        G_all = _mm_exact_lhs(tril, jnp.log(f_all))
        for ci in range(nchunk):
            rows = slice(ci * CHUNK, (ci + 1) * CHUNK)
            work.append(dict(hh=hh, ls=ls, rows=rows, G=G_all[rows], q=q_all[rows], k=k_all[rows]))
    for w in work:
        w["A"] = _hg_intra_diag(w["G"], w["q"], w["k"])
    for w in work:
        w["A"] = w["A"] + _hg_intra_off(w["G"], w["q"], w["k"])
    for w in work:
        G = w["G"]
        kT, GT = w["k"].T, G.T
        g_last = GT[:, CHUNK - 1:CHUNK]
        w["dec"] = jnp.exp(g_last)
        w["upd"] = _mm(kT * jnp.exp(g_last - GT), i_ref[w["rows"], w["ls"]])
        w["lhs"] = jnp.concatenate([w["A"], w["q"] * jnp.exp(G)], axis=1)
    S = [S_scr[hh] for hh in range(HB)]
    for ci in range(nchunk):
        for hh in range(HB):
            w = work[hh * nchunk + ci]
            rows, ls = w["rows"], w["ls"]
            o = _mm(w["lhs"], jnp.concatenate([i_ref[rows, ls], S[hh]], axis=0))
            S[hh] = S[hh] * w["dec"] + w["upd"]
            y_ref[rows, ls] = _gated_rms(o, nw, g_ref[rows, ls]).astype(y_ref.dtype)
    for hh in range(HB):
        S_scr[hh] = S[hh]

    @pl.when(c == pl.num_programs(2) - 1)
    def _():
        for hh in range(HB):
            s_ref[hh] = S[hh]


def _hg_prompt(proj, B, T, M, H, col0, lb, norm_w, layer):
    tc = _pick(T, (256, 128, 64))
    nT = T // tc
    HB = HEADS_PER_STEP
    assert H % HB == 0 and col0 % HB == 0
    wb = HB * HEAD
    col = lambda j: pl.BlockSpec((tc, wb), lambda b, h, c: (b * nT + c, (col0 + j * H) // HB + h))
    y, S = pl.pallas_call(
        functools.partial(_hg_prompt_body, HEAD ** -0.5, HB),
        grid=(B, H // HB, nT),
        in_specs=[col(0), col(1), col(2), col(3),
                  pl.BlockSpec((None, 1, wb), lambda b, h, c: (layer, 0, h)),
                  pl.BlockSpec((None, 1, HEAD), lambda b, h, c: (layer, 0, 0))],
        out_specs=[pl.BlockSpec((tc, wb), lambda b, h, c: (b * nT + c, h)),
                   pl.BlockSpec((None, HB, HEAD, HEAD), lambda b, h, c: (b, h, 0, 0))],
        out_shape=[jax.ShapeDtypeStruct((M, H * HEAD), bf16), jax.ShapeDtypeStruct((B, H, HEAD, HEAD), f32)],
        scratch_shapes=[pltpu.VMEM((HB, HEAD, HEAD), f32)],
        compiler_params=_params("arbitrary", "arbitrary", "arbitrary"), name="hg_prompt")(
            proj, proj, proj, proj, lb, norm_w)
    return y, S


SB = 16


def _state_step(s_ref, so_ref, o_scr, h, kT, q, vnew_fn):
    for j in range(SB):
        S = s_ref[j, h]
        kcol = kT[:, j:j + 1]
        d, vnew = vnew_fn(j, S, kcol)
        so_ref[j, h] = d * S + kcol * vnew
    qb = q.astype(bf16)
    for j in range(SB):
        o_scr[h, j:j + 1, :] = jnp.dot(qb, so_ref[j, h].astype(bf16), preferred_element_type=f32)[j:j + 1, :]


def _hg_sample_body(scale, H, q_ref, f_ref, i_ref, g_ref, lb_ref, nw_ref, s_ref, *rest):
    y_ref, so_ref, o_scr = rest[-3:]

    def head(h, carry):
        ls = pl.ds(pl.multiple_of(h * HEAD, HEAD), HEAD)
        f, k = _hg_gates(f_ref[:, ls], lb_ref[:, ls])
        q = q_ref[:, ls] * scale
        v = i_ref[:, ls]
        fT, kT = f.T, k.T

        def vnew(j, S, kcol):
            return fT[:, j:j + 1], v[j:j + 1, :]

        _state_step(s_ref, so_ref, o_scr, h, kT, q, vnew)
        y_ref[:, ls] = _gated_rms(o_scr[h], nw_ref[...], g_ref[:, ls]).astype(y_ref.dtype)
        return carry

    lax.fori_loop(0, H, head, 0)


def _state_out(state, stacked_prev, n_in):
    extra_in, extra_specs, aliases = [], [], {}
    if stacked_prev is not None:
        extra_in, extra_specs, aliases = [stacked_prev], [pl.BlockSpec(memory_space=pl.ANY)], {n_in: 1}
    return jax.ShapeDtypeStruct(state.shape, f32), extra_in, extra_specs, aliases


def _hg_sample(proj, y_all, row0, DB, H, col0, lb, norm_w, state, stacked_prev, layer):
    assert row0 % SB == 0 and DB % SB == 0 and col0 % H == 0
    rb = row0 // SB
    wh = H * HEAD
    col = lambda j: pl.BlockSpec((SB, wh), lambda b: (rb + b, col0 // H + j))
    st_spec = pl.BlockSpec((None, SB, H, HEAD, HEAD), lambda b: (layer, b, 0, 0, 0))
    s_shape, extra_in, extra_specs, aliases = _state_out(state, stacked_prev, 8)
    y, S = pl.pallas_call(
        functools.partial(_hg_sample_body, HEAD ** -0.5, H),
        grid=(DB // SB,),
        in_specs=[col(0), col(1), col(2), col(3),
                  pl.BlockSpec((None, 1, wh), lambda b: (layer, 0, 0)),
                  pl.BlockSpec((None, 1, HEAD), lambda b: (layer, 0, 0)),
                  st_spec,
                  pl.BlockSpec(memory_space=pl.ANY)] + extra_specs,
        out_specs=[pl.BlockSpec((SB, wh), lambda b: (rb + b, 0)), st_spec],
        out_shape=[jax.ShapeDtypeStruct(y_all.shape, y_all.dtype), s_shape],
        scratch_shapes=[pltpu.VMEM((H, SB, HEAD), f32)],
        input_output_aliases={7: 0, **aliases},
        compiler_params=_params("arbitrary"), name="hg_sample")(
            proj, proj, proj, proj, lb, norm_w, state, y_all, *extra_in)
    return y, S


def _pick_lane(x, idx):
    lane = lax.broadcasted_iota(jnp.int32, x.shape, 1)
    col = jnp.sum(jnp.where(lane == idx, x, 0.0), axis=1, keepdims=True)
    return jnp.broadcast_to(col, (x.shape[0], HEAD))


def _gd_gate_body(H, PT, u_ref, w_ref, alog_ref, dtb_ref, o_ref, wb):
    i = pl.program_id(0)
    tm = u_ref.shape[0]

    @pl.when(i == 0)
    def _():
        wb[...] = w_ref[...].T.astype(bf16)

    x = jnp.dot(u_ref[...], wb[...], preferred_element_type=f32)
    lane = lax.broadcasted_iota(jnp.int32, x.shape, 1)
    row = lax.broadcasted_iota(jnp.int32, x.shape, 0) + i * tm
    g = jnp.where(lane < H, -jnp.exp(alog_ref[...]) * _softplus(x + dtb_ref[...]), 0.0)
    blk = min(2 * CHUNK, tm)
    tril = _chunk_tril(blk)
    G = jnp.concatenate([_mm_exact_lhs(tril, g[r:r + blk]) for r in range(0, tm, blk)], axis=0)
    G = jnp.where(row < PT, G, g)
    o_ref[...] = jnp.where(lane < H, G, jax.nn.sigmoid(x))


def _gd_gate_proj(u, w_in_t, row_block, alog_pad, dtb_pad, PT, H, layer):
    M, D = u.shape
    tm = _pick(M, (640, 320, 128, 64))
    assert PT % CHUNK == 0 and tm % CHUNK == 0
    vec = pl.BlockSpec((None, 1, LANE), lambda i: (layer, 0, 0))
    return pl.pallas_call(
        functools.partial(_gd_gate_body, H, PT),
        grid=(M // tm,),
        in_specs=[pl.BlockSpec((tm, D), lambda i: (i, 0)),
                  pl.BlockSpec((None, LANE, D), lambda i: (layer, row_block, 0)), vec, vec],
        out_specs=pl.BlockSpec((tm, LANE), lambda i: (i, 0)),
        out_shape=jax.ShapeDtypeStruct((M, LANE), f32),
        scratch_shapes=[pltpu.VMEM((D, LANE), bf16)],
        compiler_params=_params("arbitrary"), name="gd_gate_proj")(u, w_in_t, alog_pad, dtb_pad)


def _unit_lower_inverses(Ns):
    r = lax.broadcasted_iota(jnp.int32, (CHUNK, CHUNK), 0)
    c = lax.broadcasted_iota(jnp.int32, (CHUNK, CHUNK), 1)
    same = lambda n: (r // n) == (c // n)
    assert CHUNK == 4 * INV_BLOCK
    dot = functools.partial(jnp.dot, preferred_element_type=f32)
    cast = lambda xs: [x.astype(bf16) for x in xs]
    Rs = [jnp.where(same(INV_BLOCK), N, 0.0) for N in Ns]
    Rb = cast(Rs)
    Ps = [dot(rb, rb) for rb in Rb]
    p = 2
    while p < INV_BLOCK:
        Pb = cast(Ps)
        if 2 * p < INV_BLOCK:
            PMs = [dot(pb, jnp.concatenate([rb, pb], axis=1)) for rb, pb in zip(Rb, Pb)]
            Rs = [R + P + PM[:, :CHUNK] for R, P, PM in zip(Rs, Ps, PMs)]
            Ps = [PM[:, CHUNK:] for PM in PMs]
            Rb = cast(Rs)
        else:
            PRs = [dot(pb, rb) for rb, pb in zip(Rb, Pb)]
            Rs = [R + P + PR for R, P, PR in zip(Rs, Ps, PRs)]
        p *= 2
    eye = jnp.where(r == c, 1.0, 0.0)
    Nb = cast(Ns)
    for n in (2 * INV_BLOCK, 4 * INV_BLOCK):
        off = jnp.logical_and(same(n), jnp.logical_not(same(n // 2)))
        Db = cast([eye + R for R in Rs])
        DCs = [dot(db, jnp.where(off, nb, jnp.zeros_like(nb))) for db, nb in zip(Db, Nb)]
        DCDs = [dot(dc, db) for dc, db in zip(cast(DCs), Db)]
        Rs = [R + DCD for R, DCD in zip(Rs, DCDs)]
    return Rs


def _gd_prompt_body(scale, H, HB, q_ref, k_ref, v_ref, z_ref, gt_ref, cwq_ref, cwk_ref, cwv_ref, nw_ref,
                    y_ref, s_ref, xq, xk, xv, S_scr):
    hb = pl.program_id(1)
    c = pl.program_id(2)
    tc = q_ref.shape[0]
    nchunk = tc // CHUNK

    @pl.when(c == 0)
    def _():
        for xb in (xq, xk, xv):
            xb[0:HIST, :] = jnp.zeros((HIST, xb.shape[1]), f32)
        S_scr[...] = jnp.zeros_like(S_scr)

    def conv_silu(x_ref, xb, cw_ref):
        xb[HIST:HIST + tc, :] = x_ref[...]
        y = _causal_conv4(xb, cw_ref, tc)
        xb[0:HIST, :] = xb[tc:tc + HIST, :]
        return _silu(y)

    qc = conv_silu(q_ref, xq, cwq_ref)
    kc = conv_silu(k_ref, xk, cwk_ref)
    vc = conv_silu(v_ref, xv, cwv_ref)
    gt = gt_ref[...]
    nw = nw_ref[...]
    tril_b = _tril_mask(CHUNK)
    strict_b = _tril_mask(CHUNK, strict=True)

    work = []
    for hh in range(HB):
        ls = slice(hh * HEAD, (hh + 1) * HEAD)
        h = hb * HB + hh
        q_all = _l2norm(qc[:, ls]) * scale
        k_all = _l2norm(kc[:, ls])
        G_all = _pick_lane(gt, h)
        beta = _pick_lane(gt, H + h)
        eG = jnp.exp(G_all)
        kb_all = k_all * beta
        rhs_all = jnp.concatenate([vc[:, ls] * beta, kb_all * eG], axis=1)
        qe_all = q_all * eG
        for ci in range(nchunk):
            rows = slice(ci * CHUNK, (ci + 1) * CHUNK)
            work.append(dict(hh=hh, ls=ls, rows=rows, G=G_all[rows], q=q_all[rows], k=k_all[rows], kb=kb_all[rows],
                             X=rhs_all[rows], qe=qe_all[rows]))
    for w in work:
        G = w["G"]
        w["decay"] = jnp.where(tril_b, jnp.exp(G[:, :CHUNK] - G.T[:CHUNK, :]), 0.0)
        w["KQ"] = _mm(jnp.concatenate([w["kb"], w["q"]], axis=0), w["k"], _NT)
    Rs = _unit_lower_inverses([jnp.where(strict_b, -(w["KQ"][:CHUNK] * w["decay"]), 0.0) for w in work])
    for w, R in zip(work, Rs):
        w["R"] = R
        w["qk"] = jnp.where(tril_b, w["KQ"][CHUNK:] * w["decay"], 0.0)
    for w in work:
        X = w["X"]
        w["X"] = X + _mm(w["R"], X)
        G = w["G"]
        g_last = G[CHUNK - 1:CHUNK, :]
        w["egl"] = jnp.exp(g_last)
        w["rhs2"] = jnp.concatenate([w["qk"], (w["k"] * jnp.exp(g_last - G)).T], axis=0)
    S = [S_scr[hh] for hh in range(HB)]
    for ci in range(nchunk):
        for hh in range(HB):
            w = work[hh * nchunk + ci]
            rows, ls = w["rows"], w["ls"]
            WS = _mm(jnp.concatenate([w["X"][:, HEAD:], w["qe"]], axis=0), S[hh])
            v_new = w["X"][:, :HEAD] - WS[:CHUNK]
            OS = _mm(w["rhs2"], v_new)
            S[hh] = w["egl"] * S[hh] + OS[CHUNK:]
            y_ref[rows, ls] = _gated_rms(WS[CHUNK:] + OS[:CHUNK], nw, z_ref[rows, ls]).astype(y_ref.dtype)
    for hh in range(HB):
        S_scr[hh] = S[hh]

    @pl.when(c == pl.num_programs(2) - 1)
    def _():
        for hh in range(HB):
            s_ref[hh] = S[hh]


def _gd_prompt(proj, gates, B, T, M, H, col0, lp, layer):
    tc = _pick(T, (256, 128, 64))
    nT = T // tc
    HB = HEADS_PER_STEP
    assert H % HB == 0 and col0 % HB == 0
    wb = HB * HEAD
    col = lambda j: pl.BlockSpec((tc, wb), lambda b, h, c: (b * nT + c, (col0 + j * H) // HB + h))
    cw = lambda j: pl.BlockSpec((None, CONV_W, wb), lambda b, h, c: (layer, 0, j * H // HB + h))
    buf = pltpu.VMEM((HIST + tc, wb), f32)
    y, S = pl.pallas_call(
        functools.partial(_gd_prompt_body, HEAD ** -0.5, H, HB),
        grid=(B, H // HB, nT),
        in_specs=[col(0), col(1), col(2), col(3),
                  pl.BlockSpec((tc, LANE), lambda b, h, c: (b * nT + c, 0)),
                  cw(0), cw(1), cw(2),
                  pl.BlockSpec((None, 1, HEAD), lambda b, h, c: (layer, 0, 0))],
        out_specs=[pl.BlockSpec((tc, wb), lambda b, h, c: (b * nT + c, h)),
                   pl.BlockSpec((None, HB, HEAD, HEAD), lambda b, h, c: (b, h, 0, 0))],
        out_shape=[jax.ShapeDtypeStruct((M, H * HEAD), bf16), jax.ShapeDtypeStruct((B, H, HEAD, HEAD), f32)],
        scratch_shapes=[buf, buf, buf, pltpu.VMEM((HB, HEAD, HEAD), f32)],
        compiler_params=_params("arbitrary", "arbitrary", "arbitrary"), name="gd_prompt")(
            proj, proj, proj, proj, gates, lp["gd_conv_w"], lp["gd_conv_w"], lp["gd_conv_w"], lp["gd_norm_w"])
    return y, S


def _gd_sample_body(scale, H, q_ref, k_ref, v_ref, z_ref, gt_ref, csq_ref, csk_ref, csv_ref, cwq_ref, cwk_ref, cwv_ref,
                    nw_ref, s_ref, *rest):
    y_ref, so_ref, o_scr = rest[-3:]
    gt = gt_ref[...]

    def head(h):
        ls = slice(h * HEAD, (h + 1) * HEAD)

        def conv_silu(x_ref, cs_ref, cw_ref):
            y = cw_ref[CONV_W - 1:CONV_W, ls] * x_ref[:, ls]
            for j in range(CONV_W - 1):
                y = y + cw_ref[j:j + 1, ls] * cs_ref[j, :, ls]
            return _silu(y)

        q = _l2norm(conv_silu(q_ref, csq_ref, cwq_ref)) * scale
        k = _l2norm(conv_silu(k_ref, csk_ref, cwk_ref))
        v = conv_silu(v_ref, csv_ref, cwv_ref)
        eg = jnp.exp(_pick_lane(gt, h))
        beta = _pick_lane(gt, H + h)
        kT = k.T

        def vnew(j, S, kcol):
            egj = eg[j:j + 1, :]
            kS = jnp.sum(kcol * S, axis=0, keepdims=True)
            return egj, beta[j:j + 1, :] * (v[j:j + 1, :] - egj * kS)

        _state_step(s_ref, so_ref, o_scr, h, kT, q, vnew)
        y_ref[:, ls] = _gated_rms(o_scr[h], nw_ref[...], z_ref[:, ls]).astype(y_ref.dtype)
    for h in range(H):
        head(h)


def _gd_sample(proj, gates, y_all, row0, DB, H, col0, conv_state_t, state, stacked_prev, lp, layer):
    assert row0 % SB == 0 and DB % SB == 0 and col0 % H == 0
    rb = row0 // SB
    wh = H * HEAD
    col = lambda j: pl.BlockSpec((SB, wh), lambda b: (rb + b, col0 // H + j))
    cs = lambda j: pl.BlockSpec((None, CONV_W - 1, SB, wh), lambda b: (layer, 0, b, j))
    cw = lambda j: pl.BlockSpec((None, CONV_W, wh), lambda b: (layer, 0, j))
    st_spec = pl.BlockSpec((None, SB, H, HEAD, HEAD), lambda b: (layer, b, 0, 0, 0))
    s_shape, extra_in, extra_specs, aliases = _state_out(state, stacked_prev, 14)
    y, S = pl.pallas_call(
        functools.partial(_gd_sample_body, HEAD ** -0.5, H),
        grid=(DB // SB,),
        in_specs=[col(0), col(1), col(2), col(3),
                  pl.BlockSpec((SB, LANE), lambda b: (rb + b, 0)),
                  cs(0), cs(1), cs(2), cw(0), cw(1), cw(2),
                  pl.BlockSpec((None, 1, HEAD), lambda b: (layer, 0, 0)),
                  st_spec,
                  pl.BlockSpec(memory_space=pl.ANY)] + extra_specs,
        out_specs=[pl.BlockSpec((SB, wh), lambda b: (rb + b, 0)), st_spec],
        out_shape=[jax.ShapeDtypeStruct(y_all.shape, y_all.dtype), s_shape],
        scratch_shapes=[pltpu.VMEM((H, SB, HEAD), f32)],
        input_output_aliases={13: 0, **aliases},
        compiler_params=_params("arbitrary"), name="gd_sample")(
            proj, proj, proj, proj, gates, conv_state_t, conv_state_t, conv_state_t,
            lp["gd_conv_w"], lp["gd_conv_w"], lp["gd_conv_w"], lp["gd_norm_w"],
            state, y_all, *extra_in)
    return y, S


def kernel(x_prompt, x_sample, state_rg_h, state_rg_conv, state_hg_S, state_gd_S, state_gd_conv, norm_mix_w, norm_mlp_w, norm_final_w, w_in, rg_conv_w, rg_conv_b, rg_wa, rg_ba, rg_wx, rg_bx, rg_a_param, hg_lb_logits, hg_norm_w, gd_conv_w, gd_A_log, gd_dt_bias, gd_norm_w, w_br_rg, w_br_hg, w_br_gd, w_out, w_up, w_down):
    B, T, D = x_prompt.shape
    DB, DT, _ = x_sample.shape
    assert DT == 1
    depth = w_in.shape[0]
    RW = rg_ba.shape[-1]
    H = gd_A_log.shape[-1]
    PT = B * T
    M = PT + DB
    assert RW % HEAD == 0 and hg_norm_w.shape[-1] == HEAD and gd_norm_w.shape[-1] == HEAD
    HW = H * HEAD
    n_main = 2 * RW + 8 * HW
    merge_col0 = n_main + 2 * H
    assert w_in.shape[-1] == merge_col0 + 3 * D and n_main % 1024 == 0 and n_main % LANE == 0
    hg_col0 = 2 * RW // HEAD
    gd_col0 = hg_col0 + 4 * H

    row3 = lambda a: a.reshape(depth, 1, a.shape[-1])
    lane_pad = lambda a: row3(jnp.pad(a.astype(f32), ((0, 0), (0, LANE - a.shape[-1]))))
    lp = dict(rg_conv_w=rg_conv_w, rg_conv_b=row3(rg_conv_b), rg_wa=rg_wa, rg_wx=rg_wx, rg_ba=row3(rg_ba),
              rg_bx=row3(rg_bx), rg_a_param=row3(rg_a_param), gd_conv_w=gd_conv_w, gd_norm_w=row3(gd_norm_w))
    alog_pad, dtb_pad = lane_pad(gd_A_log), lane_pad(gd_dt_bias)
    hg_nw = row3(hg_norm_w)
    lb = row3(_lower_bounds(hg_lb_logits.astype(f32)))
    rg_conv_t = jnp.swapaxes(state_rg_conv, 1, 2)
    gd_conv_t = jnp.swapaxes(state_gd_conv, 1, 2)

    w_in_t = jnp.swapaxes(w_in, 1, 2)
    x, u = _join_norm(x_prompt.reshape(PT, D), x_sample.reshape(DB, D), norm_mix_w[0:1])

    p_states, s_states = [], []
    y_prompt = y_sample = None
    s_hgS = s_gdS = None
    for l in range(depth):
        proj = _gemm_wres(u, w_in_t, l, 0, n_main, 1024, name="in_proj", w_is_nk=True, tm_cands=BIG_TM)
        gates = _gd_gate_proj(u, w_in_t, n_main // LANE, alog_pad, dtb_pad, PT, H, l)

        y_rg, p_h = _rg_prompt(proj, B, T, M, lp, l)
        y_rg, s_h = _rg_sample(proj, y_rg, PT, DB, rg_conv_t, state_rg_h, lp, l)
        y_hg, p_hgS = _hg_prompt(proj, B, T, M, H, hg_col0, lb, hg_nw, l)
        y_hg, s_hgS = _hg_sample(proj, y_hg, PT, DB, H, hg_col0, lb, hg_nw, state_hg_S, s_hgS, l)
        y_gd, p_gdS = _gd_prompt(proj, gates, B, T, M, H, gd_col0, lp, l)
        y_gd, s_gdS = _gd_sample(proj, gates, y_gd, PT, DB, H, gd_col0, gd_conv_t, state_gd_S, s_gdS, lp, l)

        gq = gd_col0 * HEAD
        tail = lambda c0, w: jnp.stack([lax.slice(proj, (b * T + T - (CONV_W - 1), c0), (b * T + T, c0 + w))
                                        for b in range(B)], axis=0)
        last = lambda c0, w: lax.slice(proj, (PT, c0), (M, c0 + w))[:, None, :]
        p_states.append((p_h, tail(0, RW), p_hgS, p_gdS, tail(gq, 3 * HW)))
        s_states.append((s_h,
                         jnp.concatenate([state_rg_conv[l][:, 1:], last(0, RW)], axis=1),
                         None, None,
                         jnp.concatenate([state_gd_conv[l][:, 1:], last(gq, 3 * HW)], axis=1)))

        mixed = _mix(u, y_rg, y_hg, y_gd, w_in_t, w_br_rg, w_br_hg, w_br_gd, l, merge_col0)
        x, hmid = _proj_add_norm(mixed, w_out, l, x, norm_mlp_w[l:l + 1])
        hh = _gemm_wres(hmid, w_up, l, 0, w_up.shape[-1], 1024, epi=lambda a: jnp.square(jnp.maximum(a, 0.0)),
                        out_dtype=bf16, name="mlp_up", tm_cands=BIG_TM)
        x = _gemm_wres(hh, w_down, l, 0, D, 512, name="mlp_down", single_buffer_w=True, res=x)
        if l + 1 < depth:
            u = _rms_norm(x, norm_mix_w[l + 1:l + 2], bf16)
        else:
            y_prompt, y_sample = _norm_split(x, norm_final_w.reshape(1, D), PT)

    def stack(sts, j, like):
        return jnp.stack([s[j] for s in sts], axis=0).astype(like.dtype)

    return (y_prompt.reshape(B, T, D), y_sample.reshape(DB, DT, D),
            stack(p_states, 0, state_rg_h), stack(p_states, 1, state_rg_conv), stack(p_states, 2, state_hg_S),
            stack(p_states, 3, state_gd_S), stack(p_states, 4, state_gd_conv),
            stack(s_states, 0, state_rg_h), stack(s_states, 1, state_rg_conv), s_hgS.astype(state_hg_S.dtype),
            s_gdS.astype(state_gd_S.dtype), stack(s_states, 4, state_gd_conv))
```

```python
import functools

import jax
import jax.numpy as jnp
from jax import lax
from jax.experimental import pallas as pl
from jax.experimental.pallas import tpu as pltpu

f32 = jnp.float32
bf16 = jnp.bfloat16

EPS = 1e-6
RG_C = 8.0
HEAD = 128
LANE = 128
SUBLANES = 8
HIST = SUBLANES
CHUNK = 64
SUB = 8
INV_BLOCK = 16
HEADS_PER_STEP = 8
CONV_W = 4
VMEM_LIMIT = 56 * 1024 * 1024

_NT = (((1,), (1,)), ((), ()))


def _params(*sem):
    return pltpu.CompilerParams(dimension_semantics=sem, vmem_limit_bytes=VMEM_LIMIT)


def _pick(n, cands):
    for c in cands:
        if n % c == 0:
            return c
    raise ValueError(f"no tile for {n} among {cands}")


def _mm(a, b, dims=None):
    a = a.astype(bf16)
    b = b.astype(bf16)
    if dims is None:
        return jnp.dot(a, b, preferred_element_type=f32)
    return lax.dot_general(a, b, dims, preferred_element_type=f32)


def _split3(x):
    hi = x.astype(bf16)
    r = x - hi.astype(f32)
    mid = r.astype(bf16)
    lo = (r - mid.astype(f32)).astype(bf16)
    return hi, mid, lo


def _mm_exact_lhs(a_bf16, x):
    hi, mid, lo = _split3(x)
    return (jnp.dot(a_bf16, hi, preferred_element_type=f32) + jnp.dot(a_bf16, mid, preferred_element_type=f32)
            + jnp.dot(a_bf16, lo, preferred_element_type=f32))


def _expm1_neg(x, ex):
    return -jnp.tanh(0.5 * x) * (ex + 1.0)


def _softplus(x):
    return jnp.maximum(x, 0.0) + jnp.log1p(jnp.exp(-jnp.abs(x)))


def _silu(x):
    return x * jax.nn.sigmoid(x)


def _gated_rms(o, w, z):
    o = o * lax.rsqrt(jnp.mean(o * o, axis=-1, keepdims=True) + EPS) * w
    return o * _silu(z)


def _l2norm(x):
    return x * lax.rsqrt(jnp.sum(x * x, axis=-1, keepdims=True) + EPS)


def _causal_conv4(xbuf, cw_ref, tc, ls=slice(None)):
    assert CONV_W == 4
    w0, w1, w2, w3 = (cw_ref[j:j + 1, ls] for j in range(CONV_W))
    x0 = xbuf[HIST:HIST + tc, ls]
    row = lax.broadcasted_iota(jnp.int32, x0.shape, 0)
    r2 = pltpu.roll(x0, 2, axis=0)
    xm2 = jnp.concatenate([xbuf[pl.ds(HIST - 2, SUBLANES), ls], r2[SUBLANES:]], axis=0)
    even = w3 * x0 + w1 * xm2
    odd = w2 * x0 + w0 * xm2
    odd_before = w2 * xbuf[HIST - 1:HIST, ls] + w0 * xbuf[HIST - 3:HIST - 2, ls]
    return even + jnp.where(row == 0, odd_before, pltpu.roll(odd, 1, axis=0))


def _tril_mask(n, strict=False):
    r = lax.broadcasted_iota(jnp.int32, (n, n), 0)
    c = lax.broadcasted_iota(jnp.int32, (n, n), 1)
    return (r > c) if strict else (r >= c)


def _chunk_tril(n):
    r = lax.broadcasted_iota(jnp.int32, (n, n), 0)
    c = lax.broadcasted_iota(jnp.int32, (n, n), 1)
    same = (r // CHUNK) == (c // CHUNK)
    return jnp.where(jnp.logical_and(r >= c, same), 1.0, 0.0).astype(bf16)


def _norm_body(x_ref, w_ref, n_ref):
    x = x_ref[...]
    y = x * lax.rsqrt(jnp.mean(x * x, axis=-1, keepdims=True) + EPS)
    n_ref[...] = (y * w_ref[...]).astype(n_ref.dtype)


def _rms_norm(x, w_row, out_dtype):
    M, D = x.shape
    tm = _pick(M, (416, 320, 256, 128, 64, 16))
    row = pl.BlockSpec((tm, D), lambda m: (m, 0))
    return pl.pallas_call(
        _norm_body, grid=(M // tm,), in_specs=[row, pl.BlockSpec((1, D), lambda m: (0, 0))], out_specs=row,
        out_shape=jax.ShapeDtypeStruct((M, D), out_dtype),
        compiler_params=_params("arbitrary"), name="rms_norm")(x, w_row)


def _join_norm_body(n_first, xp_ref, xs_ref, w_ref, x_ref, n_ref):
    x = jnp.where(pl.program_id(0) < n_first, xp_ref[...], xs_ref[...])
    x_ref[...] = x
    y = x * lax.rsqrt(jnp.mean(x * x, axis=-1, keepdims=True) + EPS)
    n_ref[...] = (y * w_ref[...]).astype(n_ref.dtype)


def _join_norm(xp, xs, w_row):
    PT, D = xp.shape
    DB = xs.shape[0]
    tm = _pick(DB, (128, 64, 16))
    assert PT % tm == 0
    n_first = PT // tm
    row = pl.BlockSpec((tm, D), lambda i: (i, 0))
    return pl.pallas_call(
        functools.partial(_join_norm_body, n_first),
        grid=((PT + DB) // tm,),
        in_specs=[pl.BlockSpec((tm, D), lambda i: (jnp.minimum(i, n_first - 1), 0)),
                  pl.BlockSpec((tm, D), lambda i: (jnp.maximum(i - n_first, 0), 0)),
                  pl.BlockSpec((1, D), lambda i: (0, 0))],
        out_specs=[row, row],
        out_shape=[jax.ShapeDtypeStruct((PT + DB, D), f32), jax.ShapeDtypeStruct((PT + DB, D), bf16)],
        compiler_params=_params("arbitrary"), name="join_norm")(xp, xs, w_row)


def _norm_split_body(n_first, x_ref, w_ref, yp_ref, ys_ref):
    i = pl.program_id(0)
    x = x_ref[...]
    y = x * lax.rsqrt(jnp.mean(x * x, axis=-1, keepdims=True) + EPS) * w_ref[...]

    @pl.when(i < n_first)
    def _():
        yp_ref[...] = y

    @pl.when(i >= n_first)
    def _():
        ys_ref[...] = y


def _norm_split(x, w_row, PT):
    M, D = x.shape
    tm = _pick(M - PT, (128, 64, 16))
    assert PT % tm == 0
    n_first = PT // tm
    return pl.pallas_call(
        functools.partial(_norm_split_body, n_first),
        grid=(M // tm,),
        in_specs=[pl.BlockSpec((tm, D), lambda i: (i, 0)), pl.BlockSpec((1, D), lambda i: (0, 0))],
        out_specs=[pl.BlockSpec((tm, D), lambda i: (jnp.minimum(i, n_first - 1), 0)),
                   pl.BlockSpec((tm, D), lambda i: (jnp.maximum(i - n_first, 0), 0))],
        out_shape=[jax.ShapeDtypeStruct((PT, D), f32), jax.ShapeDtypeStruct((M - PT, D), f32)],
        compiler_params=_params("arbitrary"), name="norm_split")(x, w_row)


def _gemm_wres_body(epi, w_is_nk, has_res, a_ref, w_ref, *rest):
    res_ref = rest[0] if has_res else None
    o_ref, wb = rest[-2:]

    @pl.when(pl.program_id(1) == 0)
    def _():
        w = w_ref[...]
        wb[...] = (w.T if w_is_nk else w).astype(bf16)
    acc = jnp.dot(a_ref[...], wb[...], preferred_element_type=f32)
    if epi is not None:
        acc = epi(acc)
    if has_res:
        acc = res_ref[...] + acc
    o_ref[...] = acc.astype(o_ref.dtype)


BIG_TM = (1040, 640, 512, 320, 256, 128, 64, 16)


def _gemm_wres(a, w, layer, col_block0, n_out, tn, epi=None, out_dtype=f32, name="gemm", w_is_nk=False,
               tm_cands=(640, 512, 320, 256, 128, 64, 16), single_buffer_w=False, res=None):
    M, K = a.shape
    tm = _pick(M, tm_cands)
    mode = dict(pipeline_mode=pl.Buffered(1)) if single_buffer_w else {}
    if w_is_nk:
        w_spec = pl.BlockSpec((None, tn, K), lambda n, m: (layer, n + col_block0, 0), **mode)
    else:
        w_spec = pl.BlockSpec((None, K, tn), lambda n, m: (layer, 0, n + col_block0), **mode)
    tile = pl.BlockSpec((tm, tn), lambda n, m: (m, n))
    return pl.pallas_call(
        functools.partial(_gemm_wres_body, epi, w_is_nk, res is not None),
        grid=(n_out // tn, M // tm),
        in_specs=[pl.BlockSpec((tm, K), lambda n, m: (m, 0)), w_spec] + ([tile] if res is not None else []),
        out_specs=tile,
        out_shape=jax.ShapeDtypeStruct((M, n_out), out_dtype),
        scratch_shapes=[pltpu.VMEM((K, tn), bf16)],
        compiler_params=_params("arbitrary", "arbitrary"), name=name)(a, w, *([res] if res is not None else []))


def _proj_norm_body(a_ref, w_ref, x_ref, nw_ref, xo_ref, n_ref, wb):
    @pl.when(pl.program_id(0) == 0)
    def _():
        wb[...] = w_ref[...].astype(bf16)
    x = x_ref[...] + jnp.dot(a_ref[...], wb[...], preferred_element_type=f32)
    xo_ref[...] = x
    y = x * lax.rsqrt(jnp.mean(x * x, axis=-1, keepdims=True) + EPS)
    n_ref[...] = (y * nw_ref[...]).astype(n_ref.dtype)


def _proj_add_norm(a, w, layer, x, nw_row):
    M, K = a.shape
    D = w.shape[-1]
    tm = _pick(M, (416, 320, 256, 128, 64, 16))
    row = lambda width: pl.BlockSpec((tm, width), lambda m: (m, 0))
    return pl.pallas_call(
        _proj_norm_body,
        grid=(M // tm,),
        in_specs=[row(K), pl.BlockSpec((None, K, D), lambda m: (layer, 0, 0), pipeline_mode=pl.Buffered(1)),
                  row(D), pl.BlockSpec((1, D), lambda m: (0, 0))],
        out_specs=[row(D), row(D)],
        out_shape=[jax.ShapeDtypeStruct((M, D), f32), jax.ShapeDtypeStruct((M, D), bf16)],
        scratch_shapes=[pltpu.VMEM((K, D), bf16)],
        compiler_params=_params("arbitrary"), name="out_proj_norm")(a, w, x, nw_row)


MERGE_SHIFT = 16


def _mix_body(u_ref, yr_ref, yh_ref, yg_ref, wm0, wm1, wm2, wx0, wx1, wx2, wr_ref, wh_ref, wg_ref, o_ref, wmb, wbb):
    tn = o_ref.shape[1]

    @pl.when(pl.program_id(1) == 0)
    def _():
        for b, (wm, wx) in enumerate(((wm0, wx0), (wm1, wx1), (wm2, wx2))):
            wcat = jnp.concatenate([wm[...], wx[...]], axis=0)
            wmb[b] = wcat[MERGE_SHIFT:MERGE_SHIFT + tn].T.astype(bf16)
        for b, wr in enumerate((wr_ref, wh_ref, wg_ref)):
            wbb[b] = wr[...].astype(bf16)

    u = u_ref[...]
    acc = None
    for b, y_ref in enumerate((yr_ref, yh_ref, yg_ref)):
        gate = jax.nn.sigmoid(jnp.dot(u, wmb[b], preferred_element_type=f32))
        p = jnp.dot(y_ref[...], wbb[b], preferred_element_type=f32)
        acc = gate * p if acc is None else acc + gate * p
    o_ref[...] = acc.astype(o_ref.dtype)


def _mix(u, y_rg, y_hg, y_gd, w_in_t, w_br_rg, w_br_hg, w_br_gd, layer, merge_col0):
    M, D = u.shape
    W = y_rg.shape[1]
    tn = 512
    tm = _pick(M, (640, 416, 320, 256, 128, 64, 16))
    nt = D // tn
    assert (merge_col0 - MERGE_SHIFT) % tn == 0 and tn % MERGE_SHIFT == 0
    base = (merge_col0 - MERGE_SHIFT) // tn
    r = tn // MERGE_SHIFT
    once = dict(pipeline_mode=pl.Buffered(1))

    def wm_spec(b):
        return pl.BlockSpec((None, tn, D), lambda n, m: (layer, base + b * nt + n, 0), **once)

    def wx_spec(b):
        return pl.BlockSpec((None, MERGE_SHIFT, D), lambda n, m: (layer, (base + b * nt + n + 1) * r, 0), **once)

    row = lambda w: pl.BlockSpec((tm, w), lambda n, m: (m, 0))
    br = pl.BlockSpec((None, W, tn), lambda n, m: (layer, 0, n), **once)
    return pl.pallas_call(
        _mix_body,
        grid=(nt, M // tm),
        in_specs=[row(D), row(W), row(W), row(W), wm_spec(0), wm_spec(1), wm_spec(2),
                  wx_spec(0), wx_spec(1), wx_spec(2), br, br, br],
        out_specs=pl.BlockSpec((tm, tn), lambda n, m: (m, n)),
        out_shape=jax.ShapeDtypeStruct((M, D), bf16),
        scratch_shapes=[pltpu.VMEM((3, D, tn), bf16), pltpu.VMEM((3, W, tn), bf16)],
        compiler_params=_params("arbitrary", "arbitrary"), name="mix")(
            u, y_rg, y_hg, y_gd, w_in_t, w_in_t, w_in_t, w_in_t, w_in_t, w_in_t, w_br_rg, w_br_hg, w_br_gd)


def _lb_body(x_ref, o_ref):
    x = x_ref[...]
    depth = x.shape[0]
    m = jnp.max(x, axis=0, keepdims=True)
    e = jnp.exp(x - m)
    p = e / jnp.sum(e, axis=0, keepdims=True)
    acc = jnp.zeros_like(p[0:1])
    o_ref[0:1, :] = acc
    for l in range(1, depth):
        acc = acc + p[l:l + 1]
        o_ref[l:l + 1, :] = acc


def _lower_bounds(logits):
    return pl.pallas_call(_lb_body, out_shape=jax.ShapeDtypeStruct(logits.shape, f32), name="hg_lower_bounds")(logits)


def _rg_gates(xc, wa, wx, ba, bx, sp):
    xb = xc.astype(bf16)
    r = jax.nn.sigmoid(jnp.dot(xb, wa.astype(bf16), preferred_element_type=f32) + ba)
    i = jax.nn.sigmoid(jnp.dot(xb, wx.astype(bf16), preferred_element_type=f32) + bx)
    log_a = (-RG_C) * r * sp
    a = jnp.exp(log_a)
    m2 = _expm1_neg(2.0 * log_a, a * a)
    mult = jnp.where(m2 > 0.0, m2 * lax.rsqrt(m2), 0.0)
    return a, mult, i


def _rg_prompt_body(x_ref, gate_ref, cw_ref, cb_ref, wa_ref, wx_ref, ba_ref, bx_ref, ap_ref, y_ref, h_ref, xbuf, hprev):
    c = pl.program_id(1)
    tc = x_ref.shape[0]
    nblk = x_ref.shape[1] // HEAD

    @pl.when(c == 0)
    def _():
        xbuf[0:HIST, :] = jnp.zeros((HIST, xbuf.shape[1]), f32)
        hprev[...] = jnp.zeros_like(hprev)

    xbuf[HIST:HIST + tc, :] = x_ref[...]
    row = lax.broadcasted_iota(jnp.int32, (tc, HEAD), 0)
    first = jnp.logical_and(row == 0, c == 0)
    for n in range(nblk):
        ls = slice(n * HEAD, (n + 1) * HEAD)
        xc = _causal_conv4(xbuf, cw_ref, tc, ls) + cb_ref[:, ls]
        sp = _softplus(-ap_ref[:, ls])
        a, mult, i = _rg_gates(xc, wa_ref[n], wx_ref[n], ba_ref[:, ls], bx_ref[:, ls], sp)
        mult = jnp.where(first, 1.0, mult)
        b = mult * (i * xc)
        s = 1
        while s < SUBLANES:
            keep = (row % SUBLANES) >= s
            a_sh = jnp.where(keep, pltpu.roll(a, s, axis=0), 1.0)
            b_sh = jnp.where(keep, pltpu.roll(b, s, axis=0), 0.0)
            b = a * b_sh + b
            a = a * a_sh
            s *= 2
        carry = hprev[:, ls]
        groups = []
        for g in range(tc // SUBLANES):
            rows = slice(g * SUBLANES, (g + 1) * SUBLANES)
            hg = b[rows] + a[rows] * carry
            groups.append(hg)
            carry = hg[SUBLANES - 1:SUBLANES, :]
        h = jnp.concatenate(groups, axis=0)
        hprev[:, ls] = carry
        y_ref[:, ls] = (h * jax.nn.gelu(gate_ref[:, ls], approximate=True)).astype(y_ref.dtype)
    xbuf[0:HIST, :] = xbuf[tc:tc + HIST, :]
    h_ref[...] = hprev[...]


def _rg_prompt(proj, B, T, M, lp, layer):
    W = lp["rg_ba"].shape[-1]
    tc = _pick(T, (256, 128, 64))
    nT = T // tc
    nblk = W // HEAD
    vec = pl.BlockSpec((None, 1, W), lambda b, c: (layer, 0, 0))
    blk = pl.BlockSpec((None, nblk, HEAD, HEAD), lambda b, c: (layer, 0, 0, 0))
    y, h = pl.pallas_call(
        _rg_prompt_body,
        grid=(B, nT),
        in_specs=[pl.BlockSpec((tc, W), lambda b, c: (b * nT + c, 0)),
                  pl.BlockSpec((tc, W), lambda b, c: (b * nT + c, 1)),
                  pl.BlockSpec((None, CONV_W, W), lambda b, c: (layer, 0, 0)),
                  vec, blk, blk, vec, vec, vec],
        out_specs=[pl.BlockSpec((tc, W), lambda b, c: (b * nT + c, 0)),
                   pl.BlockSpec((None, 1, W), lambda b, c: (b, 0, 0))],
        out_shape=[jax.ShapeDtypeStruct((M, W), bf16), jax.ShapeDtypeStruct((B, 1, W), f32)],
        scratch_shapes=[pltpu.VMEM((HIST + tc, W), f32), pltpu.VMEM((1, W), f32)],
        compiler_params=_params("arbitrary", "arbitrary"), name="rg_prompt")(
            proj, proj, lp["rg_conv_w"], lp["rg_conv_b"], lp["rg_wa"], lp["rg_wx"], lp["rg_ba"], lp["rg_bx"],
            lp["rg_a_param"])
    return y, h[:, 0]


def _rg_sample_body(x_ref, gate_ref, cs_ref, h0_ref, cw_ref, cb_ref, wa_ref, wx_ref, ba_ref, bx_ref, ap_ref,
                    yin_ref, y_ref, h_ref):
    del yin_ref
    nblk = x_ref.shape[1] // HEAD
    for n in range(nblk):
        ls = slice(n * HEAD, (n + 1) * HEAD)
        xc = cb_ref[:, ls] + cw_ref[CONV_W - 1:CONV_W, ls] * x_ref[:, ls]
        for j in range(CONV_W - 1):
            xc = xc + cw_ref[j:j + 1, ls] * cs_ref[j, :, ls]
        sp = _softplus(-ap_ref[:, ls])
        a, mult, i = _rg_gates(xc, wa_ref[n], wx_ref[n], ba_ref[:, ls], bx_ref[:, ls], sp)
        h = a * h0_ref[:, ls] + mult * (i * xc)
        h_ref[:, ls] = h
        y_ref[:, ls] = (h * jax.nn.gelu(gate_ref[:, ls], approximate=True)).astype(y_ref.dtype)


def _rg_sample(proj, y_all, row0, DB, conv_state_t, h0, lp, layer):
    W = lp["rg_ba"].shape[-1]
    nblk = W // HEAD
    assert row0 % DB == 0
    rb = row0 // DB
    vec = pl.BlockSpec((None, 1, W), lambda i: (layer, 0, 0))
    blk = pl.BlockSpec((None, nblk, HEAD, HEAD), lambda i: (layer, 0, 0, 0))
    y, h = pl.pallas_call(
        _rg_sample_body,
        grid=(1,),
        in_specs=[pl.BlockSpec((DB, W), lambda i: (rb, 0)),
                  pl.BlockSpec((DB, W), lambda i: (rb, 1)),
                  pl.BlockSpec((None, CONV_W - 1, DB, W), lambda i: (layer, 0, 0, 0)),
                  pl.BlockSpec((None, DB, W), lambda i: (layer, 0, 0)),
                  pl.BlockSpec((None, CONV_W, W), lambda i: (layer, 0, 0)),
                  vec, blk, blk, vec, vec, vec,
                  pl.BlockSpec(memory_space=pl.ANY)],
        out_specs=[pl.BlockSpec((DB, W), lambda i: (rb, 0)),
                   pl.BlockSpec((DB, W), lambda i: (0, 0))],
        out_shape=[jax.ShapeDtypeStruct(y_all.shape, y_all.dtype), jax.ShapeDtypeStruct((DB, W), f32)],
        input_output_aliases={11: 0},
        compiler_params=_params("arbitrary"), name="rg_sample")(
            proj, proj, conv_state_t, h0, lp["rg_conv_w"], lp["rg_conv_b"], lp["rg_wa"], lp["rg_wx"],
            lp["rg_ba"], lp["rg_bx"], lp["rg_a_param"], y_all)
    return y, h


def _hg_gates(fx, lb):
    f = lb + (1.0 - lb) * jax.nn.sigmoid(fx)
    k = (1.0 - lb) * jax.nn.sigmoid(-fx)
    return f, k


def _hg_intra_diag(G, q, k):
    lane = lax.broadcasted_iota(jnp.int32, (SUB, CHUNK), 1)
    blocks = []
    for i in range(CHUNK // SUB):
        sl = slice(i * SUB, (i + 1) * SUB)
        g_i, q_i, k_i = G[sl], q[sl], k[sl]
        a_d = jnp.zeros((SUB, CHUNK), f32)
        for s in range(SUB):
            e = jnp.exp(g_i - g_i[s:s + 1, :])
            col = jnp.sum(q_i * k_i[s:s + 1, :] * e, axis=-1, keepdims=True)
            a_d = jnp.where(lane == i * SUB + s, col, a_d)
        blocks.append(a_d)
    return jnp.where(_tril_mask(CHUNK), jnp.concatenate(blocks, axis=0), 0.0)


def _hg_intra_off(G, q, k):
    nsub = CHUNK // SUB
    row = lax.broadcasted_iota(jnp.int32, (CHUNK, HEAD), 0)
    q_parts, k_parts = [], []
    for j in range(nsub - 1):
        g_e = G[(j + 1) * SUB - 1:(j + 1) * SUB, :]
        q_parts.append(jnp.where(row >= (j + 1) * SUB, q * jnp.exp(G - g_e), 0.0))
        in_j = jnp.logical_and(row >= j * SUB, row < (j + 1) * SUB)
        k_parts.append(jnp.where(in_j, k * jnp.exp(g_e - G), 0.0))
    return _mm(jnp.concatenate(q_parts, axis=1), jnp.concatenate(k_parts, axis=1), _NT)


def _hg_prompt_body(scale, HB, q_ref, f_ref, i_ref, g_ref, lb_ref, nw_ref, y_ref, s_ref, S_scr):
    c = pl.program_id(2)
    nchunk = q_ref.shape[0] // CHUNK

    @pl.when(c == 0)
    def _():
        S_scr[...] = jnp.zeros_like(S_scr)

    nw = nw_ref[...]
    tril = _chunk_tril(nchunk * CHUNK)
    work = []
    for hh in range(HB):
        ls = slice(hh * HEAD, (hh + 1) * HEAD)
        f_all, k_all = _hg_gates(f_ref[:, ls], lb_ref[:, ls])
        q_all = q_ref[:, ls] * scale
        G_all = _mm_exact_lhs(tril, jnp.log(f_all))
        for ci in range(nchunk):
            rows = slice(ci * CHUNK, (ci + 1) * CHUNK)
            work.append(dict(hh=hh, ls=ls, rows=rows, G=G_all[rows], q=q_all[rows], k=k_all[rows]))
    for w in work:
        w["A"] = _hg_intra_diag(w["G"], w["q"], w["k"])
    for w in work:
        w["A"] = w["A"] + _hg_intra_off(w["G"], w["q"], w["k"])
    for w in work:
        G = w["G"]
        kT, GT = w["k"].T, G.T
        g_last = GT[:, CHUNK - 1:CHUNK]
        w["dec"] = jnp.exp(g_last)
        w["upd"] = _mm(kT * jnp.exp(g_last - GT), i_ref[w["rows"], w["ls"]])
        w["lhs"] = jnp.concatenate([w["A"], w["q"] * jnp.exp(G)], axis=1)
    S = [S_scr[hh] for hh in range(HB)]
    for ci in range(nchunk):
        for hh in range(HB):
            w = work[hh * nchunk + ci]
            rows, ls = w["rows"], w["ls"]
            o = _mm(w["lhs"], jnp.concatenate([i_ref[rows, ls], S[hh]], axis=0))
            S[hh] = S[hh] * w["dec"] + w["upd"]
            y_ref[rows, ls] = _gated_rms(o, nw, g_ref[rows, ls]).astype(y_ref.dtype)
    for hh in range(HB):
        S_scr[hh] = S[hh]

    @pl.when(c == pl.num_programs(2) - 1)
    def _():
        for hh in range(HB):
            s_ref[hh] = S[hh]


def _hg_prompt(proj, B, T, M, H, col0, lb, norm_w, layer):
    tc = _pick(T, (256, 128, 64))
    nT = T // tc
    HB = HEADS_PER_STEP
    assert H % HB == 0 and col0 % HB == 0
    wb = HB * HEAD
    col = lambda j: pl.BlockSpec((tc, wb), lambda b, h, c: (b * nT + c, (col0 + j * H) // HB + h))
    y, S = pl.pallas_call(
        functools.partial(_hg_prompt_body, HEAD ** -0.5, HB),
        grid=(B, H // HB, nT),
        in_specs=[col(0), col(1), col(2), col(3),
                  pl.BlockSpec((None, 1, wb), lambda b, h, c: (layer, 0, h)),
                  pl.BlockSpec((None, 1, HEAD), lambda b, h, c: (layer, 0, 0))],
        out_specs=[pl.BlockSpec((tc, wb), lambda b, h, c: (b * nT + c, h)),
                   pl.BlockSpec((None, HB, HEAD, HEAD), lambda b, h, c: (b, h, 0, 0))],
        out_shape=[jax.ShapeDtypeStruct((M, H * HEAD), bf16), jax.ShapeDtypeStruct((B, H, HEAD, HEAD), f32)],
        scratch_shapes=[pltpu.VMEM((HB, HEAD, HEAD), f32)],
        compiler_params=_params("arbitrary", "arbitrary", "arbitrary"), name="hg_prompt")(
            proj, proj, proj, proj, lb, norm_w)
    return y, S


SB = 16


def _state_step(s_ref, so_ref, o_scr, h, lhs, kT, update_fn):
    lhs_b = lhs.astype(bf16)
    prods = [jnp.dot(lhs_b, s_ref[j, h].astype(bf16), preferred_element_type=f32) for j in range(SB)]
    for j in range(SB):
        d, vnew, o = update_fn(j, prods[j])
        so_ref[j, h] = d * s_ref[j, h] + kT[:, j:j + 1] * vnew
        o_scr[h, j:j + 1, :] = o


def _hg_sample_body(scale, H, q_ref, f_ref, i_ref, g_ref, lb_ref, nw_ref, s_ref, *rest):
    y_ref, so_ref, o_scr = rest[-3:]

    def head(h, carry):
        ls = pl.ds(pl.multiple_of(h * HEAD, HEAD), HEAD)
        f, k = _hg_gates(f_ref[:, ls], lb_ref[:, ls])
        q = q_ref[:, ls] * scale
        v = i_ref[:, ls]
        fT, kT = f.T, k.T
        qk = jnp.sum(q * k, axis=-1, keepdims=True)

        def update(j, qfS):
            vj = v[j:j + 1, :]
            return fT[:, j:j + 1], vj, qfS[j:j + 1, :] + qk[j:j + 1, :] * vj

        _state_step(s_ref, so_ref, o_scr, h, q * f, kT, update)
        y_ref[:, ls] = _gated_rms(o_scr[h], nw_ref[...], g_ref[:, ls]).astype(y_ref.dtype)
        return carry

    lax.fori_loop(0, H, head, 0, unroll=2)


def _state_out(state, stacked_prev, n_in):
    extra_in, extra_specs, aliases = [], [], {}
    if stacked_prev is not None:
        extra_in, extra_specs, aliases = [stacked_prev], [pl.BlockSpec(memory_space=pl.ANY)], {n_in: 1}
    return jax.ShapeDtypeStruct(state.shape, f32), extra_in, extra_specs, aliases


def _hg_sample(proj, y_all, row0, DB, H, col0, lb, norm_w, state, stacked_prev, layer):
    assert row0 % SB == 0 and DB % SB == 0 and col0 % H == 0
    rb = row0 // SB
    wh = H * HEAD
    col = lambda j: pl.BlockSpec((SB, wh), lambda b: (rb + b, col0 // H + j))
    st_spec = pl.BlockSpec((None, SB, H, HEAD, HEAD), lambda b: (layer, b, 0, 0, 0))
    s_shape, extra_in, extra_specs, aliases = _state_out(state, stacked_prev, 8)
    y, S = pl.pallas_call(
        functools.partial(_hg_sample_body, HEAD ** -0.5, H),
        grid=(DB // SB,),
        in_specs=[col(0), col(1), col(2), col(3),
                  pl.BlockSpec((None, 1, wh), lambda b: (layer, 0, 0)),
                  pl.BlockSpec((None, 1, HEAD), lambda b: (layer, 0, 0)),
                  st_spec,
                  pl.BlockSpec(memory_space=pl.ANY)] + extra_specs,
        out_specs=[pl.BlockSpec((SB, wh), lambda b: (rb + b, 0)), st_spec],
        out_shape=[jax.ShapeDtypeStruct(y_all.shape, y_all.dtype), s_shape],
        scratch_shapes=[pltpu.VMEM((H, SB, HEAD), f32)],
        input_output_aliases={7: 0, **aliases},
        compiler_params=_params("arbitrary"), name="hg_sample")(
            proj, proj, proj, proj, lb, norm_w, state, y_all, *extra_in)
    return y, S


def _pick_lane(x, idx):
    lane = lax.broadcasted_iota(jnp.int32, x.shape, 1)
    col = jnp.sum(jnp.where(lane == idx, x, 0.0), axis=1, keepdims=True)
    return jnp.broadcast_to(col, (x.shape[0], HEAD))


def _gd_gate_body(H, PT, u_ref, w_ref, alog_ref, dtb_ref, o_ref, wb):
    i = pl.program_id(0)
    tm = u_ref.shape[0]

    @pl.when(i == 0)
    def _():
        wb[...] = w_ref[...].T.astype(bf16)

    x = jnp.dot(u_ref[...], wb[...], preferred_element_type=f32)
    lane = lax.broadcasted_iota(jnp.int32, x.shape, 1)
    row = lax.broadcasted_iota(jnp.int32, x.shape, 0) + i * tm
    g = jnp.where(lane < H, -jnp.exp(alog_ref[...]) * _softplus(x + dtb_ref[...]), 0.0)
    blk = min(2 * CHUNK, tm)
    tril = _chunk_tril(blk)
    G = jnp.concatenate([_mm_exact_lhs(tril, g[r:r + blk]) for r in range(0, tm, blk)], axis=0)
    G = jnp.where(row < PT, G, g)
    o_ref[...] = jnp.where(lane < H, G, jax.nn.sigmoid(x))


def _gd_gate_proj(u, w_in_t, row_block, alog_pad, dtb_pad, PT, H, layer):
    M, D = u.shape
    tm = _pick(M, (640, 320, 128, 64))
    assert PT % CHUNK == 0 and tm % CHUNK == 0
    vec = pl.BlockSpec((None, 1, LANE), lambda i: (layer, 0, 0))
    return pl.pallas_call(
        functools.partial(_gd_gate_body, H, PT),
        grid=(M // tm,),
        in_specs=[pl.BlockSpec((tm, D), lambda i: (i, 0)),
                  pl.BlockSpec((None, LANE, D), lambda i: (layer, row_block, 0)), vec, vec],
        out_specs=pl.BlockSpec((tm, LANE), lambda i: (i, 0)),
        out_shape=jax.ShapeDtypeStruct((M, LANE), f32),
        scratch_shapes=[pltpu.VMEM((D, LANE), bf16)],
        compiler_params=_params("arbitrary"), name="gd_gate_proj")(u, w_in_t, alog_pad, dtb_pad)


def _unit_lower_inverses(Ns):
    r = lax.broadcasted_iota(jnp.int32, (CHUNK, CHUNK), 0)
    c = lax.broadcasted_iota(jnp.int32, (CHUNK, CHUNK), 1)
    same = lambda n: (r // n) == (c // n)
    assert CHUNK == 4 * INV_BLOCK
    dot = functools.partial(jnp.dot, preferred_element_type=f32)
    cast = lambda xs: [x.astype(bf16) for x in xs]
    Rs = [jnp.where(same(INV_BLOCK), N, 0.0) for N in Ns]
    Rb = cast(Rs)
    Ps = [dot(rb, rb) for rb in Rb]
    p = 2
    while p < INV_BLOCK:
        Pb = cast(Ps)
        if 2 * p < INV_BLOCK:
            PMs = [dot(pb, jnp.concatenate([rb, pb], axis=1)) for rb, pb in zip(Rb, Pb)]
            Rs = [R + P + PM[:, :CHUNK] for R, P, PM in zip(Rs, Ps, PMs)]
            Ps = [PM[:, CHUNK:] for PM in PMs]
            Rb = cast(Rs)
        else:
            PRs = [dot(pb, rb) for rb, pb in zip(Rb, Pb)]
            Rs = [R + P + PR for R, P, PR in zip(Rs, Ps, PRs)]
        p *= 2
    eye = jnp.where(r == c, 1.0, 0.0)
    Nb = cast(Ns)
    for n in (2 * INV_BLOCK, 4 * INV_BLOCK):
        off = jnp.logical_and(same(n), jnp.logical_not(same(n // 2)))
        Db = cast([eye + R for R in Rs])
        DCs = [dot(db, jnp.where(off, nb, jnp.zeros_like(nb))) for db, nb in zip(Db, Nb)]
        DCDs = [dot(dc, db) for dc, db in zip(cast(DCs), Db)]
        Rs = [R + DCD for R, DCD in zip(Rs, DCDs)]
    return Rs


def _gd_prompt_body(scale, H, HB, q_ref, k_ref, v_ref, z_ref, gt_ref, cwq_ref, cwk_ref, cwv_ref, nw_ref,
                    y_ref, s_ref, xq, xk, xv, S_scr):
    hb = pl.program_id(1)
    c = pl.program_id(2)
    tc = q_ref.shape[0]
    nchunk = tc // CHUNK

    @pl.when(c == 0)
    def _():
        for xb in (xq, xk, xv):
            xb[0:HIST, :] = jnp.zeros((HIST, xb.shape[1]), f32)
        S_scr[...] = jnp.zeros_like(S_scr)

    def conv_silu(x_ref, xb, cw_ref):
        xb[HIST:HIST + tc, :] = x_ref[...]
        y = _causal_conv4(xb, cw_ref, tc)
        xb[0:HIST, :] = xb[tc:tc + HIST, :]
        return _silu(y)

    qc = conv_silu(q_ref, xq, cwq_ref)
    kc = conv_silu(k_ref, xk, cwk_ref)
    vc = conv_silu(v_ref, xv, cwv_ref)
    gt = gt_ref[...]
    nw = nw_ref[...]
    tril_b = _tril_mask(CHUNK)
    strict_b = _tril_mask(CHUNK, strict=True)

    work = []
    for hh in range(HB):
        ls = slice(hh * HEAD, (hh + 1) * HEAD)
        h = hb * HB + hh
        q_all = _l2norm(qc[:, ls]) * scale
        k_all = _l2norm(kc[:, ls])
        G_all = _pick_lane(gt, h)
        beta = _pick_lane(gt, H + h)
        eG = jnp.exp(G_all)
        kb_all = k_all * beta
        rhs_all = jnp.concatenate([vc[:, ls] * beta, kb_all * eG], axis=1)
        qe_all = q_all * eG
        for ci in range(nchunk):
            rows = slice(ci * CHUNK, (ci + 1) * CHUNK)
            work.append(dict(hh=hh, ls=ls, rows=rows, G=G_all[rows], q=q_all[rows], k=k_all[rows], kb=kb_all[rows],
                             X=rhs_all[rows], qe=qe_all[rows]))
    for w in work:
        G = w["G"]
        w["decay"] = jnp.where(tril_b, jnp.exp(G[:, :CHUNK] - G.T[:CHUNK, :]), 0.0)
        w["KQ"] = _mm(jnp.concatenate([w["kb"], w["q"]], axis=0), w["k"], _NT)
    Rs = _unit_lower_inverses([jnp.where(strict_b, -(w["KQ"][:CHUNK] * w["decay"]), 0.0) for w in work])
    for w, R in zip(work, Rs):
        w["R"] = R
        w["qk"] = jnp.where(tril_b, w["KQ"][CHUNK:] * w["decay"], 0.0)
    for w in work:
        X = w["X"]
        w["X"] = X + _mm(w["R"], X)
        G = w["G"]
        g_last = G[CHUNK - 1:CHUNK, :]
        w["egl"] = jnp.exp(g_last)
        w["rhs2"] = jnp.concatenate([w["qk"], (w["k"] * jnp.exp(g_last - G)).T], axis=0)
    S = [S_scr[hh] for hh in range(HB)]
    for ci in range(nchunk):
        for hh in range(HB):
            w = work[hh * nchunk + ci]
            rows, ls = w["rows"], w["ls"]
            WS = _mm(jnp.concatenate([w["X"][:, HEAD:], w["qe"]], axis=0), S[hh])
            v_new = w["X"][:, :HEAD] - WS[:CHUNK]
            OS = _mm(w["rhs2"], v_new)
            S[hh] = w["egl"] * S[hh] + OS[CHUNK:]
            y_ref[rows, ls] = _gated_rms(WS[CHUNK:] + OS[:CHUNK], nw, z_ref[rows, ls]).astype(y_ref.dtype)
    for hh in range(HB):
        S_scr[hh] = S[hh]

    @pl.when(c == pl.num_programs(2) - 1)
    def _():
        for hh in range(HB):
            s_ref[hh] = S[hh]


def _gd_prompt(proj, gates, B, T, M, H, col0, lp, layer):
    tc = _pick(T, (256, 128, 64))
    nT = T // tc
    HB = HEADS_PER_STEP
    assert H % HB == 0 and col0 % HB == 0
    wb = HB * HEAD
    col = lambda j: pl.BlockSpec((tc, wb), lambda b, h, c: (b * nT + c, (col0 + j * H) // HB + h))
    cw = lambda j: pl.BlockSpec((None, CONV_W, wb), lambda b, h, c: (layer, 0, j * H // HB + h))
    buf = pltpu.VMEM((HIST + tc, wb), f32)
    y, S = pl.pallas_call(
        functools.partial(_gd_prompt_body, HEAD ** -0.5, H, HB),
        grid=(B, H // HB, nT),
        in_specs=[col(0), col(1), col(2), col(3),
                  pl.BlockSpec((tc, LANE), lambda b, h, c: (b * nT + c, 0)),
                  cw(0), cw(1), cw(2),
                  pl.BlockSpec((None, 1, HEAD), lambda b, h, c: (layer, 0, 0))],
        out_specs=[pl.BlockSpec((tc, wb), lambda b, h, c: (b * nT + c, h)),
                   pl.BlockSpec((None, HB, HEAD, HEAD), lambda b, h, c: (b, h, 0, 0))],
        out_shape=[jax.ShapeDtypeStruct((M, H * HEAD), bf16), jax.ShapeDtypeStruct((B, H, HEAD, HEAD), f32)],
        scratch_shapes=[buf, buf, buf, pltpu.VMEM((HB, HEAD, HEAD), f32)],
        compiler_params=_params("arbitrary", "arbitrary", "arbitrary"), name="gd_prompt")(
            proj, proj, proj, proj, gates, lp["gd_conv_w"], lp["gd_conv_w"], lp["gd_conv_w"], lp["gd_norm_w"])
    return y, S


def _gd_sample_body(scale, H, q_ref, k_ref, v_ref, z_ref, gt_ref, csq_ref, csk_ref, csv_ref, cwq_ref, cwk_ref, cwv_ref,
                    nw_ref, s_ref, *rest):
    y_ref, so_ref, o_scr = rest[-3:]
    gt = gt_ref[...]

    def head(h):
        ls = slice(h * HEAD, (h + 1) * HEAD)

        def conv_silu(x_ref, cs_ref, cw_ref):
            y = cw_ref[CONV_W - 1:CONV_W, ls] * x_ref[:, ls]
            for j in range(CONV_W - 1):
                y = y + cw_ref[j:j + 1, ls] * cs_ref[j, :, ls]
            return _silu(y)

        q = _l2norm(conv_silu(q_ref, csq_ref, cwq_ref)) * scale
        k = _l2norm(conv_silu(k_ref, csk_ref, cwk_ref))
        v = conv_silu(v_ref, csv_ref, cwv_ref)
        eg = jnp.exp(_pick_lane(gt, h))
        beta = _pick_lane(gt, H + h)
        kT = k.T
        qk = jnp.sum(q * k, axis=-1, keepdims=True)

        def update(j, W):
            egj = eg[j:j + 1, :]
            vnew = beta[j:j + 1, :] * (v[j:j + 1, :] - egj * W[SB + j:SB + j + 1, :])
            return egj, vnew, egj * W[j:j + 1, :] + qk[j:j + 1, :] * vnew

        _state_step(s_ref, so_ref, o_scr, h, jnp.concatenate([q, k], axis=0), kT, update)
        y_ref[:, ls] = _gated_rms(o_scr[h], nw_ref[...], z_ref[:, ls]).astype(y_ref.dtype)
    for h in range(H):
        head(h)


def _gd_sample(proj, gates, y_all, row0, DB, H, col0, conv_state_t, state, stacked_prev, lp, layer):
    assert row0 % SB == 0 and DB % SB == 0 and col0 % H == 0
    rb = row0 // SB
    wh = H * HEAD
    col = lambda j: pl.BlockSpec((SB, wh), lambda b: (rb + b, col0 // H + j))
    cs = lambda j: pl.BlockSpec((None, CONV_W - 1, SB, wh), lambda b: (layer, 0, b, j))
    cw = lambda j: pl.BlockSpec((None, CONV_W, wh), lambda b: (layer, 0, j))
    st_spec = pl.BlockSpec((None, SB, H, HEAD, HEAD), lambda b: (layer, b, 0, 0, 0))
    s_shape, extra_in, extra_specs, aliases = _state_out(state, stacked_prev, 14)
    y, S = pl.pallas_call(
        functools.partial(_gd_sample_body, HEAD ** -0.5, H),
        grid=(DB // SB,),
        in_specs=[col(0), col(1), col(2), col(3),
                  pl.BlockSpec((SB, LANE), lambda b: (rb + b, 0)),
                  cs(0), cs(1), cs(2), cw(0), cw(1), cw(2),
                  pl.BlockSpec((None, 1, HEAD), lambda b: (layer, 0, 0)),
                  st_spec,
                  pl.BlockSpec(memory_space=pl.ANY)] + extra_specs,
        out_specs=[pl.BlockSpec((SB, wh), lambda b: (rb + b, 0)), st_spec],
        out_shape=[jax.ShapeDtypeStruct(y_all.shape, y_all.dtype), s_shape],
        scratch_shapes=[pltpu.VMEM((H, SB, HEAD), f32)],
        input_output_aliases={13: 0, **aliases},
        compiler_params=_params("arbitrary"), name="gd_sample")(
            proj, proj, proj, proj, gates, conv_state_t, conv_state_t, conv_state_t,
            lp["gd_conv_w"], lp["gd_conv_w"], lp["gd_conv_w"], lp["gd_norm_w"],
            state, y_all, *extra_in)
    return y, S


def kernel(x_prompt, x_sample, state_rg_h, state_rg_conv, state_hg_S, state_gd_S, state_gd_conv, norm_mix_w, norm_mlp_w, norm_final_w, w_in, rg_conv_w, rg_conv_b, rg_wa, rg_ba, rg_wx, rg_bx, rg_a_param, hg_lb_logits, hg_norm_w, gd_conv_w, gd_A_log, gd_dt_bias, gd_norm_w, w_br_rg, w_br_hg, w_br_gd, w_out, w_up, w_down):
    B, T, D = x_prompt.shape
    DB, DT, _ = x_sample.shape
    assert DT == 1
    depth = w_in.shape[0]
    RW = rg_ba.shape[-1]
    H = gd_A_log.shape[-1]
    PT = B * T
    M = PT + DB
    assert RW % HEAD == 0 and hg_norm_w.shape[-1] == HEAD and gd_norm_w.shape[-1] == HEAD
    HW = H * HEAD
    n_main = 2 * RW + 8 * HW
    merge_col0 = n_main + 2 * H
    assert w_in.shape[-1] == merge_col0 + 3 * D and n_main % 1024 == 0 and n_main % LANE == 0
    hg_col0 = 2 * RW // HEAD
    gd_col0 = hg_col0 + 4 * H

    row3 = lambda a: a.reshape(depth, 1, a.shape[-1])
    lane_pad = lambda a: row3(jnp.pad(a.astype(f32), ((0, 0), (0, LANE - a.shape[-1]))))
    lp = dict(rg_conv_w=rg_conv_w, rg_conv_b=row3(rg_conv_b), rg_wa=rg_wa, rg_wx=rg_wx, rg_ba=row3(rg_ba),
              rg_bx=row3(rg_bx), rg_a_param=row3(rg_a_param), gd_conv_w=gd_conv_w, gd_norm_w=row3(gd_norm_w))
    alog_pad, dtb_pad = lane_pad(gd_A_log), lane_pad(gd_dt_bias)
    hg_nw = row3(hg_norm_w)
    lb = row3(_lower_bounds(hg_lb_logits.astype(f32)))
    rg_conv_t = jnp.swapaxes(state_rg_conv, 1, 2)
    gd_conv_t = jnp.swapaxes(state_gd_conv, 1, 2)

    w_in_t = jnp.swapaxes(w_in, 1, 2)
    x, u = _join_norm(x_prompt.reshape(PT, D), x_sample.reshape(DB, D), norm_mix_w[0:1])

    p_states, s_states = [], []
    y_prompt = y_sample = None
    s_hgS = s_gdS = None
    for l in range(depth):
        proj = _gemm_wres(u, w_in_t, l, 0, n_main, 1024, name="in_proj", w_is_nk=True, tm_cands=BIG_TM)
        gates = _gd_gate_proj(u, w_in_t, n_main // LANE, alog_pad, dtb_pad, PT, H, l)

        y_rg, p_h = _rg_prompt(proj, B, T, M, lp, l)
        y_rg, s_h = _rg_sample(proj, y_rg, PT, DB, rg_conv_t, state_rg_h, lp, l)
        y_hg, p_hgS = _hg_prompt(proj, B, T, M, H, hg_col0, lb, hg_nw, l)
        y_hg, s_hgS = _hg_sample(proj, y_hg, PT, DB, H, hg_col0, lb, hg_nw, state_hg_S, s_hgS, l)
        y_gd, p_gdS = _gd_prompt(proj, gates, B, T, M, H, gd_col0, lp, l)
        y_gd, s_gdS = _gd_sample(proj, gates, y_gd, PT, DB, H, gd_col0, gd_conv_t, state_gd_S, s_gdS, lp, l)

        gq = gd_col0 * HEAD
        tail = lambda c0, w: jnp.stack([lax.slice(proj, (b * T + T - (CONV_W - 1), c0), (b * T + T, c0 + w))
                                        for b in range(B)], axis=0)
        last = lambda c0, w: lax.slice(proj, (PT, c0), (M, c0 + w))[:, None, :]
        p_states.append((p_h, tail(0, RW), p_hgS, p_gdS, tail(gq, 3 * HW)))
        s_states.append((s_h,
                         jnp.concatenate([state_rg_conv[l][:, 1:], last(0, RW)], axis=1),
                         None, None,
                         jnp.concatenate([state_gd_conv[l][:, 1:], last(gq, 3 * HW)], axis=1)))

        mixed = _mix(u, y_rg, y_hg, y_gd, w_in_t, w_br_rg, w_br_hg, w_br_gd, l, merge_col0)
        x, hmid = _proj_add_norm(mixed, w_out, l, x, norm_mlp_w[l:l + 1])
        hh = _gemm_wres(hmid, w_up, l, 0, w_up.shape[-1], 1024, epi=lambda a: jnp.square(jnp.maximum(a, 0.0)),
                        out_dtype=bf16, name="mlp_up", tm_cands=BIG_TM)
        x = _gemm_wres(hh, w_down, l, 0, D, 512, name="mlp_down", single_buffer_w=True, res=x)
        if l + 1 < depth:
            u = _rms_norm(x, norm_mix_w[l + 1:l + 2], bf16)
        else:
            y_prompt, y_sample = _norm_split(x, norm_final_w.reshape(1, D), PT)

    def stack(sts, j, like):
        return jnp.stack([s[j] for s in sts], axis=0).astype(like.dtype)

    return (y_prompt.reshape(B, T, D), y_sample.reshape(DB, DT, D),
            stack(p_states, 0, state_rg_h), stack(p_states, 1, state_rg_conv), stack(p_states, 2, state_hg_S),
            stack(p_states, 3, state_gd_S), stack(p_states, 4, state_gd_conv),
            stack(s_states, 0, state_rg_h), stack(s_states, 1, state_rg_conv), s_hgS.astype(state_hg_S.dtype),
            s_gdS.astype(state_gd_S.dtype), stack(s_states, 4, state_gd_conv))
```
